```python
import math
import jax
import jax.numpy as jnp
from jax import lax
import numpy as np

D_MODEL = 1024
BATCH = 16
SEQ = 2048
DEPTH = 4
DEC_BATCH = 128
DEC_SEQ = 8
PAST_LEN = 8192
PAGE_SIZE = 128

N_MIXERS = 3
N_RWKV = (DEPTH + 2) // 3
N_MLA = (DEPTH + 1) // 3
N_MAMBA = DEPTH // 3

RW_HEAD = 64
RW_HEADS = D_MODEL // RW_HEAD
RW_DECAY_LORA = 64
RW_AAA_LORA = 64
RW_MV_LORA = 32
RW_GATE_LORA = 160
RW_LNX_EPS = 64e-5

MLA_HEADS = 16
MLA_Q_LORA = 512
MLA_KV_LORA = 256
MLA_NOPE = 64
MLA_ROPE = 32
MLA_V = 64
MLA_SCALE = 1.0 / math.sqrt(MLA_NOPE + MLA_ROPE)
ROPE_THETA = 10000.0
Q_BLOCK = 128

MB_INNER = 2 * D_MODEL
MB_HEAD = 64
MB_HEADS = MB_INNER // MB_HEAD
MB_GROUPS = 4
MB_STATE = 128
MB_CONV = 4
MB_CONV_DIM = MB_INNER + 2 * MB_GROUPS * MB_STATE
MB_IN_DIM = MB_INNER + MB_CONV_DIM + MB_HEADS
MB_CHUNK = 128

FFN_HIDDEN = 4 * D_MODEL
NORM_EPS = 1e-6
F32 = jnp.float32

kernel_name = 'hybrid_rwkv7_mla_mamba2_decode_step'


def rmsnorm(x, g):
    xf = x.astype(F32)
    y = xf * lax.rsqrt(jnp.mean(xf * xf, axis=-1, keepdims=True) + NORM_EPS)
    return (y * g.astype(F32)).astype(x.dtype)


def rope(x, pos):
    half = MLA_ROPE // 2
    inv = ROPE_THETA ** (-jnp.arange(half, dtype=F32) / half)
    ang = pos.astype(F32)[:, None] * inv[None, :]
    shape = (pos.shape[0],) + (1,) * (x.ndim - 3) + (half,)
    cos = jnp.cos(ang).reshape(shape).astype(x.dtype)
    sin = jnp.sin(ang).reshape(shape).astype(x.dtype)
    x1, x2 = x[..., :half], x[..., half:]
    return jnp.concatenate([x1 * cos - x2 * sin, x2 * cos + x1 * sin], axis=-1)


def sqrelu_ffn(x, w1, w2):
    return jnp.square(jax.nn.relu(x @ w1)) @ w2


def _wkv_step(S, inp):
    r_t, w_t, k_t, v_t, a_t, b_t = inp
    sa = jnp.einsum('bhij,bhj->bhi', S, a_t)
    S = S * w_t[:, :, None, :] + sa[..., None] * b_t[:, :, None, :] + v_t[..., None] * k_t[:, :, None, :]
    return S, jnp.einsum('bhij,bhj->bhi', S, r_t)


def rwkv_mix(x, shift, wkv, v_first, vres, mu, w_rkv, w0, w1, w2, a0, a1, a2,
             g1, g2, k_k, k_a, r_k, lnx_w, lnx_b, w_o):
    b, t, c = x.shape
    hd = (b, t, RW_HEADS, RW_HEAD)
    x_prev = jnp.concatenate([shift[:, None].astype(x.dtype), x[:, :-1]], axis=1)
    xm = x[None] + (x_prev - x)[None] * mu[:, None, None, :]
    rkv = jnp.einsum('pbtc,pcd->pbtd', xm[:3], w_rkv)
    r, k, v = rkv[0], rkv[1], rkv[2]
    w_log = -jax.nn.softplus(-(w0 + jnp.tanh(xm[3] @ w1) @ w2)) - 0.5
    decay = jnp.exp(-jnp.exp(w_log.astype(F32)))
    if vres is None:
        v_first = v
    else:
        v0, v1, v2 = vres
        v = v + (v_first - v) * jax.nn.sigmoid(v0 + (xm[2] @ v1) @ v2)
    a = jax.nn.sigmoid(a0 + (xm[4] @ a1) @ a2)
    g = jax.nn.sigmoid(xm[5] @ g1) @ g2
    kk = (k * k_k).reshape(hd).astype(F32)
    kk = kk / jnp.maximum(jnp.linalg.norm(kk, axis=-1, keepdims=True), 1e-12)
    k = k * (1.0 + (a - 1.0) * k_a)
    rh, kh, vh, ah = [z.reshape(hd).astype(F32) for z in (r, k, v, a)]
    seq = tuple(jnp.moveaxis(z, 1, 0) for z in (rh, decay.reshape(hd), kh, vh, -kk, kk * ah))
    S, y = lax.scan(_wkv_step, wkv.astype(F32), seq)
    y = jnp.moveaxis(y, 0, 1)
    mean = jnp.mean(y, axis=-1, keepdims=True)
    var = jnp.mean(jnp.square(y - mean), axis=-1, keepdims=True)
    y = (y - mean) * lax.rsqrt(var + RW_LNX_EPS) * lnx_w.reshape(RW_HEADS, RW_HEAD) \
        + lnx_b.reshape(RW_HEADS, RW_HEAD)
    y = y + jnp.sum(rh * kh * r_k, axis=-1, keepdims=True) * vh
    out = (y.reshape(b, t, c).astype(x.dtype) * g) @ w_o
    return out, S, x[:, -1], v_first


def mla_project(x, pos, w_in, q_norm, kv_norm, w_uq, qn_norm, qr_norm, kr_norm):
    h = x @ w_in
    q_a = h[..., :MLA_Q_LORA]
    c_raw = h[..., MLA_Q_LORA:MLA_Q_LORA + MLA_KV_LORA]
    kp_raw = h[..., MLA_Q_LORA + MLA_KV_LORA:]
    q = jnp.einsum('btl,lhd->bthd', rmsnorm(q_a, q_norm), w_uq)
    q_nope = rmsnorm(q[..., :MLA_NOPE], qn_norm)
    q_pe = rope(rmsnorm(q[..., MLA_NOPE:], qr_norm), pos)
    c = rmsnorm(c_raw, kv_norm)
    k_pe = rope(rmsnorm(kp_raw, kr_norm), pos)
    return q_nope, q_pe, c, k_pe


def mla_keys(c, w_uk, kn_norm):
    return rmsnorm(jnp.einsum('...r,rhd->...hd', c, w_uk), kn_norm)


def mla_attend(q_nope, q_pe, k_nope, k_pe, c, mask, w_uv):
    s = jnp.einsum('bqhd,bkhd->bhqk', q_nope, k_nope) + jnp.einsum('bqhd,bkd->bhqk', q_pe, k_pe)
    s = jnp.where(mask, s.astype(F32) * MLA_SCALE, -jnp.inf)
    p = jax.nn.softmax(s, axis=-1).astype(c.dtype)
    o_lat = jnp.einsum('bhqk,bkr->bqhr', p, c)
    return jnp.einsum('bqhr,rhd->bqhd', o_lat, w_uv)


def mla_prompt(x, w_in, q_norm, kv_norm, w_uq, w_uk, w_uv, qn_norm, qr_norm, kn_norm, kr_norm, w_o):
    b, t, _ = x.shape
    pos = jnp.arange(t)
    qn, qp, c, kp = mla_project(x, pos, w_in, q_norm, kv_norm, w_uq, qn_norm, qr_norm, kr_norm)
    kn = mla_keys(c, w_uk, kn_norm)
    nb = t // Q_BLOCK
    qn_b = jnp.swapaxes(qn.reshape(b, nb, Q_BLOCK, MLA_HEADS, MLA_NOPE), 0, 1)
    qp_b = jnp.swapaxes(qp.reshape(b, nb, Q_BLOCK, MLA_HEADS, MLA_ROPE), 0, 1)
    kpos = jnp.arange(t)

    def block(args):
        qn_i, qp_i, i = args
        qpos = i * Q_BLOCK + jnp.arange(Q_BLOCK)
        return mla_attend(qn_i, qp_i, kn, kp, c, kpos[None, :] <= qpos[:, None], w_uv)

    o = lax.map(block, (qn_b, qp_b, jnp.arange(nb)))
    o = jnp.swapaxes(o, 0, 1).reshape(b, t, MLA_HEADS * MLA_V)
    return o @ w_o, c, kp


def mla_sample(x, pool_c, pool_kp, page_table, w_in, q_norm, kv_norm, w_uq, w_uk, w_uv,
               qn_norm, qr_norm, kn_norm, kr_norm, w_o):
    b, t, _ = x.shape
    pos = PAST_LEN + jnp.arange(t)
    qn, qp, c, kp = mla_project(x, pos, w_in, q_norm, kv_norm, w_uq, qn_norm, qr_norm, kr_norm)
    n_past = page_table.shape[1] * PAGE_SIZE
    mask = jnp.concatenate([jnp.ones((t, n_past), bool), jnp.tril(jnp.ones((t, t), bool))], axis=1)

    def one_seq(args):
        pt, qn_s, qp_s, c_s, kp_s = args
        c_all = jnp.concatenate([pool_c[pt].reshape(n_past, MLA_KV_LORA), c_s.astype(pool_c.dtype)], axis=0)
        kp_all = jnp.concatenate([pool_kp[pt].reshape(n_past, MLA_ROPE), kp_s.astype(pool_kp.dtype)], axis=0)
        kn_all = mla_keys(c_all, w_uk, kn_norm)
        return mla_attend(qn_s[None], qp_s[None], kn_all[None], kp_all[None], c_all[None], mask, w_uv)[0]

    o = lax.map(one_seq, (page_table, qn, qp, c, kp))
    return o.reshape(b, t, MLA_HEADS * MLA_V) @ w_o, c, kp


def ssd_scan(xh, dt, A, Bm, Cm, h0, chunk):
    b, t = xh.shape[0], xh.shape[1]
    nc = t // chunk
    r = MB_HEADS // MB_GROUPS

    def to_chunks(z):
        return jnp.moveaxis(z.reshape((b, nc, chunk) + z.shape[2:]), 1, 0)

    xc = to_chunks(xh.astype(F32).reshape(b, t, MB_GROUPS, r, MB_HEAD))
    dtc = to_chunks(dt.astype(F32).reshape(b, t, MB_GROUPS, r))
    Bc = to_chunks(Bm.astype(F32))
    Cc = to_chunks(Cm.astype(F32))
    a_gr = A.reshape(MB_GROUPS, r)
    causal = jnp.tril(jnp.ones((chunk, chunk), bool))[None, :, :, None, None]

    def step(h, inp):
        x_q, dt_q, b_q, c_q = inp
        acum = jnp.cumsum(dt_q * a_gr, axis=1)
        seg = acum[:, :, None] - acum[:, None, :]
        lmat = jnp.exp(jnp.where(causal, seg, -jnp.inf))
        xdt = x_q * dt_q[..., None]
        cb = jnp.einsum('bign,bjgn->bijg', c_q, b_q)
        y = jnp.einsum('bijgr,bjgrp->bigrp', cb[..., None] * lmat, xdt)
        y = y + jnp.einsum('bign,bgrpn->bigrp', c_q, h) * jnp.exp(acum)[..., None]
        to_end = jnp.exp(acum[:, -1:] - acum)
        h = h * jnp.exp(acum[:, -1])[..., None, None] \
            + jnp.einsum('bjgn,bjgrp->bgrpn', b_q, xdt * to_end[..., None])
        return h, y

    h, ys = lax.scan(step, h0.astype(F32).reshape(b, MB_GROUPS, r, MB_HEAD, MB_STATE), (xc, dtc, Bc, Cc))
    y = jnp.moveaxis(ys, 0, 1).reshape(b, t, MB_HEADS, MB_HEAD)
    return y, h.reshape(b, MB_HEADS, MB_HEAD, MB_STATE)


def mamba_mix(x, conv_state, ssm_state, w_in, conv_w, conv_b, dt_bias, a_log, d_skip, norm_w, w_o):
    b, t, _ = x.shape
    zxbcdt = x @ w_in
    z = zxbcdt[..., :MB_INNER]
    xbc = zxbcdt[..., MB_INNER:MB_INNER + MB_CONV_DIM]
    dt_raw = zxbcdt[..., MB_INNER + MB_CONV_DIM:]
    xpad = jnp.concatenate([conv_state.astype(xbc.dtype), xbc], axis=1)
    conv = conv_b + sum(xpad[:, j:j + t] * conv_w[j] for j in range(MB_CONV))
    xbc = jax.nn.silu(conv)
    new_conv = xpad[:, t:]
    gn = MB_GROUPS * MB_STATE
    xs = xbc[..., :MB_INNER].reshape(b, t, MB_HEADS, MB_HEAD)
    Bm = xbc[..., MB_INNER:MB_INNER + gn].reshape(b, t, MB_GROUPS, MB_STATE)
    Cm = xbc[..., MB_INNER + gn:].reshape(b, t, MB_GROUPS, MB_STATE)
    dt = jax.nn.softplus(dt_raw.astype(F32) + dt_bias.astype(F32))
    A = -jnp.exp(a_log.astype(F32))
    y, h = ssd_scan(xs, dt, A, Bm, Cm, ssm_state, math.gcd(t, MB_CHUNK))
    y = y + d_skip.astype(F32)[:, None] * xs.astype(F32)
    yz = (y.reshape(b, t, MB_INNER) * jax.nn.silu(z.astype(F32))).reshape(b, t, MB_GROUPS, MB_INNER // MB_GROUPS)
    yz = rmsnorm(yz, norm_w.reshape(MB_GROUPS, MB_INNER // MB_GROUPS)).reshape(b, t, MB_INNER)
    return yz.astype(x.dtype) @ w_o, new_conv, h


def setup_inputs(seed: int = 0) -> dict:
    key = jax.random.key(seed)
    ks = iter(jax.random.split(key, 64))
    C = D_MODEL
    n_pages = PAST_LEN // PAGE_SIZE
    n_pool = (DEC_BATCH * n_pages * 5) // 4
    nv = max(N_RWKV - 1, 0)

    def nrm(shape, scale):
        return scale * jax.random.normal(next(ks), shape, F32)

    def gain(shape):
        return 1.0 + 0.02 * jax.random.normal(next(ks), shape, F32)

    def unif(shape, lo, hi):
        return jax.random.uniform(next(ks), shape, F32, lo, hi)

    dt0 = jnp.exp(unif((N_MAMBA, MB_HEADS), math.log(1e-3), math.log(1e-1)))
    page_table = jax.random.permutation(next(ks), n_pool)[:DEC_BATCH * n_pages]
    page_table = page_table.reshape(DEC_BATCH, n_pages).astype(jnp.int32)
    return {
        'x_prompt': nrm((BATCH, SEQ, C), 1.0),
        'x_sample': nrm((DEC_BATCH, DEC_SEQ, C), 1.0),
        'cache_mla_ckv': nrm((N_MLA, n_pool, PAGE_SIZE, MLA_KV_LORA), 1.0),
        'cache_mla_kpe': nrm((N_MLA, n_pool, PAGE_SIZE, MLA_ROPE), 1.0),
        'state_rwkv_wkv': nrm((N_RWKV, DEC_BATCH, RW_HEADS, RW_HEAD, RW_HEAD), 0.5),
        'state_rwkv_shift': nrm((N_RWKV, DEC_BATCH, C), 1.0),
        'state_ssm': nrm((N_MAMBA, DEC_BATCH, MB_HEADS, MB_HEAD, MB_STATE), 0.5),
        'state_conv': nrm((N_MAMBA, DEC_BATCH, MB_CONV - 1, MB_CONV_DIM), 1.0),
        'page_table': page_table,
        'norm_mix': gain((DEPTH, C)),
        'norm_ffn': gain((DEPTH, C)),
        'ffn_w1': nrm((DEPTH, C, FFN_HIDDEN), C ** -0.5),
        'ffn_w2': nrm((DEPTH, FFN_HIDDEN, C), FFN_HIDDEN ** -0.5),
        'rwkv_mu': unif((N_RWKV, 6, C), 0.0, 1.0),
        'rwkv_w_rkv': nrm((N_RWKV, 3, C, C), C ** -0.5),
        'rwkv_w0': unif((N_RWKV, C), -6.0, 1.0),
        'rwkv_w1': nrm((N_RWKV, C, RW_DECAY_LORA), C ** -0.5),
        'rwkv_w2': nrm((N_RWKV, RW_DECAY_LORA, C), 0.1 * RW_DECAY_LORA ** -0.5),
        'rwkv_a0': nrm((N_RWKV, C), 0.1),
        'rwkv_a1': nrm((N_RWKV, C, RW_AAA_LORA), C ** -0.5),
        'rwkv_a2': nrm((N_RWKV, RW_AAA_LORA, C), 0.1 * RW_AAA_LORA ** -0.5),
        'rwkv_v0': nrm((nv, C), 0.1),
        'rwkv_v1': nrm((nv, C, RW_MV_LORA), C ** -0.5),
        'rwkv_v2': nrm((nv, RW_MV_LORA, C), 0.1 * RW_MV_LORA ** -0.5),
        'rwkv_g1': nrm((N_RWKV, C, RW_GATE_LORA), C ** -0.5),
        'rwkv_g2': nrm((N_RWKV, RW_GATE_LORA, C), RW_GATE_LORA ** -0.5),
        'rwkv_k_k': 0.85 + nrm((N_RWKV, C), 0.05),
        'rwkv_k_a': 1.0 + nrm((N_RWKV, C), 0.05),
        'rwkv_r_k': nrm((N_RWKV, RW_HEADS, RW_HEAD), 0.1),
        'rwkv_lnx_w': gain((N_RWKV, C)),
        'rwkv_lnx_b': nrm((N_RWKV, C), 0.01),
        'rwkv_w_o': nrm((N_RWKV, C, C), C ** -0.5),
        'mla_w_in': nrm((N_MLA, C, MLA_Q_LORA + MLA_KV_LORA + MLA_ROPE), C ** -0.5),
        'mla_q_norm': gain((N_MLA, MLA_Q_LORA)),
        'mla_kv_norm': gain((N_MLA, MLA_KV_LORA)),
        'mla_w_uq': nrm((N_MLA, MLA_Q_LORA, MLA_HEADS, MLA_NOPE + MLA_ROPE), MLA_Q_LORA ** -0.5),
        'mla_w_uk': nrm((N_MLA, MLA_KV_LORA, MLA_HEADS, MLA_NOPE), MLA_KV_LORA ** -0.5),
        'mla_w_uv': nrm((N_MLA, MLA_KV_LORA, MLA_HEADS, MLA_V), MLA_KV_LORA ** -0.5),
        'mla_qn_norm': gain((N_MLA, MLA_NOPE)),
        'mla_qr_norm': gain((N_MLA, MLA_ROPE)),
        'mla_kn_norm': gain((N_MLA, MLA_NOPE)),
        'mla_kr_norm': gain((N_MLA, MLA_ROPE)),
        'mla_w_o': nrm((N_MLA, MLA_HEADS * MLA_V, C), (MLA_HEADS * MLA_V) ** -0.5),
        'mamba_w_in': nrm((N_MAMBA, C, MB_IN_DIM), C ** -0.5),
        'mamba_conv_w': nrm((N_MAMBA, MB_CONV, MB_CONV_DIM), MB_CONV ** -0.5),
        'mamba_conv_b': nrm((N_MAMBA, MB_CONV_DIM), 0.01),
        'mamba_dt_bias': dt0 + jnp.log(-jnp.expm1(-dt0)),
        'mamba_a_log': jnp.log(unif((N_MAMBA, MB_HEADS), 1.0, 16.0)),
        'mamba_d': 1.0 + nrm((N_MAMBA, MB_HEADS), 0.1),
        'mamba_norm': gain((N_MAMBA, MB_INNER)),
        'mamba_w_o': nrm((N_MAMBA, MB_INNER, C), MB_INNER ** -0.5),
    }


def reference(x_prompt, x_sample, cache_mla_ckv, cache_mla_kpe, state_rwkv_wkv, state_rwkv_shift,
              state_ssm, state_conv, page_table, norm_mix, norm_ffn, ffn_w1, ffn_w2,
              rwkv_mu, rwkv_w_rkv, rwkv_w0, rwkv_w1, rwkv_w2, rwkv_a0, rwkv_a1, rwkv_a2,
              rwkv_v0, rwkv_v1, rwkv_v2, rwkv_g1, rwkv_g2, rwkv_k_k, rwkv_k_a, rwkv_r_k,
              rwkv_lnx_w, rwkv_lnx_b, rwkv_w_o, mla_w_in, mla_q_norm, mla_kv_norm, mla_w_uq,
              mla_w_uk, mla_w_uv, mla_qn_norm, mla_qr_norm, mla_kn_norm, mla_kr_norm, mla_w_o,
              mamba_w_in, mamba_conv_w, mamba_conv_b, mamba_dt_bias, mamba_a_log, mamba_d,
              mamba_norm, mamba_w_o):
    bp = x_prompt.shape[0]
    hp, hs = x_prompt, x_sample
    vf_p, vf_s = None, None
    ckv_p, kpe_p, ckv_s, kpe_s = [], [], [], []
    wkv_p, sh_p, wkv_s, sh_s = [], [], [], []
    ssm_p, conv_p, ssm_s, conv_s = [], [], [], []
    for i in range(DEPTH):
        kind, j = i % N_MIXERS, i // N_MIXERS
        xp = rmsnorm(hp, norm_mix[i])
        xs = rmsnorm(hs, norm_mix[i])
        if kind == 0:
            rw = (rwkv_mu[j], rwkv_w_rkv[j], rwkv_w0[j], rwkv_w1[j], rwkv_w2[j], rwkv_a0[j],
                  rwkv_a1[j], rwkv_a2[j], rwkv_g1[j], rwkv_g2[j], rwkv_k_k[j], rwkv_k_a[j],
                  rwkv_r_k[j], rwkv_lnx_w[j], rwkv_lnx_b[j], rwkv_w_o[j])
            vres = None if j == 0 else (rwkv_v0[j - 1], rwkv_v1[j - 1], rwkv_v2[j - 1])
            op, s_p, l_p, vf_p = rwkv_mix(xp, jnp.zeros((bp, D_MODEL), xp.dtype),
                                          jnp.zeros((bp, RW_HEADS, RW_HEAD, RW_HEAD), F32),
                                          vf_p, vres, *rw)
            os_, s_s, l_s, vf_s = rwkv_mix(xs, state_rwkv_shift[j], state_rwkv_wkv[j], vf_s, vres, *rw)
            wkv_p.append(s_p); sh_p.append(l_p); wkv_s.append(s_s); sh_s.append(l_s)
        elif kind == 1:
            ml = (mla_w_in[j], mla_q_norm[j], mla_kv_norm[j], mla_w_uq[j], mla_w_uk[j], mla_w_uv[j],
                  mla_qn_norm[j], mla_qr_norm[j], mla_kn_norm[j], mla_kr_norm[j], mla_w_o[j])
            op, c_p, k_p = mla_prompt(xp, *ml)
            os_, c_s, k_s = mla_sample(xs, cache_mla_ckv[j], cache_mla_kpe[j], page_table, *ml)
            ckv_p.append(c_p); kpe_p.append(k_p); ckv_s.append(c_s); kpe_s.append(k_s)
        else:
            mb = (mamba_w_in[j], mamba_conv_w[j], mamba_conv_b[j], mamba_dt_bias[j],
                  mamba_a_log[j], mamba_d[j], mamba_norm[j], mamba_w_o[j])
            op, cv_p, h_p = mamba_mix(xp, jnp.zeros((bp, MB_CONV - 1, MB_CONV_DIM), xp.dtype),
                                      jnp.zeros((bp, MB_HEADS, MB_HEAD, MB_STATE), F32), *mb)
            os_, cv_s, h_s = mamba_mix(xs, state_conv[j], state_ssm[j], *mb)
            ssm_p.append(h_p); conv_p.append(cv_p); ssm_s.append(h_s); conv_s.append(cv_s)
        hp = hp + op.astype(hp.dtype)
        hs = hs + os_.astype(hs.dtype)
        hp = hp + sqrelu_ffn(rmsnorm(hp, norm_ffn[i]), ffn_w1[i], ffn_w2[i]).astype(hp.dtype)
        hs = hs + sqrelu_ffn(rmsnorm(hs, norm_ffn[i]), ffn_w1[i], ffn_w2[i]).astype(hs.dtype)
    return (hp, hs,
            jnp.stack(ckv_p), jnp.stack(kpe_p), jnp.stack(ckv_s), jnp.stack(kpe_s),
            jnp.stack(wkv_p), jnp.stack(sh_p), jnp.stack(wkv_s), jnp.stack(sh_s),
            jnp.stack(ssm_p), jnp.stack(conv_p), jnp.stack(ssm_s), jnp.stack(conv_s))
```

```python
import functools
import math

import jax
import jax.numpy as jnp
from jax import lax
from jax.experimental import pallas as pl
from jax.experimental.pallas import tpu as pltpu

F32 = jnp.float32
BF16 = jnp.bfloat16

D_MODEL = 1024
NORM_EPS = 1e-6

RW_HEAD = 64
RW_HEADS = D_MODEL // RW_HEAD
RW_LNX_EPS = 64e-5

MLA_HEADS = 16
MLA_Q_LORA = 512
MLA_KV_LORA = 256
MLA_NOPE = 64
MLA_ROPE = 32
MLA_V = 64
MLA_SCALE = 1.0 / math.sqrt(MLA_NOPE + MLA_ROPE)
ROPE_THETA = 10000.0
PAGE_SIZE = 128

MB_INNER = 2 * D_MODEL
MB_HEAD = 64
MB_HEADS = MB_INNER // MB_HEAD
MB_GROUPS = 4
MB_STATE = 128
MB_CONV = 4
MB_CONV_DIM = MB_INNER + 2 * MB_GROUPS * MB_STATE
MB_CHUNK = 128

FFN_HIDDEN = 4 * D_MODEL

VMEM_LIMIT_BYTES = 56 * 2**20


def _params(*sem):
    return pltpu.CompilerParams(dimension_semantics=sem, vmem_limit_bytes=VMEM_LIMIT_BYTES)


def _dot(a, b):
    return jnp.dot(a.astype(BF16), b.astype(BF16), preferred_element_type=F32)


def _dot_nt(a, b):
    return lax.dot_general(a.astype(BF16), b.astype(BF16), (((1,), (1,)), ((), ())),
                           preferred_element_type=F32)


def _dot_tn(a, b):
    return lax.dot_general(a.astype(BF16), b.astype(BF16), (((0,), (0,)), ((), ())),
                           preferred_element_type=F32)


def _split3(a):
    hi = a.astype(BF16)
    r1 = a - hi.astype(F32)
    mid = r1.astype(BF16)
    lo = (r1 - mid.astype(F32)).astype(BF16)
    return hi, mid, lo


def _dot_exact_lhs(ones, x):
    hi, mid, lo = _split3(x)
    o = ones.astype(BF16)
    return (jnp.dot(o, hi, preferred_element_type=F32)
            + jnp.dot(o, mid, preferred_element_type=F32)
            + jnp.dot(o, lo, preferred_element_type=F32))


def _dot_exact_rhs(x, ones):
    hi, mid, lo = _split3(x)
    o = ones.astype(BF16)
    return (jnp.dot(hi, o, preferred_element_type=F32)
            + jnp.dot(mid, o, preferred_element_type=F32)
            + jnp.dot(lo, o, preferred_element_type=F32))


def _rms(x, g):
    return x * lax.rsqrt(jnp.mean(x * x, axis=-1, keepdims=True) + NORM_EPS) * g


def _softplus(z):
    return jnp.maximum(z, 0.0) + jnp.log(1.0 + jnp.exp(-jnp.abs(z)))


def _sigmoid(z):
    return 1.0 / (1.0 + jnp.exp(-z))


def _silu(z):
    return z * _sigmoid(z)


def _rmsnorm_kernel(x_ref, g_ref, o_ref):
    o_ref[...] = _rms(x_ref[...], g_ref[...])


def rmsnorm_rows(x, g, tm):
    m, c = x.shape
    return pl.pallas_call(
        _rmsnorm_kernel,
        grid=(m // tm,),
        in_specs=[pl.BlockSpec((tm, c), lambda i: (i, 0)), pl.BlockSpec((1, c), lambda i: (0, 0))],
        out_specs=pl.BlockSpec((tm, c), lambda i: (i, 0)),
        out_shape=jax.ShapeDtypeStruct((m, c), F32),
        compiler_params=_params("parallel"),
        name="rmsnorm_rows",
    )(x, g.reshape(1, c))


def _linear_res_kernel(x_ref, w_ref, h_ref, o_ref):
    o_ref[...] = h_ref[...] + _dot(x_ref[...], w_ref[...])


def _gated_linear_res_kernel(x_ref, gate_ref, w_ref, h_ref, o_ref):
    o_ref[...] = h_ref[...] + _dot(x_ref[...] * gate_ref[...], w_ref[...])


def linear_res(x, w, h, tm, gate=None):
    m, k = x.shape
    n = w.shape[1]
    row = lambda width: pl.BlockSpec((tm, width), lambda i: (i, 0))
    wspec = pl.BlockSpec((k, n), lambda i: (0, 0))
    if gate is None:
        kern, args, specs = _linear_res_kernel, (x, w, h), [row(k), wspec, row(n)]
    else:
        kern, args, specs = _gated_linear_res_kernel, (x, gate, w, h), [row(k), row(k), wspec, row(n)]
    return pl.pallas_call(
        kern,
        grid=(m // tm,),
        in_specs=specs,
        out_specs=row(n),
        out_shape=jax.ShapeDtypeStruct((m, n), F32),
        compiler_params=_params("parallel"),
        name="linear_res",
    )(*args)


def _norm_linear_kernel(x_ref, g_ref, w_ref, o_ref, xn_ref):
    @pl.when(pl.program_id(1) == 0)
    def _():
        xn_ref[...] = _rms(x_ref[...], g_ref[...]).astype(BF16)

    o_ref[...] = jnp.dot(xn_ref[...], w_ref[...], preferred_element_type=F32)


def norm_linear(x, g, w, tm, tn):
    m, k = x.shape
    n = w.shape[1]
    return pl.pallas_call(
        _norm_linear_kernel,
        grid=(m // tm, n // tn),
        in_specs=[pl.BlockSpec((tm, k), lambda i, j: (i, 0)),
                  pl.BlockSpec((1, k), lambda i, j: (0, 0)),
                  pl.BlockSpec((k, tn), lambda i, j: (0, j))],
        out_specs=pl.BlockSpec((tm, tn), lambda i, j: (i, j)),
        out_shape=jax.ShapeDtypeStruct((m, n), F32),
        scratch_shapes=[pltpu.VMEM((tm, k), BF16)],
        compiler_params=_params("parallel", "arbitrary"),
        name="norm_linear",
    )(x, g.reshape(1, k), w)


def _ffn_kernel(h_ref, g_ref, w1_ref, w2_ref, o_ref, xn_ref, acc_ref):
    j = pl.program_id(1)

    @pl.when(j == 0)
    def _():
        xn_ref[...] = _rms(h_ref[...], g_ref[...]).astype(BF16)
        acc_ref[...] = jnp.zeros_like(acc_ref)

    u = jnp.dot(xn_ref[...], w1_ref[...], preferred_element_type=F32)
    u = jnp.square(jnp.maximum(u, 0.0))
    acc_ref[...] += jnp.dot(u.astype(BF16), w2_ref[...], preferred_element_type=F32)

    @pl.when(j == pl.num_programs(1) - 1)
    def _():
        o_ref[...] = h_ref[...] + acc_ref[...]


def ffn_res(h, g, w1, w2, tm, th):
    m, c = h.shape
    hid = w1.shape[1]
    return pl.pallas_call(
        _ffn_kernel,
        grid=(m // tm, hid // th),
        in_specs=[pl.BlockSpec((tm, c), lambda i, j: (i, 0)),
                  pl.BlockSpec((1, c), lambda i, j: (0, 0)),
                  pl.BlockSpec((c, th), lambda i, j: (0, j)),
                  pl.BlockSpec((th, c), lambda i, j: (j, 0))],
        out_specs=pl.BlockSpec((tm, c), lambda i, j: (i, 0)),
        out_shape=jax.ShapeDtypeStruct((m, c), F32),
        scratch_shapes=[pltpu.VMEM((tm, c), BF16), pltpu.VMEM((tm, c), F32)],
        compiler_params=_params("parallel", "arbitrary"),
        name="ffn_res",
    )(h, g.reshape(1, c), w1, w2)


def _rwkv_proj_kernel(has_vres, *refs):
    if has_vres:
        (xn_ref, xp_ref, mu_ref, wrkv_ref, w0_ref, w1_ref, w2_ref, a0_ref, a1_ref, a2_ref,
         g1_ref, g2_ref, v0_ref, v1_ref, v2_ref,
         r_ref, lw_ref, k_ref, v_ref, a_ref, g_ref, vg_ref) = refs
    else:
        (xn_ref, xp_ref, mu_ref, wrkv_ref, w0_ref, w1_ref, w2_ref, a0_ref, a1_ref, a2_ref,
         g1_ref, g2_ref,
         r_ref, lw_ref, k_ref, v_ref, a_ref, g_ref) = refs
    x = xn_ref[...]
    dx = xp_ref[...] - x
    xm = [(x + dx * mu_ref[p:p + 1, :]).astype(BF16) for p in range(6)]

    def heads_out(ref, val):
        for h in range(RW_HEADS):
            ref[h] = val[:, h * RW_HEAD:(h + 1) * RW_HEAD]

    heads_out(r_ref, jnp.dot(xm[0], wrkv_ref[0], preferred_element_type=F32))
    heads_out(k_ref, jnp.dot(xm[1], wrkv_ref[1], preferred_element_type=F32))
    heads_out(v_ref, jnp.dot(xm[2], wrkv_ref[2], preferred_element_type=F32))
    wpre = w0_ref[...] + _dot(jnp.tanh(jnp.dot(xm[3], w1_ref[...], preferred_element_type=F32)), w2_ref[...])
    w_log = -_softplus(-wpre) - 0.5
    heads_out(lw_ref, -jnp.exp(w_log))
    a = _sigmoid(a0_ref[...] + _dot(jnp.dot(xm[4], a1_ref[...], preferred_element_type=F32), a2_ref[...]))
    heads_out(a_ref, a)
    g_ref[...] = _dot(_sigmoid(jnp.dot(xm[5], g1_ref[...], preferred_element_type=F32)), g2_ref[...])
    if has_vres:
        vg = _sigmoid(v0_ref[...] + _dot(jnp.dot(xm[2], v1_ref[...], preferred_element_type=F32), v2_ref[...]))
        heads_out(vg_ref, vg)


def rwkv_proj(xn, xprev, p, tm):
    m, c = xn.shape
    has_vres = "v1" in p
    row = pl.BlockSpec((tm, c), lambda i: (i, 0))
    full = lambda arr: pl.BlockSpec(arr.shape, lambda i: (0,) * arr.ndim)
    head = pl.BlockSpec((RW_HEADS, tm, RW_HEAD), lambda i: (0, i, 0))
    names = ["mu", "w_rkv", "w0", "w1", "w2", "a0", "a1", "a2", "g1", "g2"]
    if has_vres:
        names += ["v0", "v1", "v2"]
    weights = [p[n] for n in names]
    hm = jax.ShapeDtypeStruct((RW_HEADS, m, RW_HEAD), F32)
    out_shape = [hm] * 5 + [jax.ShapeDtypeStruct((m, c), F32)] + ([hm] if has_vres else [])
    out_specs = [head] * 5 + [row] + ([head] if has_vres else [])
    return pl.pallas_call(
        functools.partial(_rwkv_proj_kernel, has_vres),
        grid=(m // tm,),
        in_specs=[row, row] + [full(w) for w in weights],
        out_specs=out_specs,
        out_shape=out_shape,
        compiler_params=_params("parallel"),
        name="rwkv_proj",
    )(xn, xprev, *weights)


def _wkv_kernel(has_vres, hb, ln, *refs):
    if has_vres:
        (r_ref, lw_ref, k_ref, v_ref, a_ref, vf_ref, vg_ref, s0_ref,
         kk_ref, ka_ref, rk_ref, lnw_ref, lnb_ref, y_ref, sout_ref, s_ref) = refs
    else:
        (r_ref, lw_ref, k_ref, v_ref, a_ref, s0_ref,
         kk_ref, ka_ref, rk_ref, lnw_ref, lnb_ref, y_ref, sout_ref, s_ref) = refs
    c = pl.program_id(2)
    rows = hb * ln
    n = RW_HEAD

    @pl.when(c == 0)
    def _():
        s_ref[...] = s0_ref[0]

    r = r_ref[...]
    lw = lw_ref[...]
    k = k_ref[...]
    v = v_ref[...]
    a = a_ref[...]
    if has_vres:
        v = v + (vf_ref[...] - v) * vg_ref[...]
    kk = k * kk_ref[...]
    kk = kk / jnp.maximum(jnp.sqrt(jnp.sum(kk * kk, axis=-1, keepdims=True)), 1e-12)
    k2 = k * (1.0 + (a - 1.0) * ka_ref[...])
    av = -kk
    bv = kk * a

    ri = lax.broadcasted_iota(jnp.int32, (rows, rows), 0)
    ci = lax.broadcasted_iota(jnp.int32, (rows, rows), 1)
    same = (ri // ln) == (ci // ln)
    incl = same & (ci <= ri)
    strict = same & (ci < ri)

    flat = lambda z: z.reshape(rows, n)
    cl = _dot_exact_lhs(jnp.where(incl, 1.0, 0.0), flat(lw))
    cl3 = cl.reshape(hb, ln, n)
    cl_end = cl3[:, ln - 1:ln, :]
    g_in = jnp.exp(cl3)
    g_ex = jnp.exp(cl3 - lw)
    g_inv = jnp.exp(-cl3)
    g_end = jnp.exp(cl_end - cl3)

    rt = r * g_in
    at = av * g_ex
    lhs = jnp.concatenate([flat(at), flat(rt)], axis=0)
    rhs = jnp.concatenate([flat(bv * g_inv), flat(k2 * g_inv)], axis=0)
    aa = _dot_nt(lhs, rhs)
    n_ab = jnp.where(strict, aa[:rows, :rows], 0.0)
    a_ak = jnp.where(strict, aa[:rows, rows:], 0.0)
    a_rb = jnp.where(incl, aa[rows:, :rows], 0.0)
    a_rk = jnp.where(incl, aa[rows:, rows:], 0.0)

    tinv = jnp.where(ri == ci, 1.0, 0.0) + n_ab
    npow = n_ab
    for _ in range(int(math.log2(ln)) - 1):
        npow = _dot(npow, npow)
        tinv = tinv + _dot(npow, tinv)

    v2 = flat(v)
    x0 = jnp.concatenate([_dot_nt(at[h], s_ref[h]) for h in range(hb)], axis=0)
    u = _dot(tinv, x0 + _dot(a_ak, v2))
    y = jnp.concatenate([_dot_nt(rt[h], s_ref[h]) for h in range(hb)], axis=0)
    y = y + _dot(a_rb, u) + _dot(a_rk, v2)

    u3 = u.reshape(hb, ln, n)
    b_end = bv * g_end
    k_end = k2 * g_end
    for h in range(hb):
        zs = jnp.concatenate([u3[h], v[h]], axis=0)
        ws = jnp.concatenate([b_end[h], k_end[h]], axis=0)
        s_ref[h] = s_ref[h] * jnp.exp(cl_end[h]) + _dot_tn(zs, ws)

    y3 = y.reshape(hb, ln, n)
    mean = jnp.mean(y3, axis=-1, keepdims=True)
    yc = y3 - mean
    var = jnp.mean(yc * yc, axis=-1, keepdims=True)
    y3 = yc * lax.rsqrt(var + RW_LNX_EPS) * lnw_ref[...] + lnb_ref[...]
    y3 = y3 + jnp.sum(r * k2 * rk_ref[...], axis=-1, keepdims=True) * v
    y_ref[...] = y3

    @pl.when(c == pl.num_programs(2) - 1)
    def _():
        sout_ref[0] = s_ref[...]


def wkv_scan(r, lw, k, v, a, s0, p, batch, seq, hb, ln, vres=None):
    nc = seq // ln
    has_vres = vres is not None
    tok = pl.BlockSpec((hb, ln, RW_HEAD), lambda b, q, c: (q, b * nc + c, 0))
    par = pl.BlockSpec((hb, 1, RW_HEAD), lambda b, q, c: (q, 0, 0))
    st = pl.BlockSpec((1, hb, RW_HEAD, RW_HEAD), lambda b, q, c: (b, q, 0, 0))
    seqs = [r, lw, k, v, a] + (list(vres) if has_vres else [])
    pars = [p[nm].reshape(RW_HEADS, 1, RW_HEAD) for nm in ("k_k", "k_a", "r_k", "lnx_w", "lnx_b")]
    return pl.pallas_call(
        functools.partial(_wkv_kernel, has_vres, hb, ln),
        grid=(batch, RW_HEADS // hb, nc),
        in_specs=[tok] * len(seqs) + [st] + [par] * 5,
        out_specs=[tok, st],
        out_shape=[jax.ShapeDtypeStruct(r.shape, F32), jax.ShapeDtypeStruct(s0.shape, F32)],
        scratch_shapes=[pltpu.VMEM((hb, RW_HEAD, RW_HEAD), F32)],
        compiler_params=_params("parallel", "parallel", "arbitrary"),
        name="wkv_scan",
    )(*seqs, s0, *pars)


def prep_rwkv(w, j, i):
    row = lambda z: z.reshape(1, -1).astype(F32)
    p = {
        "norm_mix": w["norm_mix"][i],
        "mu": w["rwkv_mu"][j],
        "w_rkv": w["rwkv_w_rkv"][j].astype(BF16),
        "w0": row(w["rwkv_w0"][j]), "w1": w["rwkv_w1"][j].astype(BF16), "w2": w["rwkv_w2"][j].astype(BF16),
        "a0": row(w["rwkv_a0"][j]), "a1": w["rwkv_a1"][j].astype(BF16), "a2": w["rwkv_a2"][j].astype(BF16),
        "g1": w["rwkv_g1"][j].astype(BF16), "g2": w["rwkv_g2"][j].astype(BF16),
        "k_k": w["rwkv_k_k"][j], "k_a": w["rwkv_k_a"][j], "r_k": w["rwkv_r_k"][j],
        "lnx_w": w["rwkv_lnx_w"][j], "lnx_b": w["rwkv_lnx_b"][j],
        "w_o": w["rwkv_w_o"][j].astype(BF16),
    }
    if j > 0:
        p["v0"] = row(w["rwkv_v0"][j - 1])
        p["v1"] = w["rwkv_v1"][j - 1].astype(BF16)
        p["v2"] = w["rwkv_v2"][j - 1].astype(BF16)
    return p


def rwkv_layer(h, shift, s0, vfirst, p, batch, seq, tm, hb, ln):
    m, c = h.shape
    xn = rmsnorm_rows(h, p["norm_mix"], tm)
    xn3 = xn.reshape(batch, seq, c)
    xprev = jnp.concatenate([shift[:, None, :], xn3[:, :-1]], axis=1).reshape(m, c)
    outs = rwkv_proj(xn, xprev, p, tm)
    r, lw, k, v, a, g = outs[:6]
    vres = None if vfirst is None else (vfirst, outs[6])
    y, s_new = wkv_scan(r, lw, k, v, a, s0, p, batch, seq, hb, ln, vres)
    y_tok = jnp.swapaxes(y, 0, 1).reshape(m, c)
    h = linear_res(y_tok, p["w_o"], h, tm, gate=g)
    return h, s_new, xn3[:, -1], (v if vfirst is None else vfirst)


SEG_PAD = 128


def _dot_hilo_rhs(x, ones):
    hi = x.astype(BF16)
    lo = (x - hi.astype(F32)).astype(BF16)
    return jnp.dot(hi, ones, preferred_element_type=F32) + jnp.dot(lo, ones, preferred_element_type=F32)


def _seg_rms_scale(x, seg_ref, segt_ref, width):
    ssq = _dot(x * x, seg_ref[...])
    return _dot_hilo_rhs(lax.rsqrt(ssq * (1.0 / width) + NORM_EPS), segt_ref[...])


def _swap_halves(x, group):
    half = group // 2
    width = x.shape[-1]
    lane = lax.broadcasted_iota(jnp.int32, x.shape, x.ndim - 1)
    return jnp.where((lane % group) < half,
                     pltpu.roll(x, width - half, axis=x.ndim - 1),
                     pltpu.roll(x, half, axis=x.ndim - 1))


def _mla_proj_kernel(with_kv, *refs):
    (h_ref, cos_ref, sin_ref, gmix_ref, win_ref, gq_ref, gkv_ref, gkr_ref, wuq_ref, gqn_ref, gqr_ref,
     seg64_ref, seg64t_ref, seg32_ref, seg32t_ref) = refs[:15]
    if with_kv:
        wuk_ref, wuv_ref, gkn_ref, qn_ref, qr_ref, c_ref, kp_ref, kn_ref, v_ref = refs[15:]
    else:
        qn_ref, qr_ref, c_ref, kp_ref = refs[15:]
    x = _rms(h_ref[...], gmix_ref[...])
    hp = _dot(x, win_ref[...])
    q_a = _rms(hp[:, :MLA_Q_LORA], gq_ref[...])
    c = _rms(hp[:, MLA_Q_LORA:MLA_Q_LORA + MLA_KV_LORA], gkv_ref[...])
    c_ref[...] = c

    cos = cos_ref[...]
    sin = sin_ref[...]
    kp_raw = hp[:, MLA_Q_LORA + MLA_KV_LORA:]
    kp_scale = lax.rsqrt(jnp.sum(kp_raw * kp_raw, axis=-1, keepdims=True) * (1.0 / MLA_ROPE) + NORM_EPS)
    kp_g = kp_raw * gkr_ref[...]
    kp = (kp_g * cos[:, :128] + _swap_halves(kp_g, MLA_ROPE) * sin[:, :128]) * kp_scale
    kp_ref[...] = kp[:, :MLA_ROPE]

    q = _dot(q_a, wuq_ref[...])
    nn = MLA_HEADS * MLA_NOPE
    qn_raw = q[:, :nn]
    qr_raw = q[:, nn:]
    qn = qn_raw * _seg_rms_scale(qn_raw, seg64_ref, seg64t_ref, MLA_NOPE) * gqn_ref[...]
    qn_ref[...] = qn.astype(BF16)
    qr_g = qr_raw * gqr_ref[...]
    qr = (qr_g * cos + _swap_halves(qr_g, MLA_ROPE) * sin) * _seg_rms_scale(qr_raw, seg32_ref, seg32t_ref, MLA_ROPE)
    qr_ref[...] = (qr * MLA_SCALE).astype(BF16)
    if with_kv:
        kraw = _dot(c, wuk_ref[...])
        kn = kraw * _seg_rms_scale(kraw, seg64_ref, seg64t_ref, MLA_NOPE) * gkn_ref[...]
        kn_ref[...] = kn.astype(BF16)
        v_ref[...] = _dot(c, wuv_ref[...]).astype(BF16)


def _seg_matrix(n_seg, width):
    lane = jnp.arange(n_seg * width)[:, None] // width
    return (lane == jnp.arange(SEG_PAD)[None, :]).astype(BF16)


def _rope_tables(pos, reps):
    half = MLA_ROPE // 2
    inv = ROPE_THETA ** (-jnp.arange(half, dtype=F32) / half)
    ang = pos.astype(F32)[:, None] * inv[None, :]
    cos = jnp.cos(ang)
    sin = jnp.sin(ang)
    return (jnp.tile(jnp.concatenate([cos, cos], axis=1), (1, reps)),
            jnp.tile(jnp.concatenate([-sin, sin], axis=1), (1, reps)))


def mla_proj(h, pos, p, tm, with_kv):
    m, c = h.shape
    seq = pos.shape[0]
    cos, sin = _rope_tables(pos, MLA_HEADS)
    if seq >= tm:
        nrep = seq // tm
    else:
        cos, sin = jnp.tile(cos, (tm // seq, 1)), jnp.tile(sin, (tm // seq, 1))
        nrep = 1
    row = lambda width: pl.BlockSpec((tm, width), lambda i: (i, 0))
    tab = pl.BlockSpec((tm, MLA_HEADS * MLA_ROPE), lambda i: (i % nrep, 0))
    full = lambda arr: pl.BlockSpec(arr.shape, lambda i: (0,) * arr.ndim)
    seg64, seg32 = _seg_matrix(MLA_HEADS, MLA_NOPE), _seg_matrix(MLA_HEADS, MLA_ROPE)
    weights = [p["norm_mix"], p["w_in"], p["q_norm"], p["kv_norm"], p["kr_norm"], p["w_uq"],
               p["qn_gain"] if with_kv else p["qn_gain_abs"], p["qr_gain"],
               seg64, seg64.T, seg32, seg32.T]
    nn = MLA_HEADS * MLA_NOPE
    out_shape = [jax.ShapeDtypeStruct((m, nn), BF16), jax.ShapeDtypeStruct((m, MLA_HEADS * MLA_ROPE), BF16),
                 jax.ShapeDtypeStruct((m, MLA_KV_LORA), F32), jax.ShapeDtypeStruct((m, MLA_ROPE), F32)]
    out_specs = [row(nn), row(MLA_HEADS * MLA_ROPE), row(MLA_KV_LORA), row(MLA_ROPE)]
    if with_kv:
        weights += [p["w_uk"], p["w_uv"], p["kn_gain"]]
        out_shape += [jax.ShapeDtypeStruct((m, nn), BF16)] * 2
        out_specs += [row(nn)] * 2
    return pl.pallas_call(
        functools.partial(_mla_proj_kernel, with_kv),
        grid=(m // tm,),
        in_specs=[row(c), tab, tab] + [full(w) for w in weights],
        out_specs=out_specs,
        out_shape=out_shape,
        compiler_params=_params("parallel"),
        name="mla_proj",
    )(h, cos, sin, *weights)


def prep_mla(w, j, i):
    row = lambda z: z.reshape(1, -1).astype(F32)
    w_in = w["mla_w_in"][j]
    pad = jnp.zeros((D_MODEL, 128 - MLA_ROPE), F32)
    w_uq = w["mla_w_uq"][j]
    kr = jnp.concatenate([w["mla_kr_norm"][j], jnp.zeros((128 - MLA_ROPE,), F32)])
    qn_gain = jnp.tile(w["mla_qn_norm"][j], MLA_HEADS) * MLA_SCALE
    kn_gain = jnp.tile(w["mla_kn_norm"][j], MLA_HEADS)
    return {
        "norm_mix": row(w["norm_mix"][i]),
        "w_in": jnp.concatenate([w_in, pad], axis=1).astype(BF16),
        "q_norm": row(w["mla_q_norm"][j]), "kv_norm": row(w["mla_kv_norm"][j]), "kr_norm": row(kr),
        "w_uq": jnp.concatenate([w_uq[:, :, :MLA_NOPE].reshape(MLA_Q_LORA, -1),
                                 w_uq[:, :, MLA_NOPE:].reshape(MLA_Q_LORA, -1)], axis=1).astype(BF16),
        "qn_gain": row(qn_gain), "qn_gain_abs": row(qn_gain * kn_gain), "kn_gain": row(kn_gain),
        "qr_gain": row(jnp.tile(w["mla_qr_norm"][j], MLA_HEADS)),
        "w_uk": w["mla_w_uk"][j].reshape(MLA_KV_LORA, -1).astype(BF16),
        "w_uv": w["mla_w_uv"][j].reshape(MLA_KV_LORA, -1).astype(BF16),
        "w_uk_t": w["mla_w_uk"][j].reshape(MLA_KV_LORA, -1).T.astype(BF16),
        "w_uk_heads": jnp.transpose(w["mla_w_uk"][j], (1, 2, 0)).astype(BF16),
        "w_o": w["mla_w_o"][j].astype(BF16),
    }


ATTN_HEADS_PER_STEP = 4
NEG_BIG = -1e30


def _flash_kernel(tq, qn_ref, qr_ref, kn_ref, kp_ref, v_ref, o_ref):
    qi = pl.program_id(2)
    row = lax.broadcasted_iota(jnp.int32, (tq, tq), 0)
    col = lax.broadcasted_iota(jnp.int32, (tq, tq), 1)
    outs = []
    for hh in range(ATTN_HEADS_PER_STEP):
        qn = qn_ref[0, :, hh * MLA_NOPE:(hh + 1) * MLA_NOPE]
        qr = qr_ref[0, :, hh * MLA_ROPE:(hh + 1) * MLA_ROPE]

        def scores(j):
            kn = kn_ref[0, pl.ds(j * tq, tq), hh * MLA_NOPE:(hh + 1) * MLA_NOPE]
            kp = kp_ref[0, pl.ds(j * tq, tq), :]
            return _dot_nt(qn, kn) + _dot_nt(qr, kp)

        def update(carry, s, j):
            m_prev, l_prev, acc = carry
            m_new = jnp.maximum(m_prev, jnp.max(s, axis=-1, keepdims=True))
            alpha = jnp.exp(m_prev - m_new)
            pr = jnp.exp(s - m_new)
            vv = v_ref[0, pl.ds(j * tq, tq), hh * MLA_V:(hh + 1) * MLA_V]
            return (m_new, alpha * l_prev + jnp.sum(pr, axis=-1, keepdims=True),
                    alpha * acc + _dot(pr, vv))

        init = (jnp.full((tq, 1), NEG_BIG, F32), jnp.zeros((tq, 1), F32), jnp.zeros((tq, MLA_V), F32))
        carry = lax.fori_loop(0, qi, lambda j, cy: update(cy, scores(j), j), init)
        carry = update(carry, jnp.where(col <= row, scores(qi), NEG_BIG), qi)
        outs.append(carry[2] / carry[1])
    o_ref[0] = jnp.concatenate(outs, axis=-1)


def flash_prompt(qn, qr, kn, kp, v, batch, seq, tq):
    g = ATTN_HEADS_PER_STEP
    r3 = lambda z: z.reshape(batch, seq, -1)
    qblk = lambda width: pl.BlockSpec((1, tq, g * width), lambda b, hq, i: (b, i, hq))
    kblk = lambda width: pl.BlockSpec((1, seq, g * width), lambda b, hq, i: (b, 0, hq))
    return pl.pallas_call(
        functools.partial(_flash_kernel, tq),
        grid=(batch, MLA_HEADS // g, seq // tq),
        in_specs=[qblk(MLA_NOPE), qblk(MLA_ROPE), kblk(MLA_NOPE),
                  pl.BlockSpec((1, seq, MLA_ROPE), lambda b, hq, i: (b, 0, 0)), kblk(MLA_V)],
        out_specs=qblk(MLA_V),
        out_shape=jax.ShapeDtypeStruct((batch, seq, MLA_HEADS * MLA_V), F32),
        compiler_params=_params("parallel", "parallel", "arbitrary"),
        name="flash_prompt",
    )(r3(qn), r3(qr), r3(kn), r3(kp.astype(BF16)), r3(v)).reshape(batch * seq, -1)


def _bmm_kernel(a_ref, b_ref, o_ref):
    o_ref[0] = _dot(a_ref[0], b_ref[0]).astype(o_ref.dtype)


def bmm(a, b, out_dtype):
    g, m, k = a.shape
    n = b.shape[2]
    return pl.pallas_call(
        _bmm_kernel,
        grid=(g,),
        in_specs=[pl.BlockSpec((1, m, k), lambda i: (i, 0, 0)), pl.BlockSpec((1, k, n), lambda i: (i, 0, 0))],
        out_specs=pl.BlockSpec((1, m, n), lambda i: (i, 0, 0)),
        out_shape=jax.ShapeDtypeStruct((g, m, n), out_dtype),
        compiler_params=_params("parallel"),
        name="bmm",
    )(a, b)


def _paged_attn_kernel(pp, seq, *refs):
    pt_ref = refs[0]
    qa_ref, qp_ref, cn_ref, kpn_ref, wukt_ref, wuv_ref = refs[1:7]
    c_refs = refs[7:7 + pp]
    kp_refs = refs[7 + pp:7 + 2 * pp]
    o_ref, m_ref, l_ref, acc_ref = refs[7 + 2 * pp:]
    del pt_ref
    s_idx = pl.program_id(1)
    nq = seq * MLA_HEADS

    @pl.when(s_idx == 0)
    def _():
        m_ref[...] = jnp.full_like(m_ref, NEG_BIG)
        l_ref[...] = jnp.zeros_like(l_ref)
        acc_ref[...] = jnp.zeros_like(acc_ref)

    qa = qa_ref[0]
    qp = qp_ref[0]

    def attend(c_blk, kp_blk, mask):
        nk = c_blk.shape[0]
        c_bf = c_blk.astype(BF16)
        kraw_t = _dot_nt(wukt_ref[...], c_bf)
        ssq = jnp.sum((kraw_t * kraw_t).reshape(MLA_HEADS, MLA_NOPE, nk), axis=1)
        rs = lax.rsqrt(ssq * (1.0 / MLA_NOPE) + NORM_EPS)
        s = _dot_nt(qa, c_bf) * jnp.concatenate([rs] * seq, axis=0) + _dot_nt(qp, kp_blk)
        if mask is not None:
            s = jnp.where(mask, s, NEG_BIG)
        m_prev = m_ref[...]
        m_new = jnp.maximum(m_prev, jnp.max(s, axis=-1, keepdims=True))
        alpha = jnp.exp(m_prev - m_new)
        pr = jnp.exp(s - m_new)
        l_ref[...] = alpha * l_ref[...] + jnp.sum(pr, axis=-1, keepdims=True)
        acc_ref[...] = alpha * acc_ref[...] + _dot(pr, c_bf)
        m_ref[...] = m_new

    attend(jnp.concatenate([r[0] for r in c_refs], axis=0),
           jnp.concatenate([r[0] for r in kp_refs], axis=0), None)

    @pl.when(s_idx == pl.num_programs(1) - 1)
    def _():
        qtok = lax.broadcasted_iota(jnp.int32, (nq, seq), 0) // MLA_HEADS
        ktok = lax.broadcasted_iota(jnp.int32, (nq, seq), 1)
        attend(cn_ref[0], kpn_ref[0], ktok <= qtok)
        o_lat = acc_ref[...] / l_ref[...]
        full = _dot(o_lat, wuv_ref[...])
        rhead = lax.broadcasted_iota(jnp.int32, full.shape, 0) % MLA_HEADS
        lhead = lax.broadcasted_iota(jnp.int32, full.shape, 1) // MLA_V
        full = jnp.where(rhead == lhead, full, 0.0)
        o_ref[0] = jnp.sum(full.reshape(seq, MLA_HEADS, MLA_HEADS * MLA_V), axis=1)


def paged_attn(q_abs, qp, c_new, kp_new, pool_c, pool_kp, page_table, p, batch, seq, pp):
    n_pages = page_table.shape[1]
    ns = n_pages // pp
    nq = seq * MLA_HEADS
    per_b = lambda shp: pl.BlockSpec((1,) + shp, lambda b, s, pt: (b, 0, 0))
    full = lambda arr: pl.BlockSpec(arr.shape, lambda b, s, pt: (0,) * arr.ndim)

    def page(width, i):
        return pl.BlockSpec((1, PAGE_SIZE, width), lambda b, s, pt: (pt[b * n_pages + s * pp + i], 0, 0))

    grid_spec = pltpu.PrefetchScalarGridSpec(
        num_scalar_prefetch=1,
        grid=(batch, ns),
        in_specs=[per_b((nq, MLA_KV_LORA)), per_b((nq, MLA_ROPE)), per_b((seq, MLA_KV_LORA)),
                  per_b((seq, MLA_ROPE)), full(p["w_uk_t"]), full(p["w_uv"])]
                 + [page(MLA_KV_LORA, i) for i in range(pp)] + [page(MLA_ROPE, i) for i in range(pp)],
        out_specs=per_b((seq, MLA_HEADS * MLA_V)),
        scratch_shapes=[pltpu.VMEM((nq, 1), F32), pltpu.VMEM((nq, 1), F32), pltpu.VMEM((nq, MLA_KV_LORA), F32)],
    )
    return pl.pallas_call(
        functools.partial(_paged_attn_kernel, pp, seq),
        grid_spec=grid_spec,
        out_shape=jax.ShapeDtypeStruct((batch, seq, MLA_HEADS * MLA_V), F32),
        compiler_params=_params("parallel", "arbitrary"),
        name="paged_attn",
    )(page_table.reshape(-1), q_abs, qp, c_new.reshape(batch, seq, -1), kp_new.reshape(batch, seq, -1),
      p["w_uk_t"], p["w_uv"], *([pool_c] * pp), *([pool_kp] * pp))


def mla_layer_prompt(h, p, batch, seq, tm, tq):
    qn, qr, c, kp, kn, v = mla_proj(h, jnp.arange(seq), p, tm, True)
    o = flash_prompt(qn, qr, kn, kp, v, batch, seq, tq)
    return linear_res(o, p["w_o"], h, tm), c, kp


def mla_layer_sample(h, pool_c, pool_kp, page_table, p, batch, seq, past_len, tm, pp):
    m = batch * seq
    qn, qr, c, kp = mla_proj(h, past_len + jnp.arange(seq), p, tm, False)
    q_heads = jnp.swapaxes(qn.reshape(m, MLA_HEADS, MLA_NOPE), 0, 1)
    q_abs = bmm(q_heads, p["w_uk_heads"], BF16)
    q_abs = jnp.swapaxes(q_abs, 0, 1).reshape(batch, seq * MLA_HEADS, MLA_KV_LORA)
    qp = qr.reshape(batch, seq * MLA_HEADS, MLA_ROPE)
    o = paged_attn(q_abs, qp, c, kp, pool_c, pool_kp, page_table, p, batch, seq, pp)
    return linear_res(o.reshape(m, -1), p["w_o"], h, tm), c, kp


DT_PAD = 128
CONV_HALO = 8
MB_GN = MB_GROUPS * MB_STATE
MB_GROUP_INNER = MB_INNER // MB_GROUPS
MB_HEADS_PER_GROUP = MB_HEADS // MB_GROUPS


def _ssd_kernel(ck, xbc_ref, prev_ref, cs_ref, z_ref, dtr_ref, h0_ref, cw_ref, cb_ref, dtb_ref, alog_ref,
                dskip_ref, nw_ref, y_ref, hout_ref, h_ref):
    c = pl.program_id(1)

    @pl.when(c == 0)
    def _():
        h_ref[...] = h0_ref[0]

    halo = jnp.where(c == 0, cs_ref[0], prev_ref[0])
    xext = jnp.concatenate([halo, xbc_ref[0]], axis=0)
    conv = cb_ref[...]
    for j in range(MB_CONV):
        off = CONV_HALO - (MB_CONV - 1) + j
        conv = conv + xext[off:off + ck, :] * cw_ref[j:j + 1, :]
    xbc = _silu(conv)
    xs = xbc[:, :MB_INNER]
    bm = xbc[:, MB_INNER:MB_INNER + MB_GN]
    cm = xbc[:, MB_INNER + MB_GN:]

    dt = _softplus(dtr_ref[0] + dtb_ref[...])
    da = dt * (-jnp.exp(alog_ref[...]))
    ri = lax.broadcasted_iota(jnp.int32, (ck, ck), 0)
    ci = lax.broadcasted_iota(jnp.int32, (ck, ck), 1)
    causal = ci <= ri
    tri = jnp.where(causal, 1.0, 0.0).astype(BF16)
    hi, mid, lo = _split3(da)
    acum = (jnp.dot(tri, hi, preferred_element_type=F32) + jnp.dot(tri, mid, preferred_element_type=F32)
            + jnp.dot(tri, lo, preferred_element_type=F32))
    tn = (((0,), (0,)), ((), ()))
    tri_t = jnp.where(ri <= ci, 1.0, 0.0).astype(BF16)
    acum_t = (lax.dot_general(hi, tri_t, tn, preferred_element_type=F32)
              + lax.dot_general(mid, tri_t, tn, preferred_element_type=F32)
              + lax.dot_general(lo, tri_t, tn, preferred_element_type=F32))
    e_cum = jnp.exp(acum)
    a_last = acum[ck - 1:ck, :]
    e_end = jnp.exp(a_last - acum)
    e_last = jnp.exp(a_last)

    ys = []
    xdt_end = []
    for g in range(MB_GROUPS):
        b_g = bm[:, g * MB_STATE:(g + 1) * MB_STATE]
        c_g = cm[:, g * MB_STATE:(g + 1) * MB_STATE]
        cb = _dot_nt(c_g, b_g)
        h_g = h_ref[g * MB_GROUP_INNER:(g + 1) * MB_GROUP_INNER, :]
        y_state = _dot_nt(c_g, h_g)
        xe = []
        for hh in range(MB_HEADS_PER_GROUP):
            hd = g * MB_HEADS_PER_GROUP + hh
            x_h = xs[:, hd * MB_HEAD:(hd + 1) * MB_HEAD]
            xdt = x_h * dt[:, hd:hd + 1]
            seg = acum[:, hd:hd + 1] - acum_t[hd:hd + 1, :]
            lmat = jnp.where(causal, jnp.exp(jnp.where(causal, seg, 0.0)), 0.0)
            y_h = _dot(cb * lmat, xdt) + y_state[:, hh * MB_HEAD:(hh + 1) * MB_HEAD] * e_cum[:, hd:hd + 1]
            ys.append(y_h)
            xe.append(xdt * e_end[:, hd:hd + 1])
        xdt_end.append(jnp.concatenate(xe, axis=1))
    for g in range(MB_GROUPS):
        b_g = bm[:, g * MB_STATE:(g + 1) * MB_STATE]
        for hh in range(MB_HEADS_PER_GROUP):
            hd = g * MB_HEADS_PER_GROUP + hh
            upd = _dot_tn(xdt_end[g][:, hh * MB_HEAD:(hh + 1) * MB_HEAD], b_g)
            rows = slice(hd * MB_HEAD, (hd + 1) * MB_HEAD)
            h_ref[rows, :] = h_ref[rows, :] * e_last[:, hd:hd + 1] + upd

    y = jnp.concatenate(ys, axis=1) + dskip_ref[...] * xs
    yz = y * _silu(z_ref[0])
    outs = []
    for g in range(MB_GROUPS):
        yg = yz[:, g * MB_GROUP_INNER:(g + 1) * MB_GROUP_INNER]
        outs.append(yg * lax.rsqrt(jnp.mean(yg * yg, axis=-1, keepdims=True) + NORM_EPS))
    y_ref[0] = (jnp.concatenate(outs, axis=1) * nw_ref[...]).astype(y_ref.dtype)

    @pl.when(c == pl.num_programs(1) - 1)
    def _():
        hout_ref[0] = h_ref[...]


def ssd_scan(xbc, z, dt_raw, conv_state, h0, p, batch, seq, ck):
    assert CONV_HALO % 8 == 0 and ck % CONV_HALO == 0
    nc = seq // ck
    halo_blocks = ck // CONV_HALO
    cs8 = jnp.concatenate([jnp.zeros((batch, CONV_HALO - (MB_CONV - 1), MB_CONV_DIM), F32), conv_state], axis=1)
    chunk = lambda width: pl.BlockSpec((1, ck, width), lambda b, c: (b, c, 0))
    full = lambda arr: pl.BlockSpec(arr.shape, lambda b, c: (0,) * arr.ndim)
    weights = [p["conv_w"], p["conv_b"], p["dt_bias"], p["a_log"], p["d_skip"], p["norm_w"]]
    state = pl.BlockSpec((1, MB_INNER, MB_STATE), lambda b, c: (b, 0, 0))
    return pl.pallas_call(
        functools.partial(_ssd_kernel, ck),
        grid=(batch, nc),
        in_specs=[chunk(MB_CONV_DIM),
                  pl.BlockSpec((1, CONV_HALO, MB_CONV_DIM),
                               lambda b, c: (b, jnp.maximum(c * halo_blocks - 1, 0), 0)),
                  pl.BlockSpec((1, CONV_HALO, MB_CONV_DIM), lambda b, c: (b, 0, 0)),
                  chunk(MB_INNER), chunk(DT_PAD), state] + [full(w) for w in weights],
        out_specs=[chunk(MB_INNER), state],
        out_shape=[jax.ShapeDtypeStruct((batch, seq, MB_INNER), BF16),
                   jax.ShapeDtypeStruct((batch, MB_INNER, MB_STATE), F32)],
        scratch_shapes=[pltpu.VMEM((MB_INNER, MB_STATE), F32)],
        compiler_params=_params("parallel", "arbitrary"),
        name="ssd_scan",
    )(xbc, xbc, cs8, z, dt_raw, h0, *weights)


def prep_mamba(w, j, i):
    row = lambda z: z.reshape(1, -1).astype(F32)
    w_in = w["mamba_w_in"][j]
    padv = lambda z: jnp.concatenate([z.astype(F32), jnp.zeros((DT_PAD - MB_HEADS,), F32)])
    return {
        "norm_mix": w["norm_mix"][i],
        "w_z": w_in[:, :MB_INNER].astype(BF16),
        "w_xbc": w_in[:, MB_INNER:MB_INNER + MB_CONV_DIM].astype(BF16),
        "w_dt": jnp.concatenate([w_in[:, MB_INNER + MB_CONV_DIM:], jnp.zeros((D_MODEL, DT_PAD - MB_HEADS), F32)],
                                axis=1).astype(BF16),
        "conv_w": w["mamba_conv_w"][j], "conv_b": row(w["mamba_conv_b"][j]),
        "dt_bias": row(padv(w["mamba_dt_bias"][j])), "a_log": row(padv(w["mamba_a_log"][j])),
        "d_skip": row(jnp.repeat(w["mamba_d"][j].astype(F32), MB_HEAD)),
        "norm_w": row(w["mamba_norm"][j]),
        "w_o": w["mamba_w_o"][j].astype(BF16),
    }


def mamba_layer(h, conv_state, h0, p, batch, seq, tm, ck):
    m = batch * seq
    assert seq >= MB_CONV - 1
    z = norm_linear(h, p["norm_mix"], p["w_z"], tm, 1024)
    xbc = norm_linear(h, p["norm_mix"], p["w_xbc"], tm, 1024)
    dt_raw = norm_linear(h, p["norm_mix"], p["w_dt"], tm, DT_PAD)
    xbc3 = xbc.reshape(batch, seq, MB_CONV_DIM)
    y, h_new = ssd_scan(xbc3, z.reshape(batch, seq, MB_INNER), dt_raw.reshape(batch, seq, DT_PAD),
                        conv_state, h0.reshape(batch, MB_INNER, MB_STATE), p, batch, seq, ck)
    h = linear_res(y.reshape(m, MB_INNER), p["w_o"], h, tm)
    return h, xbc3[:, seq - (MB_CONV - 1):], h_new.reshape(batch, MB_HEADS, MB_HEAD, MB_STATE)


N_MIXERS = 3
TM_PROJ = 256
TM_ROWS = 512
TM_FFN = 1024
TH_FFN = 512
WKV_CHUNK = 64
WKV_HEADS_PER_STEP = 4
ATTN_TQ = 256
PAGES_PER_STEP = 8


def kernel(x_prompt, x_sample, cache_mla_ckv, cache_mla_kpe, state_rwkv_wkv, state_rwkv_shift, state_ssm, state_conv, page_table, norm_mix, norm_ffn, ffn_w1, ffn_w2, rwkv_mu, rwkv_w_rkv, rwkv_w0, rwkv_w1, rwkv_w2, rwkv_a0, rwkv_a1, rwkv_a2, rwkv_v0, rwkv_v1, rwkv_v2, rwkv_g1, rwkv_g2, rwkv_k_k, rwkv_k_a, rwkv_r_k, rwkv_lnx_w, rwkv_lnx_b, rwkv_w_o, mla_w_in, mla_q_norm, mla_kv_norm, mla_w_uq, mla_w_uk, mla_w_uv, mla_qn_norm, mla_qr_norm, mla_kn_norm, mla_kr_norm, mla_w_o, mamba_w_in, mamba_conv_w, mamba_conv_b, mamba_dt_bias, mamba_a_log, mamba_d, mamba_norm, mamba_w_o):
    w = dict(locals())
    bp, tp, c = x_prompt.shape
    bs, ts, _ = x_sample.shape
    depth = norm_mix.shape[0]
    past_len = page_table.shape[1] * PAGE_SIZE
    hp = x_prompt.reshape(bp * tp, c)
    hs = x_sample.reshape(bs * ts, c)
    vf_p = vf_s = None
    out = {k: [] for k in ("ckv_p", "kpe_p", "ckv_s", "kpe_s", "wkv_p", "sh_p", "wkv_s", "sh_s",
                           "ssm_p", "conv_p", "ssm_s", "conv_s")}
    for i in range(depth):
        kind, j = i % N_MIXERS, i // N_MIXERS
        if kind == 0:
            p = prep_rwkv(w, j, i)
            hp, s_p, l_p, vf_p = rwkv_layer(hp, jnp.zeros((bp, c), F32), jnp.zeros((bp, RW_HEADS, RW_HEAD, RW_HEAD), F32),
                                            vf_p, p, bp, tp, TM_PROJ, WKV_HEADS_PER_STEP, WKV_CHUNK)
            hs, s_s, l_s, vf_s = rwkv_layer(hs, state_rwkv_shift[j], state_rwkv_wkv[j], vf_s, p, bs, ts,
                                            TM_PROJ, RW_HEADS, ts)
            out["wkv_p"].append(s_p); out["sh_p"].append(l_p); out["wkv_s"].append(s_s); out["sh_s"].append(l_s)
        elif kind == 1:
            p = prep_mla(w, j, i)
            hp, c_p, k_p = mla_layer_prompt(hp, p, bp, tp, TM_ROWS, ATTN_TQ)
            hs, c_s, k_s = mla_layer_sample(hs, cache_mla_ckv[j], cache_mla_kpe[j], page_table, p, bs, ts,
                                            past_len, TM_ROWS, PAGES_PER_STEP)
            out["ckv_p"].append(c_p.reshape(bp, tp, -1)); out["kpe_p"].append(k_p.reshape(bp, tp, -1))
            out["ckv_s"].append(c_s.reshape(bs, ts, -1)); out["kpe_s"].append(k_s.reshape(bs, ts, -1))
        else:
            p = prep_mamba(w, j, i)
            hp, cv_p, h_p = mamba_layer(hp, jnp.zeros((bp, MB_CONV - 1, MB_CONV_DIM), F32),
                                        jnp.zeros((bp, MB_HEADS, MB_HEAD, MB_STATE), F32), p, bp, tp,
                                        TM_ROWS, MB_CHUNK)
            hs, cv_s, h_s = mamba_layer(hs, state_conv[j], state_ssm[j], p, bs, ts, TM_ROWS, math.gcd(ts, MB_CHUNK))
            out["ssm_p"].append(h_p); out["conv_p"].append(cv_p); out["ssm_s"].append(h_s); out["conv_s"].append(cv_s)
        w1, w2 = ffn_w1[i].astype(BF16), ffn_w2[i].astype(BF16)
        hp = ffn_res(hp, norm_ffn[i], w1, w2, TM_FFN, TH_FFN)
        hs = ffn_res(hs, norm_ffn[i], w1, w2, TM_FFN, TH_FFN)
    stack = lambda k: jnp.stack(out[k])
    return (hp.reshape(bp, tp, c), hs.reshape(bs, ts, c),
            stack("ckv_p"), stack("kpe_p"), stack("ckv_s"), stack("kpe_s"),
            stack("wkv_p"), stack("sh_p"), stack("wkv_s"), stack("sh_s"),
            stack("ssm_p"), stack("conv_p"), stack("ssm_s"), stack("conv_s"))
```

```python
import functools
import math

import jax
import jax.numpy as jnp
from jax import lax
from jax.experimental import pallas as pl
from jax.experimental.pallas import tpu as pltpu

F32 = jnp.float32
BF16 = jnp.bfloat16

D_MODEL = 1024
NORM_EPS = 1e-6

RW_HEAD = 64
RW_HEADS = D_MODEL // RW_HEAD
RW_LNX_EPS = 64e-5

MLA_HEADS = 16
MLA_Q_LORA = 512
MLA_KV_LORA = 256
MLA_NOPE = 64
MLA_ROPE = 32
MLA_V = 64
MLA_SCALE = 1.0 / math.sqrt(MLA_NOPE + MLA_ROPE)
ROPE_THETA = 10000.0
PAGE_SIZE = 128

MB_INNER = 2 * D_MODEL
MB_HEAD = 64
MB_HEADS = MB_INNER // MB_HEAD
MB_GROUPS = 4
MB_STATE = 128
MB_CONV = 4
MB_CONV_DIM = MB_INNER + 2 * MB_GROUPS * MB_STATE
MB_CHUNK = 128

FFN_HIDDEN = 4 * D_MODEL

VMEM_LIMIT_BYTES = 56 * 2**20


def _params(*sem):
    return pltpu.CompilerParams(dimension_semantics=sem, vmem_limit_bytes=VMEM_LIMIT_BYTES)


def _dot(a, b):
    return jnp.dot(a.astype(BF16), b.astype(BF16), preferred_element_type=F32)


def _dot_nt(a, b):
    return lax.dot_general(a.astype(BF16), b.astype(BF16), (((1,), (1,)), ((), ())),
                           preferred_element_type=F32)


def _dot_tn(a, b):
    return lax.dot_general(a.astype(BF16), b.astype(BF16), (((0,), (0,)), ((), ())),
                           preferred_element_type=F32)


def _split3(a):
    hi = a.astype(BF16)
    r1 = a - hi.astype(F32)
    mid = r1.astype(BF16)
    lo = (r1 - mid.astype(F32)).astype(BF16)
    return hi, mid, lo


def _dot_exact_lhs(ones, x):
    hi, mid, lo = _split3(x)
    o = ones.astype(BF16)
    return (jnp.dot(o, hi, preferred_element_type=F32)
            + jnp.dot(o, mid, preferred_element_type=F32)
            + jnp.dot(o, lo, preferred_element_type=F32))


def _dot_exact_rhs(x, ones):
    hi, mid, lo = _split3(x)
    o = ones.astype(BF16)
    return (jnp.dot(hi, o, preferred_element_type=F32)
            + jnp.dot(mid, o, preferred_element_type=F32)
            + jnp.dot(lo, o, preferred_element_type=F32))


def _rms(x, g):
    return x * lax.rsqrt(jnp.mean(x * x, axis=-1, keepdims=True) + NORM_EPS) * g


def _softplus(z):
    return jnp.maximum(z, 0.0) + jnp.log(1.0 + jnp.exp(-jnp.abs(z)))


def _sigmoid(z):
    return 1.0 / (1.0 + jnp.exp(-z))


def _silu(z):
    return z * _sigmoid(z)


def _rmsnorm_kernel(x_ref, g_ref, o_ref):
    o_ref[...] = _rms(x_ref[...], g_ref[...])


def rmsnorm_rows(x, g, tm):
    m, c = x.shape
    return pl.pallas_call(
        _rmsnorm_kernel,
        grid=(m // tm,),
        in_specs=[pl.BlockSpec((tm, c), lambda i: (i, 0)), pl.BlockSpec((1, c), lambda i: (0, 0))],
        out_specs=pl.BlockSpec((tm, c), lambda i: (i, 0)),
        out_shape=jax.ShapeDtypeStruct((m, c), F32),
        compiler_params=_params("parallel"),
        name="rmsnorm_rows",
    )(x, g.reshape(1, c))


def _linear_res_kernel(x_ref, w_ref, h_ref, o_ref):
    o_ref[...] = h_ref[...] + _dot(x_ref[...], w_ref[...])


def _gated_linear_res_kernel(x_ref, gate_ref, w_ref, h_ref, o_ref):
    o_ref[...] = h_ref[...] + _dot(x_ref[...] * gate_ref[...], w_ref[...])


def linear_res(x, w, h, tm, gate=None):
    m, k = x.shape
    n = w.shape[1]
    row = lambda width: pl.BlockSpec((tm, width), lambda i: (i, 0))
    wspec = pl.BlockSpec((k, n), lambda i: (0, 0))
    if gate is None:
        kern, args, specs = _linear_res_kernel, (x, w, h), [row(k), wspec, row(n)]
    else:
        kern, args, specs = _gated_linear_res_kernel, (x, gate, w, h), [row(k), row(k), wspec, row(n)]
    return pl.pallas_call(
        kern,
        grid=(m // tm,),
        in_specs=specs,
        out_specs=row(n),
        out_shape=jax.ShapeDtypeStruct((m, n), F32),
        compiler_params=_params("parallel"),
        name="linear_res",
    )(*args)


def _norm_linear_kernel(x_ref, g_ref, w_ref, o_ref, xn_ref):
    @pl.when(pl.program_id(1) == 0)
    def _():
        xn_ref[...] = _rms(x_ref[...], g_ref[...]).astype(BF16)

    o_ref[...] = jnp.dot(xn_ref[...], w_ref[...], preferred_element_type=F32)


def norm_linear(x, g, w, tm, tn):
    m, k = x.shape
    n = w.shape[1]
    return pl.pallas_call(
        _norm_linear_kernel,
        grid=(m // tm, n // tn),
        in_specs=[pl.BlockSpec((tm, k), lambda i, j: (i, 0)),
                  pl.BlockSpec((1, k), lambda i, j: (0, 0)),
                  pl.BlockSpec((k, tn), lambda i, j: (0, j))],
        out_specs=pl.BlockSpec((tm, tn), lambda i, j: (i, j)),
        out_shape=jax.ShapeDtypeStruct((m, n), F32),
        scratch_shapes=[pltpu.VMEM((tm, k), BF16)],
        compiler_params=_params("parallel", "arbitrary"),
        name="norm_linear",
    )(x, g.reshape(1, k), w)


def _ffn_kernel(h_ref, g_ref, w1_ref, w2_ref, o_ref, xn_ref, acc_ref):
    j = pl.program_id(1)

    @pl.when(j == 0)
    def _():
        xn_ref[...] = _rms(h_ref[...], g_ref[...]).astype(BF16)
        acc_ref[...] = jnp.zeros_like(acc_ref)

    u = jnp.dot(xn_ref[...], w1_ref[...], preferred_element_type=F32)
    u = jnp.square(jnp.maximum(u, 0.0))
    acc_ref[...] += jnp.dot(u.astype(BF16), w2_ref[...], preferred_element_type=F32)

    @pl.when(j == pl.num_programs(1) - 1)
    def _():
        o_ref[...] = h_ref[...] + acc_ref[...]


def ffn_res(h, g, w1, w2, tm, th):
    m, c = h.shape
    hid = w1.shape[1]
    return pl.pallas_call(
        _ffn_kernel,
        grid=(m // tm, hid // th),
        in_specs=[pl.BlockSpec((tm, c), lambda i, j: (i, 0)),
                  pl.BlockSpec((1, c), lambda i, j: (0, 0)),
                  pl.BlockSpec((c, th), lambda i, j: (0, j)),
                  pl.BlockSpec((th, c), lambda i, j: (j, 0))],
        out_specs=pl.BlockSpec((tm, c), lambda i, j: (i, 0)),
        out_shape=jax.ShapeDtypeStruct((m, c), F32),
        scratch_shapes=[pltpu.VMEM((tm, c), BF16), pltpu.VMEM((tm, c), F32)],
        compiler_params=_params("parallel", "arbitrary"),
        name="ffn_res",
    )(h, g.reshape(1, c), w1, w2)


def _rwkv_proj_kernel(has_vres, *refs):
    if has_vres:
        (xn_ref, xp_ref, mu_ref, wrkv_ref, w0_ref, w1_ref, w2_ref, a0_ref, a1_ref, a2_ref,
         g1_ref, g2_ref, v0_ref, v1_ref, v2_ref,
         r_ref, lw_ref, k_ref, v_ref, a_ref, g_ref, vg_ref) = refs
    else:
        (xn_ref, xp_ref, mu_ref, wrkv_ref, w0_ref, w1_ref, w2_ref, a0_ref, a1_ref, a2_ref,
         g1_ref, g2_ref,
         r_ref, lw_ref, k_ref, v_ref, a_ref, g_ref) = refs
    x = xn_ref[...]
    dx = xp_ref[...] - x
    xm = [(x + dx * mu_ref[p:p + 1, :]).astype(BF16) for p in range(6)]
    r_ref[...] = jnp.dot(xm[0], wrkv_ref[0], preferred_element_type=F32)
    k_ref[...] = jnp.dot(xm[1], wrkv_ref[1], preferred_element_type=F32)
    v_ref[...] = jnp.dot(xm[2], wrkv_ref[2], preferred_element_type=F32)
    wpre = w0_ref[...] + _dot(jnp.tanh(jnp.dot(xm[3], w1_ref[...], preferred_element_type=F32)), w2_ref[...])
    w_log = -_softplus(-wpre) - 0.5
    lw_ref[...] = -jnp.exp(w_log)
    a_ref[...] = _sigmoid(a0_ref[...] + _dot(jnp.dot(xm[4], a1_ref[...], preferred_element_type=F32), a2_ref[...]))
    g_ref[...] = _dot(_sigmoid(jnp.dot(xm[5], g1_ref[...], preferred_element_type=F32)), g2_ref[...])
    if has_vres:
        vg_ref[...] = _sigmoid(v0_ref[...] + _dot(jnp.dot(xm[2], v1_ref[...], preferred_element_type=F32),
                                                  v2_ref[...]))


def rwkv_proj(xn, xprev, p, tm):
    m, c = xn.shape
    has_vres = "v1" in p
    row = pl.BlockSpec((tm, c), lambda i: (i, 0))
    full = lambda arr: pl.BlockSpec(arr.shape, lambda i: (0,) * arr.ndim)
    names = ["mu", "w_rkv", "w0", "w1", "w2", "a0", "a1", "a2", "g1", "g2"]
    if has_vres:
        names += ["v0", "v1", "v2"]
    weights = [p[n] for n in names]
    n_out = 7 if has_vres else 6
    return pl.pallas_call(
        functools.partial(_rwkv_proj_kernel, has_vres),
        grid=(m // tm,),
        in_specs=[row, row] + [full(w) for w in weights],
        out_specs=[row] * n_out,
        out_shape=[jax.ShapeDtypeStruct((m, c), F32)] * n_out,
        compiler_params=_params("parallel"),
        name="rwkv_proj",
    )(xn, xprev, *weights)


PAIR = 2 * RW_HEAD


def _dot_tn_hilo(a, b):
    tn = (((0,), (0,)), ((), ()))
    ah = a.astype(BF16)
    al = (a - ah.astype(F32)).astype(BF16)
    bh = b.astype(BF16)
    bl = (b - bh.astype(F32)).astype(BF16)
    f = lambda x, y: lax.dot_general(x, y, tn, preferred_element_type=F32)
    return f(ah, bh) + f(ah, bl) + f(al, bh)


def _wkv_kernel(has_vres, npair, ln, *refs):
    if has_vres:
        (r_ref, lw_ref, k_ref, v_ref, a_ref, vf_ref, vg_ref, s0_ref,
         kk_ref, ka_ref, rk_ref, lnw_ref, lnb_ref, y_ref, sout_ref, s_ref) = refs
    else:
        (r_ref, lw_ref, k_ref, v_ref, a_ref, s0_ref,
         kk_ref, ka_ref, rk_ref, lnw_ref, lnb_ref, y_ref, sout_ref, s_ref) = refs
    c = pl.program_id(2)
    n = RW_HEAD
    rows = 2 * ln

    @pl.when(c == 0)
    def _():
        zero = jnp.zeros((n, n), F32)
        for p in range(npair):
            s_ref[p] = jnp.concatenate([jnp.concatenate([s0_ref[0, 2 * p], zero], axis=1),
                                        jnp.concatenate([zero, s0_ref[0, 2 * p + 1]], axis=1)], axis=0)

    first = lax.broadcasted_iota(jnp.int32, (ln, PAIR), 1) < n
    ri = lax.broadcasted_iota(jnp.int32, (rows, rows), 0)
    ci = lax.broadcasted_iota(jnp.int32, (rows, rows), 1)
    same = (ri // ln) == (ci // ln)
    incl = same & (ci <= ri)
    strict = same & (ci < ri)
    eye = jnp.where(ri == ci, 1.0, 0.0)
    ti = lax.broadcasted_iota(jnp.int32, (ln, ln), 0)
    tj = lax.broadcasted_iota(jnp.int32, (ln, ln), 1)
    tri = jnp.where(tj <= ti, 1.0, 0.0).astype(BF16)

    def seg_sum(x):
        s1 = jnp.sum(jnp.where(first, x, 0.0), axis=-1, keepdims=True)
        s2 = jnp.sum(jnp.where(first, 0.0, x), axis=-1, keepdims=True)
        return jnp.where(first, s1, s2)

    def by_head(x):
        return jnp.concatenate([jnp.where(first, x, 0.0), jnp.where(first, 0.0, x)], axis=0)

    pairs = range(npair)
    lanes = [slice(p * PAIR, (p + 1) * PAIR) for p in pairs]

    def prepare(p):
        r = r_ref[:, lanes[p]]
        lw = lw_ref[:, lanes[p]]
        k = k_ref[:, lanes[p]]
        v = v_ref[:, lanes[p]]
        a = a_ref[:, lanes[p]]
        if has_vres:
            v = v + (vf_ref[:, lanes[p]] - v) * vg_ref[:, lanes[p]]
        kk = k * kk_ref[:, lanes[p]]
        kk = kk / jnp.maximum(jnp.sqrt(seg_sum(kk * kk)), 1e-12)
        k2 = k * (1.0 + (a - 1.0) * ka_ref[:, lanes[p]])
        bv = kk * a
        hi, mid, lo = _split3(lw)
        cl3 = jnp.dot(tri, jnp.concatenate([hi, mid, lo], axis=1), preferred_element_type=F32)
        cl = cl3[:, :PAIR] + cl3[:, PAIR:2 * PAIR] + cl3[:, 2 * PAIR:]
        cl_end = cl[ln - 1:ln, :]
        g_inv = jnp.exp(-cl)
        g_end = jnp.exp(cl_end - cl)
        rt = r * jnp.exp(cl)
        at = -kk * jnp.exp(cl - lw)
        lhs = jnp.concatenate([by_head(at), by_head(rt)], axis=0)
        bh = bv * g_inv
        kh = k2 * g_inv
        return dict(r=r, v=v, k2=k2, lhs=lhs, rhs=jnp.concatenate([bh, bh, kh, kh], axis=0),
                    decay=jnp.exp(cl_end), v_h=by_head(v),
                    ws=jnp.concatenate([by_head(bv * g_end), by_head(k2 * g_end)], axis=0))

    st = [prepare(p) for p in pairs]
    aa = [_dot_nt(st[p]["lhs"], st[p]["rhs"]) for p in pairs]
    n_ab = [jnp.where(strict, aa[p][:rows, :rows], 0.0) for p in pairs]
    a_ak = [jnp.where(strict, aa[p][:rows, rows:], 0.0) for p in pairs]
    a_r = [jnp.concatenate([jnp.where(incl, aa[p][rows:, :rows], 0.0),
                            jnp.where(incl, aa[p][rows:, rows:], 0.0)], axis=1) for p in pairs]
    akv = [_dot(a_ak[p], st[p]["v_h"]) for p in pairs]
    tinv = [eye + n_ab[p] for p in pairs]
    npow = n_ab
    for _ in range(int(math.log2(ln)) - 1):
        npow = [_dot(npow[p], npow[p]) for p in pairs]
        tinv = [tinv[p] + _dot(npow[p], tinv[p]) for p in pairs]

    s_prev = [s_ref[p] for p in pairs]
    xy = [_dot_nt(st[p]["lhs"], s_prev[p]) for p in pairs]
    u = [_dot(tinv[p], xy[p][:rows] + akv[p]) for p in pairs]
    zs = [jnp.concatenate([u[p], st[p]["v_h"]], axis=0) for p in pairs]
    for p in pairs:
        s_ref[p] = s_prev[p] * st[p]["decay"] + _dot_tn_hilo(zs[p], st[p]["ws"])
    y_h = [xy[p][rows:] + _dot(a_r[p], zs[p]) for p in pairs]
    for p in pairs:
        y = y_h[p][:ln] + y_h[p][ln:]
        mean = seg_sum(y) * (1.0 / n)
        yc = y - mean
        var = seg_sum(yc * yc) * (1.0 / n)
        y = yc * lax.rsqrt(var + RW_LNX_EPS) * lnw_ref[:, lanes[p]] + lnb_ref[:, lanes[p]]
        y_ref[:, lanes[p]] = y + seg_sum(st[p]["r"] * st[p]["k2"] * rk_ref[:, lanes[p]]) * st[p]["v"]

    @pl.when(c == pl.num_programs(2) - 1)
    def _():
        for p in range(npair):
            sout_ref[0, 2 * p] = s_ref[p, :n, :n]
            sout_ref[0, 2 * p + 1] = s_ref[p, n:, n:]


def wkv_scan(r, lw, k, v, a, s0, p, batch, seq, npair, ln, vres=None):
    nc = seq // ln
    has_vres = vres is not None
    width = npair * PAIR
    tok = pl.BlockSpec((ln, width), lambda b, q, c: (b * nc + c, q))
    par = pl.BlockSpec((1, width), lambda b, q, c: (0, q))
    st = pl.BlockSpec((1, 2 * npair, RW_HEAD, RW_HEAD), lambda b, q, c: (b, q, 0, 0))
    seqs = [r, lw, k, v, a] + (list(vres) if has_vres else [])
    pars = [p[nm].reshape(1, D_MODEL) for nm in ("k_k", "k_a", "r_k", "lnx_w", "lnx_b")]
    return pl.pallas_call(
        functools.partial(_wkv_kernel, has_vres, npair, ln),
        grid=(batch, D_MODEL // width, nc),
        in_specs=[tok] * len(seqs) + [st] + [par] * 5,
        out_specs=[tok, st],
        out_shape=[jax.ShapeDtypeStruct(r.shape, F32), jax.ShapeDtypeStruct(s0.shape, F32)],
        scratch_shapes=[pltpu.VMEM((npair, PAIR, PAIR), F32)],
        compiler_params=_params("parallel", "parallel", "arbitrary"),
        name="wkv_scan",
    )(*seqs, s0, *pars)


def prep_rwkv(w, j, i):
    row = lambda z: z.reshape(1, -1).astype(F32)
    p = {
        "norm_mix": w["norm_mix"][i],
        "mu": w["rwkv_mu"][j],
        "w_rkv": w["rwkv_w_rkv"][j].astype(BF16),
        "w0": row(w["rwkv_w0"][j]), "w1": w["rwkv_w1"][j].astype(BF16), "w2": w["rwkv_w2"][j].astype(BF16),
        "a0": row(w["rwkv_a0"][j]), "a1": w["rwkv_a1"][j].astype(BF16), "a2": w["rwkv_a2"][j].astype(BF16),
        "g1": w["rwkv_g1"][j].astype(BF16), "g2": w["rwkv_g2"][j].astype(BF16),
        "k_k": w["rwkv_k_k"][j], "k_a": w["rwkv_k_a"][j], "r_k": w["rwkv_r_k"][j],
        "lnx_w": w["rwkv_lnx_w"][j], "lnx_b": w["rwkv_lnx_b"][j],
        "w_o": w["rwkv_w_o"][j].astype(BF16),
    }
    if j > 0:
        p["v0"] = row(w["rwkv_v0"][j - 1])
        p["v1"] = w["rwkv_v1"][j - 1].astype(BF16)
        p["v2"] = w["rwkv_v2"][j - 1].astype(BF16)
    return p


def rwkv_layer(h, shift, s0, vfirst, p, batch, seq, tm, npair, ln):
    m, c = h.shape
    xn = rmsnorm_rows(h, p["norm_mix"], tm)
    xn3 = xn.reshape(batch, seq, c)
    xprev = jnp.concatenate([shift[:, None, :], xn3[:, :-1]], axis=1).reshape(m, c)
    outs = rwkv_proj(xn, xprev, p, tm)
    r, lw, k, v, a, g = outs[:6]
    vres = None if vfirst is None else (vfirst, outs[6])
    y, s_new = wkv_scan(r, lw, k, v, a, s0, p, batch, seq, npair, ln, vres)
    h = linear_res(y, p["w_o"], h, tm, gate=g)
    return h, s_new, xn3[:, -1], (v if vfirst is None else vfirst)


SEG_PAD = 128


def _dot_hilo_rhs(x, ones):
    hi = x.astype(BF16)
    lo = (x - hi.astype(F32)).astype(BF16)
    return jnp.dot(hi, ones, preferred_element_type=F32) + jnp.dot(lo, ones, preferred_element_type=F32)


def _seg_rms_scale(x, seg_ref, segt_ref, invw_ref):
    ssq = _dot(x * x, seg_ref[...])
    return _dot_hilo_rhs(lax.rsqrt(ssq * invw_ref[...] + NORM_EPS), segt_ref[...])


def _swap_halves(x, group):
    half = group // 2
    width = x.shape[-1]
    lane = lax.broadcasted_iota(jnp.int32, x.shape, x.ndim - 1)
    return jnp.where((lane % group) < half,
                     pltpu.roll(x, width - half, axis=x.ndim - 1),
                     pltpu.roll(x, half, axis=x.ndim - 1))


SLOT = 128


def _mla_proj_kernel(with_kv, *refs):
    (h_ref, cos_ref, sin_ref, gmix_ref, win_ref, gq_ref, gkv_ref, gkr_ref, wuq_ref, gqc_ref,
     seg_ref, segt_ref, invw_ref) = refs[:13]
    if with_kv:
        wuk_ref, wuv_ref, gkn_ref, qc_ref, c_ref, kp_ref, kc_ref, v_ref = refs[13:]
    else:
        qc_ref, c_ref, kp_ref = refs[13:]
    x = _rms(h_ref[...], gmix_ref[...])
    hp = _dot(x, win_ref[...])
    q_a = _rms(hp[:, :MLA_Q_LORA], gq_ref[...])
    c = _rms(hp[:, MLA_Q_LORA:MLA_Q_LORA + MLA_KV_LORA], gkv_ref[...])
    c_ref[...] = c

    cos = cos_ref[...]
    sin = sin_ref[...]

    def rope(z, cs, sn):
        return z * cs + _swap_halves(z, MLA_ROPE) * sn

    kp_raw = hp[:, MLA_Q_LORA + MLA_KV_LORA:]
    kp_scale = lax.rsqrt(jnp.sum(kp_raw * kp_raw, axis=-1, keepdims=True) * (1.0 / MLA_ROPE) + NORM_EPS)
    kp = rope(pltpu.roll(kp_raw, MLA_NOPE, axis=1) * gkr_ref[...], cos, sin) * kp_scale
    kp_ref[...] = kp[:, MLA_NOPE:MLA_NOPE + MLA_ROPE]

    q = _dot(q_a, wuq_ref[...])
    cos_h = jnp.tile(cos, (1, MLA_HEADS))
    sin_h = jnp.tile(sin, (1, MLA_HEADS))
    qc = rope(q * gqc_ref[...], cos_h, sin_h) * _seg_rms_scale(q, seg_ref, segt_ref, invw_ref)
    qc_ref[...] = qc.astype(BF16)
    if with_kv:
        kraw = _dot(c, wuk_ref[...])
        kn = kraw * _seg_rms_scale(kraw, seg_ref, segt_ref, invw_ref) * gkn_ref[...]
        kc_ref[...] = (kn + jnp.tile(kp, (1, MLA_HEADS))).astype(BF16)
        v_ref[...] = _dot(c, wuv_ref[...]).astype(BF16)


def _slot_seg_matrix():
    lane = jnp.arange(MLA_HEADS * SLOT)
    head, off = lane // SLOT, lane % SLOT
    col = jnp.where(off < MLA_NOPE, head, jnp.where(off < MLA_NOPE + MLA_ROPE, MLA_HEADS + head, SEG_PAD))
    return (col[:, None] == jnp.arange(SEG_PAD)[None, :]).astype(BF16)


def _slot(nope, rope):
    pad = jnp.zeros(nope.shape[:-1] + (SLOT - MLA_NOPE - MLA_ROPE,), nope.dtype)
    return jnp.concatenate([nope, rope, pad], axis=-1)


def _rope_tables(pos):
    half = MLA_ROPE // 2
    inv = ROPE_THETA ** (-jnp.arange(half, dtype=F32) / half)
    ang = pos.astype(F32)[:, None] * inv[None, :]
    cos = jnp.cos(ang)
    sin = jnp.sin(ang)
    ones = jnp.ones((pos.shape[0], MLA_NOPE), F32)
    return (_slot(ones, jnp.concatenate([cos, cos], axis=1)),
            _slot(jnp.zeros_like(ones), jnp.concatenate([-sin, sin], axis=1)))


def mla_proj(h, pos, p, tm, with_kv):
    m, c = h.shape
    seq = pos.shape[0]
    cos, sin = _rope_tables(pos)
    if seq >= tm:
        nrep = seq // tm
    else:
        cos, sin = jnp.tile(cos, (tm // seq, 1)), jnp.tile(sin, (tm // seq, 1))
        nrep = 1
    row = lambda width: pl.BlockSpec((tm, width), lambda i: (i, 0))
    tab = pl.BlockSpec((tm, SLOT), lambda i: (i % nrep, 0))
    full = lambda arr: pl.BlockSpec(arr.shape, lambda i: (0,) * arr.ndim)
    seg = _slot_seg_matrix()
    invw = jnp.concatenate([jnp.full((MLA_HEADS,), 1.0 / MLA_NOPE, F32), jnp.full((MLA_HEADS,), 1.0 / MLA_ROPE, F32),
                            jnp.ones((SEG_PAD - 2 * MLA_HEADS,), F32)]).reshape(1, SEG_PAD)
    weights = [p["norm_mix"], p["w_in"], p["q_norm"], p["kv_norm"], p["kr_norm"], p["w_uq"],
               p["q_gain"] if with_kv else p["q_gain_abs"], seg, seg.T, invw]
    wide = MLA_HEADS * SLOT
    out_shape = [jax.ShapeDtypeStruct((m, wide), BF16),
                 jax.ShapeDtypeStruct((m, MLA_KV_LORA), F32), jax.ShapeDtypeStruct((m, MLA_ROPE), F32)]
    out_specs = [row(wide), row(MLA_KV_LORA), row(MLA_ROPE)]
    if with_kv:
        weights += [p["w_uk"], p["w_uv"], p["kn_gain"]]
        out_shape += [jax.ShapeDtypeStruct((m, wide), BF16), jax.ShapeDtypeStruct((m, MLA_HEADS * MLA_V), BF16)]
        out_specs += [row(wide), row(MLA_HEADS * MLA_V)]
    return pl.pallas_call(
        functools.partial(_mla_proj_kernel, with_kv),
        grid=(m // tm,),
        in_specs=[row(c), tab, tab] + [full(w) for w in weights],
        out_specs=out_specs,
        out_shape=out_shape,
        compiler_params=_params("parallel"),
        name="mla_proj",
    )(h, cos, sin, *weights)


def prep_mla(w, j, i):
    row = lambda z: z.reshape(1, -1).astype(F32)
    w_in = w["mla_w_in"][j]
    pad = jnp.zeros((D_MODEL, 128 - MLA_ROPE), F32)
    w_uq = w["mla_w_uq"][j]
    w_uk = w["mla_w_uk"][j]
    zero_n = jnp.zeros((MLA_NOPE,), F32)
    zero_r = jnp.zeros((MLA_ROPE,), F32)
    qn, qr, kn = w["mla_qn_norm"][j], w["mla_qr_norm"][j], w["mla_kn_norm"][j]
    return {
        "norm_mix": row(w["norm_mix"][i]),
        "w_in": jnp.concatenate([w_in, pad], axis=1).astype(BF16),
        "q_norm": row(w["mla_q_norm"][j]), "kv_norm": row(w["mla_kv_norm"][j]),
        "kr_norm": row(_slot(zero_n, w["mla_kr_norm"][j])),
        "w_uq": _slot(w_uq[:, :, :MLA_NOPE], w_uq[:, :, MLA_NOPE:]).reshape(MLA_Q_LORA, -1).astype(BF16),
        "q_gain": row(jnp.tile(_slot(qn, qr) * MLA_SCALE, MLA_HEADS)),
        "q_gain_abs": row(jnp.tile(_slot(qn * kn, qr) * MLA_SCALE, MLA_HEADS)),
        "kn_gain": row(jnp.tile(_slot(kn, zero_r), MLA_HEADS)),
        "w_uk": _slot(w_uk, jnp.zeros(w_uk.shape[:2] + (MLA_ROPE,), F32)).reshape(MLA_KV_LORA, -1).astype(BF16),
        "w_uv": w["mla_w_uv"][j].reshape(MLA_KV_LORA, -1).astype(BF16),
        "w_uk_t": w["mla_w_uk"][j].reshape(MLA_KV_LORA, -1).T.astype(BF16),
        "w_uk_heads": jnp.transpose(w["mla_w_uk"][j], (1, 2, 0)).astype(BF16),
        "w_o": w["mla_w_o"][j].astype(BF16),
    }


ATTN_HEADS_PER_STEP = 4
NEG_BIG = -1e30


def _flash_kernel(tq, qc_ref, kc_ref, v_ref, o_ref, m_ref, l_ref, acc_ref):
    qi = pl.program_id(2)
    g = ATTN_HEADS_PER_STEP
    row = lax.broadcasted_iota(jnp.int32, (tq, tq), 0)
    col = lax.broadcasted_iota(jnp.int32, (tq, tq), 1)
    m_ref[...] = jnp.full_like(m_ref, NEG_BIG)
    l_ref[...] = jnp.zeros_like(l_ref)
    acc_ref[...] = jnp.zeros_like(acc_ref)

    def block(j, masked):
        keys = pl.ds(pl.multiple_of(j * tq, tq), tq)
        heads = range(g)
        s = [_dot_nt(qc_ref[0, :, hh * SLOT:(hh + 1) * SLOT], kc_ref[0, keys, hh * SLOT:(hh + 1) * SLOT])
             for hh in heads]
        if masked:
            s = [jnp.where(col <= row, s[hh], NEG_BIG) for hh in heads]
        m_prev = [m_ref[hh] for hh in heads]
        m_new = [jnp.maximum(m_prev[hh], jnp.max(s[hh], axis=-1, keepdims=True)) for hh in heads]
        alpha = [jnp.exp(m_prev[hh] - m_new[hh]) for hh in heads]
        pr = [jnp.exp(s[hh] - m_new[hh]) for hh in heads]
        pv = [_dot(pr[hh], v_ref[0, keys, (hh // 2) * 2 * MLA_V:(hh // 2 + 1) * 2 * MLA_V]) for hh in heads]
        for hh in heads:
            l_ref[hh] = alpha[hh] * l_ref[hh] + jnp.sum(pr[hh], axis=-1, keepdims=True)
            acc_ref[hh] = alpha[hh] * acc_ref[hh] + pv[hh]
            m_ref[hh] = m_new[hh]

    def body(j, carry):
        block(j, False)
        return carry

    lax.fori_loop(0, qi, body, 0)
    block(qi, True)
    first = lax.broadcasted_iota(jnp.int32, (tq, 2 * MLA_V), 1) < MLA_V
    outs = [jnp.where(first, acc_ref[2 * pr] / l_ref[2 * pr], acc_ref[2 * pr + 1] / l_ref[2 * pr + 1])
            for pr in range(g // 2)]
    o_ref[0] = jnp.concatenate(outs, axis=-1)


def flash_prompt(qc, kc, v, batch, seq, tq):
    g = ATTN_HEADS_PER_STEP
    r3 = lambda z: z.reshape(batch, seq, -1)
    return pl.pallas_call(
        functools.partial(_flash_kernel, tq),
        grid=(batch, MLA_HEADS // g, seq // tq),
        in_specs=[pl.BlockSpec((1, tq, g * SLOT), lambda b, hq, i: (b, i, hq)),
                  pl.BlockSpec((1, seq, g * SLOT), lambda b, hq, i: (b, 0, hq)),
                  pl.BlockSpec((1, seq, g * MLA_V), lambda b, hq, i: (b, 0, hq))],
        out_specs=pl.BlockSpec((1, tq, g * MLA_V), lambda b, hq, i: (b, i, hq)),
        out_shape=jax.ShapeDtypeStruct((batch, seq, MLA_HEADS * MLA_V), F32),
        scratch_shapes=[pltpu.VMEM((g, tq, 1), F32), pltpu.VMEM((g, tq, 1), F32),
                        pltpu.VMEM((g, tq, 2 * MLA_V), F32)],
        compiler_params=_params("parallel", "parallel", "parallel"),
        name="flash_prompt",
    )(r3(qc), r3(kc), r3(v)).reshape(batch * seq, -1)


def _bmm_kernel(a_ref, b_ref, o_ref):
    o_ref[0] = _dot(a_ref[0], b_ref[0]).astype(o_ref.dtype)


def bmm(a, b, out_dtype):
    g, m, k = a.shape
    n = b.shape[2]
    return pl.pallas_call(
        _bmm_kernel,
        grid=(g,),
        in_specs=[pl.BlockSpec((1, m, k), lambda i: (i, 0, 0)), pl.BlockSpec((1, k, n), lambda i: (i, 0, 0))],
        out_specs=pl.BlockSpec((1, m, n), lambda i: (i, 0, 0)),
        out_shape=jax.ShapeDtypeStruct((g, m, n), out_dtype),
        compiler_params=_params("parallel"),
        name="bmm",
    )(a, b)


def _paged_attn_kernel(pp, seq, *refs):
    pt_ref = refs[0]
    qa_ref, qp_ref, cn_ref, kpn_ref, wukt_ref, wuv_ref = refs[1:7]
    c_refs = refs[7:7 + pp]
    kp_refs = refs[7 + pp:7 + 2 * pp]
    o_ref, m_ref, l_ref, acc_ref = refs[7 + 2 * pp:]
    del pt_ref
    s_idx = pl.program_id(1)
    nq = seq * MLA_HEADS

    @pl.when(s_idx == 0)
    def _():
        m_ref[...] = jnp.full_like(m_ref, NEG_BIG)
        l_ref[...] = jnp.zeros_like(l_ref)
        acc_ref[...] = jnp.zeros_like(acc_ref)

    qa = qa_ref[0]
    qp = qp_ref[0]

    def attend(c_blk, kp_blk, mask):
        nk = c_blk.shape[0]
        c_bf = c_blk.astype(BF16)
        kraw_t = _dot_nt(wukt_ref[...], c_bf)
        ssq = jnp.sum((kraw_t * kraw_t).reshape(MLA_HEADS, MLA_NOPE, nk), axis=1)
        rs = lax.rsqrt(ssq * (1.0 / MLA_NOPE) + NORM_EPS)
        s = _dot_nt(qa, c_bf) * jnp.concatenate([rs] * seq, axis=0) + _dot_nt(qp, kp_blk)
        if mask is not None:
            s = jnp.where(mask, s, NEG_BIG)
        m_prev = m_ref[...]
        m_new = jnp.maximum(m_prev, jnp.max(s, axis=-1, keepdims=True))
        alpha = jnp.exp(m_prev - m_new)
        pr = jnp.exp(s - m_new)
        l_ref[...] = alpha * l_ref[...] + jnp.sum(pr, axis=-1, keepdims=True)
        acc_ref[...] = alpha * acc_ref[...] + _dot(pr, c_bf)
        m_ref[...] = m_new

    attend(jnp.concatenate([r[0] for r in c_refs], axis=0),
           jnp.concatenate([r[0] for r in kp_refs], axis=0), None)

    @pl.when(s_idx == pl.num_programs(1) - 1)
    def _():
        qtok = lax.broadcasted_iota(jnp.int32, (nq, seq), 0) // MLA_HEADS
        ktok = lax.broadcasted_iota(jnp.int32, (nq, seq), 1)
        attend(cn_ref[0], kpn_ref[0], ktok <= qtok)
        o_lat = acc_ref[...] / l_ref[...]
        full = _dot(o_lat, wuv_ref[...])
        rhead = lax.broadcasted_iota(jnp.int32, full.shape, 0) % MLA_HEADS
        lhead = lax.broadcasted_iota(jnp.int32, full.shape, 1) // MLA_V
        full = jnp.where(rhead == lhead, full, 0.0)
        o_ref[0] = jnp.sum(full.reshape(seq, MLA_HEADS, MLA_HEADS * MLA_V), axis=1)


def paged_attn(q_abs, qp, c_new, kp_new, pool_c, pool_kp, page_table, p, batch, seq, pp):
    n_pages = page_table.shape[1]
    ns = n_pages // pp
    nq = seq * MLA_HEADS
    per_b = lambda shp: pl.BlockSpec((1,) + shp, lambda b, s, pt: (b, 0, 0))
    full = lambda arr: pl.BlockSpec(arr.shape, lambda b, s, pt: (0,) * arr.ndim)

    def page(width, i):
        return pl.BlockSpec((1, PAGE_SIZE, width), lambda b, s, pt: (pt[b * n_pages + s * pp + i], 0, 0))

    grid_spec = pltpu.PrefetchScalarGridSpec(
        num_scalar_prefetch=1,
        grid=(batch, ns),
        in_specs=[per_b((nq, MLA_KV_LORA)), per_b((nq, MLA_ROPE)), per_b((seq, MLA_KV_LORA)),
                  per_b((seq, MLA_ROPE)), full(p["w_uk_t"]), full(p["w_uv"])]
                 + [page(MLA_KV_LORA, i) for i in range(pp)] + [page(MLA_ROPE, i) for i in range(pp)],
        out_specs=per_b((seq, MLA_HEADS * MLA_V)),
        scratch_shapes=[pltpu.VMEM((nq, 1), F32), pltpu.VMEM((nq, 1), F32), pltpu.VMEM((nq, MLA_KV_LORA), F32)],
    )
    return pl.pallas_call(
        functools.partial(_paged_attn_kernel, pp, seq),
        grid_spec=grid_spec,
        out_shape=jax.ShapeDtypeStruct((batch, seq, MLA_HEADS * MLA_V), F32),
        compiler_params=_params("parallel", "arbitrary"),
        name="paged_attn",
    )(page_table.reshape(-1), q_abs, qp, c_new.reshape(batch, seq, -1), kp_new.reshape(batch, seq, -1),
      p["w_uk_t"], p["w_uv"], *([pool_c] * pp), *([pool_kp] * pp))


def mla_layer_prompt(h, p, batch, seq, tm, tq):
    qc, c, kp, kc, v = mla_proj(h, jnp.arange(seq), p, tm, True)
    o = flash_prompt(qc, kc, v, batch, seq, tq)
    return linear_res(o, p["w_o"], h, tm), c, kp


def mla_layer_sample(h, pool_c, pool_kp, page_table, p, batch, seq, past_len, tm, pp):
    m = batch * seq
    qc, c, kp = mla_proj(h, past_len + jnp.arange(seq), p, tm, False)
    qc = qc.reshape(m, MLA_HEADS, SLOT)
    q_heads = jnp.swapaxes(qc[:, :, :MLA_NOPE], 0, 1)
    q_abs = bmm(q_heads, p["w_uk_heads"], BF16)
    q_abs = jnp.swapaxes(q_abs, 0, 1).reshape(batch, seq * MLA_HEADS, MLA_KV_LORA)
    qp = qc[:, :, MLA_NOPE:MLA_NOPE + MLA_ROPE].reshape(batch, seq * MLA_HEADS, MLA_ROPE)
    o = paged_attn(q_abs, qp, c, kp, pool_c, pool_kp, page_table, p, batch, seq, pp)
    return linear_res(o.reshape(m, -1), p["w_o"], h, tm), c, kp


DT_PAD = 128
CONV_HALO = 8
MB_GN = MB_GROUPS * MB_STATE
MB_GROUP_INNER = MB_INNER // MB_GROUPS
MB_HEADS_PER_GROUP = MB_HEADS // MB_GROUPS


def _ssd_kernel(ck, xbc_ref, prev_ref, cs_ref, z_ref, dtr_ref, h0_ref, cw_ref, cb_ref, dtb_ref, alog_ref,
                dskip_ref, nw_ref, y_ref, hout_ref, h_ref):
    c = pl.program_id(1)

    @pl.when(c == 0)
    def _():
        h_ref[...] = h0_ref[0]

    halo = jnp.where(c == 0, cs_ref[0], prev_ref[0])
    xext = jnp.concatenate([halo, xbc_ref[0]], axis=0)
    conv = cb_ref[...]
    for j in range(MB_CONV):
        off = CONV_HALO - (MB_CONV - 1) + j
        conv = conv + xext[off:off + ck, :] * cw_ref[j:j + 1, :]
    xbc = _silu(conv)
    xs = xbc[:, :MB_INNER]
    bm = xbc[:, MB_INNER:MB_INNER + MB_GN]
    cm = xbc[:, MB_INNER + MB_GN:]

    dt = _softplus(dtr_ref[0] + dtb_ref[...])
    da = dt * (-jnp.exp(alog_ref[...]))
    ri = lax.broadcasted_iota(jnp.int32, (ck, ck), 0)
    ci = lax.broadcasted_iota(jnp.int32, (ck, ck), 1)
    causal = ci <= ri
    tri = jnp.where(causal, 1.0, 0.0).astype(BF16)
    hi, mid, lo = _split3(da)
    acum = (jnp.dot(tri, hi, preferred_element_type=F32) + jnp.dot(tri, mid, preferred_element_type=F32)
            + jnp.dot(tri, lo, preferred_element_type=F32))
    tn = (((0,), (0,)), ((), ()))
    tri_t = jnp.where(ri <= ci, 1.0, 0.0).astype(BF16)
    acum_t = (lax.dot_general(hi, tri_t, tn, preferred_element_type=F32)
              + lax.dot_general(mid, tri_t, tn, preferred_element_type=F32)
              + lax.dot_general(lo, tri_t, tn, preferred_element_type=F32))
    e_cum = jnp.exp(acum)
    a_last = acum[ck - 1:ck, :]
    e_end = jnp.exp(a_last - acum)
    e_last = jnp.exp(a_last)

    ys = []
    xdt_end = []
    for g in range(MB_GROUPS):
        b_g = bm[:, g * MB_STATE:(g + 1) * MB_STATE]
        c_g = cm[:, g * MB_STATE:(g + 1) * MB_STATE]
        cb = _dot_nt(c_g, b_g)
        h_g = h_ref[g * MB_GROUP_INNER:(g + 1) * MB_GROUP_INNER, :]
        y_state = _dot_nt(c_g, h_g)
        xe = []
        for hh in range(MB_HEADS_PER_GROUP):
            hd = g * MB_HEADS_PER_GROUP + hh
            x_h = xs[:, hd * MB_HEAD:(hd + 1) * MB_HEAD]
            xdt = x_h * dt[:, hd:hd + 1]
            seg = acum[:, hd:hd + 1] - acum_t[hd:hd + 1, :]
            lmat = jnp.where(causal, jnp.exp(jnp.where(causal, seg, 0.0)), 0.0)
            y_h = _dot(cb * lmat, xdt) + y_state[:, hh * MB_HEAD:(hh + 1) * MB_HEAD] * e_cum[:, hd:hd + 1]
            ys.append(y_h)
            xe.append(xdt * e_end[:, hd:hd + 1])
        xdt_end.append(jnp.concatenate(xe, axis=1))
    for g in range(MB_GROUPS):
        b_g = bm[:, g * MB_STATE:(g + 1) * MB_STATE]
        for hh in range(MB_HEADS_PER_GROUP):
            hd = g * MB_HEADS_PER_GROUP + hh
            upd = _dot_tn(xdt_end[g][:, hh * MB_HEAD:(hh + 1) * MB_HEAD], b_g)
            rows = slice(hd * MB_HEAD, (hd + 1) * MB_HEAD)
            h_ref[rows, :] = h_ref[rows, :] * e_last[:, hd:hd + 1] + upd

    y = jnp.concatenate(ys, axis=1) + dskip_ref[...] * xs
    yz = y * _silu(z_ref[0])
    outs = []
    for g in range(MB_GROUPS):
        yg = yz[:, g * MB_GROUP_INNER:(g + 1) * MB_GROUP_INNER]
        outs.append(yg * lax.rsqrt(jnp.mean(yg * yg, axis=-1, keepdims=True) + NORM_EPS))
    y_ref[0] = (jnp.concatenate(outs, axis=1) * nw_ref[...]).astype(y_ref.dtype)

    @pl.when(c == pl.num_programs(1) - 1)
    def _():
        hout_ref[0] = h_ref[...]


def ssd_scan(xbc, z, dt_raw, conv_state, h0, p, batch, seq, ck):
    assert CONV_HALO % 8 == 0 and ck % CONV_HALO == 0
    nc = seq // ck
    halo_blocks = ck // CONV_HALO
    cs8 = jnp.concatenate([jnp.zeros((batch, CONV_HALO - (MB_CONV - 1), MB_CONV_DIM), F32), conv_state], axis=1)
    chunk = lambda width: pl.BlockSpec((1, ck, width), lambda b, c: (b, c, 0))
    full = lambda arr: pl.BlockSpec(arr.shape, lambda b, c: (0,) * arr.ndim)
    weights = [p["conv_w"], p["conv_b"], p["dt_bias"], p["a_log"], p["d_skip"], p["norm_w"]]
    state = pl.BlockSpec((1, MB_INNER, MB_STATE), lambda b, c: (b, 0, 0))
    return pl.pallas_call(
        functools.partial(_ssd_kernel, ck),
        grid=(batch, nc),
        in_specs=[chunk(MB_CONV_DIM),
                  pl.BlockSpec((1, CONV_HALO, MB_CONV_DIM),
                               lambda b, c: (b, jnp.maximum(c * halo_blocks - 1, 0), 0)),
                  pl.BlockSpec((1, CONV_HALO, MB_CONV_DIM), lambda b, c: (b, 0, 0)),
                  chunk(MB_INNER), chunk(DT_PAD), state] + [full(w) for w in weights],
        out_specs=[chunk(MB_INNER), state],
        out_shape=[jax.ShapeDtypeStruct((batch, seq, MB_INNER), BF16),
                   jax.ShapeDtypeStruct((batch, MB_INNER, MB_STATE), F32)],
        scratch_shapes=[pltpu.VMEM((MB_INNER, MB_STATE), F32)],
        compiler_params=_params("parallel", "arbitrary"),
        name="ssd_scan",
    )(xbc, xbc, cs8, z, dt_raw, h0, *weights)


def prep_mamba(w, j, i):
    row = lambda z: z.reshape(1, -1).astype(F32)
    w_in = w["mamba_w_in"][j]
    padv = lambda z: jnp.concatenate([z.astype(F32), jnp.zeros((DT_PAD - MB_HEADS,), F32)])
    return {
        "norm_mix": w["norm_mix"][i],
        "w_z": w_in[:, :MB_INNER].astype(BF16),
        "w_xbc": w_in[:, MB_INNER:MB_INNER + MB_CONV_DIM].astype(BF16),
        "w_dt": jnp.concatenate([w_in[:, MB_INNER + MB_CONV_DIM:], jnp.zeros((D_MODEL, DT_PAD - MB_HEADS), F32)],
                                axis=1).astype(BF16),
        "conv_w": w["mamba_conv_w"][j], "conv_b": row(w["mamba_conv_b"][j]),
        "dt_bias": row(padv(w["mamba_dt_bias"][j])), "a_log": row(padv(w["mamba_a_log"][j])),
        "d_skip": row(jnp.repeat(w["mamba_d"][j].astype(F32), MB_HEAD)),
        "norm_w": row(w["mamba_norm"][j]),
        "w_o": w["mamba_w_o"][j].astype(BF16),
    }


def mamba_layer(h, conv_state, h0, p, batch, seq, tm, ck):
    m = batch * seq
    assert seq >= MB_CONV - 1
    z = norm_linear(h, p["norm_mix"], p["w_z"], tm, 1024)
    xbc = norm_linear(h, p["norm_mix"], p["w_xbc"], tm, 1024)
    dt_raw = norm_linear(h, p["norm_mix"], p["w_dt"], tm, DT_PAD)
    xbc3 = xbc.reshape(batch, seq, MB_CONV_DIM)
    y, h_new = ssd_scan(xbc3, z.reshape(batch, seq, MB_INNER), dt_raw.reshape(batch, seq, DT_PAD),
                        conv_state, h0.reshape(batch, MB_INNER, MB_STATE), p, batch, seq, ck)
    h = linear_res(y.reshape(m, MB_INNER), p["w_o"], h, tm)
    return h, xbc3[:, seq - (MB_CONV - 1):], h_new.reshape(batch, MB_HEADS, MB_HEAD, MB_STATE)


N_MIXERS = 3
TM_PROJ = 256
TM_ROWS = 512
TM_FFN = 1024
TH_FFN = 512
WKV_CHUNK = 64
WKV_PAIRS_PER_STEP = 8
ATTN_TQ = 256
PAGES_PER_STEP = 8


def kernel(x_prompt, x_sample, cache_mla_ckv, cache_mla_kpe, state_rwkv_wkv, state_rwkv_shift, state_ssm, state_conv, page_table, norm_mix, norm_ffn, ffn_w1, ffn_w2, rwkv_mu, rwkv_w_rkv, rwkv_w0, rwkv_w1, rwkv_w2, rwkv_a0, rwkv_a1, rwkv_a2, rwkv_v0, rwkv_v1, rwkv_v2, rwkv_g1, rwkv_g2, rwkv_k_k, rwkv_k_a, rwkv_r_k, rwkv_lnx_w, rwkv_lnx_b, rwkv_w_o, mla_w_in, mla_q_norm, mla_kv_norm, mla_w_uq, mla_w_uk, mla_w_uv, mla_qn_norm, mla_qr_norm, mla_kn_norm, mla_kr_norm, mla_w_o, mamba_w_in, mamba_conv_w, mamba_conv_b, mamba_dt_bias, mamba_a_log, mamba_d, mamba_norm, mamba_w_o):
    w = dict(locals())
    bp, tp, c = x_prompt.shape
    bs, ts, _ = x_sample.shape
    depth = norm_mix.shape[0]
    past_len = page_table.shape[1] * PAGE_SIZE
    hp = x_prompt.reshape(bp * tp, c)
    hs = x_sample.reshape(bs * ts, c)
    vf_p = vf_s = None
    out = {k: [] for k in ("ckv_p", "kpe_p", "ckv_s", "kpe_s", "wkv_p", "sh_p", "wkv_s", "sh_s",
                           "ssm_p", "conv_p", "ssm_s", "conv_s")}
    for i in range(depth):
        kind, j = i % N_MIXERS, i // N_MIXERS
        if kind == 0:
            p = prep_rwkv(w, j, i)
            hp, s_p, l_p, vf_p = rwkv_layer(hp, jnp.zeros((bp, c), F32), jnp.zeros((bp, RW_HEADS, RW_HEAD, RW_HEAD), F32),
                                            vf_p, p, bp, tp, TM_PROJ, WKV_PAIRS_PER_STEP, WKV_CHUNK)
            hs, s_s, l_s, vf_s = rwkv_layer(hs, state_rwkv_shift[j], state_rwkv_wkv[j], vf_s, p, bs, ts,
                                            TM_PROJ, WKV_PAIRS_PER_STEP, ts)
            out["wkv_p"].append(s_p); out["sh_p"].append(l_p); out["wkv_s"].append(s_s); out["sh_s"].append(l_s)
        elif kind == 1:
            p = prep_mla(w, j, i)
            hp, c_p, k_p = mla_layer_prompt(hp, p, bp, tp, TM_PROJ, ATTN_TQ)
            hs, c_s, k_s = mla_layer_sample(hs, cache_mla_ckv[j], cache_mla_kpe[j], page_table, p, bs, ts,
                                            past_len, TM_PROJ, PAGES_PER_STEP)
            out["ckv_p"].append(c_p.reshape(bp, tp, -1)); out["kpe_p"].append(k_p.reshape(bp, tp, -1))
            out["ckv_s"].append(c_s.reshape(bs, ts, -1)); out["kpe_s"].append(k_s.reshape(bs, ts, -1))
        else:
            p = prep_mamba(w, j, i)
            hp, cv_p, h_p = mamba_layer(hp, jnp.zeros((bp, MB_CONV - 1, MB_CONV_DIM), F32),
                                        jnp.zeros((bp, MB_HEADS, MB_HEAD, MB_STATE), F32), p, bp, tp,
                                        TM_ROWS, MB_CHUNK)
            hs, cv_s, h_s = mamba_layer(hs, state_conv[j], state_ssm[j], p, bs, ts, TM_ROWS, math.gcd(ts, MB_CHUNK))
            out["ssm_p"].append(h_p); out["conv_p"].append(cv_p); out["ssm_s"].append(h_s); out["conv_s"].append(cv_s)
        w1, w2 = ffn_w1[i].astype(BF16), ffn_w2[i].astype(BF16)
        hp = ffn_res(hp, norm_ffn[i], w1, w2, TM_FFN, TH_FFN)
        hs = ffn_res(hs, norm_ffn[i], w1, w2, TM_FFN, TH_FFN)
    stack = lambda k: jnp.stack(out[k])
    return (hp.reshape(bp, tp, c), hs.reshape(bs, ts, c),
            stack("ckv_p"), stack("kpe_p"), stack("ckv_s"), stack("kpe_s"),
            stack("wkv_p"), stack("sh_p"), stack("wkv_s"), stack("sh_s"),
            stack("ssm_p"), stack("conv_p"), stack("ssm_s"), stack("conv_s"))
```

```python
import functools
import math

import jax
import jax.numpy as jnp
from jax import lax
from jax.experimental import pallas as pl
from jax.experimental.pallas import tpu as pltpu

F32 = jnp.float32
BF16 = jnp.bfloat16

D_MODEL = 1024
NORM_EPS = 1e-6

RW_HEAD = 64
RW_HEADS = D_MODEL // RW_HEAD
RW_LNX_EPS = 64e-5

MLA_HEADS = 16
MLA_Q_LORA = 512
MLA_KV_LORA = 256
MLA_NOPE = 64
MLA_ROPE = 32
MLA_V = 64
MLA_SCALE = 1.0 / math.sqrt(MLA_NOPE + MLA_ROPE)
ROPE_THETA = 10000.0
PAGE_SIZE = 128

MB_INNER = 2 * D_MODEL
MB_HEAD = 64
MB_HEADS = MB_INNER // MB_HEAD
MB_GROUPS = 4
MB_STATE = 128
MB_CONV = 4
MB_CONV_DIM = MB_INNER + 2 * MB_GROUPS * MB_STATE
MB_CHUNK = 128

FFN_HIDDEN = 4 * D_MODEL

VMEM_LIMIT_BYTES = 56 * 2**20


def _params(*sem):
    return pltpu.CompilerParams(dimension_semantics=sem, vmem_limit_bytes=VMEM_LIMIT_BYTES)


def _dot(a, b):
    return jnp.dot(a.astype(BF16), b.astype(BF16), preferred_element_type=F32)


def _dot_nt(a, b):
    return lax.dot_general(a.astype(BF16), b.astype(BF16), (((1,), (1,)), ((), ())),
                           preferred_element_type=F32)


def _dot_tn(a, b):
    return lax.dot_general(a.astype(BF16), b.astype(BF16), (((0,), (0,)), ((), ())),
                           preferred_element_type=F32)


def _split3(a):
    hi = a.astype(BF16)
    r1 = a - hi.astype(F32)
    mid = r1.astype(BF16)
    lo = (r1 - mid.astype(F32)).astype(BF16)
    return hi, mid, lo


def _dot_exact_lhs(ones, x):
    hi, mid, lo = _split3(x)
    o = ones.astype(BF16)
    return (jnp.dot(o, hi, preferred_element_type=F32)
            + jnp.dot(o, mid, preferred_element_type=F32)
            + jnp.dot(o, lo, preferred_element_type=F32))


def _dot_exact_rhs(x, ones):
    hi, mid, lo = _split3(x)
    o = ones.astype(BF16)
    return (jnp.dot(hi, o, preferred_element_type=F32)
            + jnp.dot(mid, o, preferred_element_type=F32)
            + jnp.dot(lo, o, preferred_element_type=F32))


def _rms(x, g):
    return x * lax.rsqrt(jnp.mean(x * x, axis=-1, keepdims=True) + NORM_EPS) * g


def _softplus(z):
    return jnp.maximum(z, 0.0) + jnp.log(1.0 + jnp.exp(-jnp.abs(z)))


def _sigmoid(z):
    return 0.5 * jnp.tanh(0.5 * z) + 0.5


def _silu(z):
    return z * _sigmoid(z)


def _rmsnorm_kernel(x_ref, g_ref, o_ref):
    o_ref[...] = _rms(x_ref[...], g_ref[...])


def rmsnorm_rows(x, g, tm):
    m, c = x.shape
    return pl.pallas_call(
        _rmsnorm_kernel,
        grid=(m // tm,),
        in_specs=[pl.BlockSpec((tm, c), lambda i: (i, 0)), pl.BlockSpec((1, c), lambda i: (0, 0))],
        out_specs=pl.BlockSpec((tm, c), lambda i: (i, 0)),
        out_shape=jax.ShapeDtypeStruct((m, c), F32),
        compiler_params=_params("parallel"),
        name="rmsnorm_rows",
    )(x, g.reshape(1, c))


def _linear_res_kernel(x_ref, w_ref, h_ref, o_ref):
    o_ref[...] = h_ref[...] + _dot(x_ref[...], w_ref[...])


def _gated_linear_res_kernel(x_ref, gate_ref, w_ref, h_ref, o_ref):
    o_ref[...] = h_ref[...] + _dot(x_ref[...] * gate_ref[...], w_ref[...])


def linear_res(x, w, h, tm, gate=None):
    m, k = x.shape
    n = w.shape[1]
    row = lambda width: pl.BlockSpec((tm, width), lambda i: (i, 0))
    wspec = pl.BlockSpec((k, n), lambda i: (0, 0))
    if gate is None:
        kern, args, specs = _linear_res_kernel, (x, w, h), [row(k), wspec, row(n)]
    else:
        kern, args, specs = _gated_linear_res_kernel, (x, gate, w, h), [row(k), row(k), wspec, row(n)]
    return pl.pallas_call(
        kern,
        grid=(m // tm,),
        in_specs=specs,
        out_specs=row(n),
        out_shape=jax.ShapeDtypeStruct((m, n), F32),
        compiler_params=_params("parallel"),
        name="linear_res",
    )(*args)


def _norm_linear_kernel(x_ref, g_ref, w_ref, o_ref, xn_ref):
    @pl.when(pl.program_id(1) == 0)
    def _():
        xn_ref[...] = _rms(x_ref[...], g_ref[...]).astype(BF16)

    o_ref[...] = jnp.dot(xn_ref[...], w_ref[...], preferred_element_type=F32)


def norm_linear(x, g, w, tm, tn):
    m, k = x.shape
    n = w.shape[1]
    return pl.pallas_call(
        _norm_linear_kernel,
        grid=(m // tm, n // tn),
        in_specs=[pl.BlockSpec((tm, k), lambda i, j: (i, 0)),
                  pl.BlockSpec((1, k), lambda i, j: (0, 0)),
                  pl.BlockSpec((k, tn), lambda i, j: (0, j))],
        out_specs=pl.BlockSpec((tm, tn), lambda i, j: (i, j)),
        out_shape=jax.ShapeDtypeStruct((m, n), F32),
        scratch_shapes=[pltpu.VMEM((tm, k), BF16)],
        compiler_params=_params("parallel", "arbitrary"),
        name="norm_linear",
    )(x, g.reshape(1, k), w)


def _ffn_kernel(h_ref, g_ref, w1_ref, w2_ref, o_ref, xn_ref, acc_ref):
    j = pl.program_id(1)

    @pl.when(j == 0)
    def _():
        xn_ref[...] = _rms(h_ref[...], g_ref[...]).astype(BF16)
        acc_ref[...] = jnp.zeros_like(acc_ref)

    u = jnp.dot(xn_ref[...], w1_ref[...], preferred_element_type=F32)
    u = jnp.square(jnp.maximum(u, 0.0))
    acc_ref[...] += jnp.dot(u.astype(BF16), w2_ref[...], preferred_element_type=F32)

    @pl.when(j == pl.num_programs(1) - 1)
    def _():
        o_ref[...] = h_ref[...] + acc_ref[...]


def ffn_res(h, g, w1, w2, tm, th):
    m, c = h.shape
    hid = w1.shape[1]
    return pl.pallas_call(
        _ffn_kernel,
        grid=(m // tm, hid // th),
        in_specs=[pl.BlockSpec((tm, c), lambda i, j: (i, 0)),
                  pl.BlockSpec((1, c), lambda i, j: (0, 0)),
                  pl.BlockSpec((c, th), lambda i, j: (0, j)),
                  pl.BlockSpec((th, c), lambda i, j: (j, 0))],
        out_specs=pl.BlockSpec((tm, c), lambda i, j: (i, 0)),
        out_shape=jax.ShapeDtypeStruct((m, c), F32),
        scratch_shapes=[pltpu.VMEM((tm, c), BF16), pltpu.VMEM((tm, c), F32)],
        compiler_params=_params("parallel", "arbitrary"),
        name="ffn_res",
    )(h, g.reshape(1, c), w1, w2)


SHIFT_HALO = 8


def _rwkv_proj_kernel(has_vres, seq, tm, *refs):
    if has_vres:
        (h_ref, halo_ref, sh_ref, gmix_ref, mu_ref, wrkv_ref, w0_ref, w1_ref, w2_ref, a0_ref, a1_ref, a2_ref,
         g1_ref, g2_ref, v0_ref, v1_ref, v2_ref,
         r_ref, lw_ref, k_ref, v_ref, a_ref, g_ref, vg_ref) = refs
    else:
        (h_ref, halo_ref, sh_ref, gmix_ref, mu_ref, wrkv_ref, w0_ref, w1_ref, w2_ref, a0_ref, a1_ref, a2_ref,
         g1_ref, g2_ref,
         r_ref, lw_ref, k_ref, v_ref, a_ref, g_ref) = refs
    x = _rms(h_ref[...], gmix_ref[...])
    rolled = pltpu.roll(x, 1, axis=0)
    row = lax.broadcasted_iota(jnp.int32, x.shape, 0)
    if seq >= tm:
        is_start = (pl.program_id(0) % (seq // tm)) == 0
        tail = _rms(halo_ref[...], gmix_ref[...])[SHIFT_HALO - 1:SHIFT_HALO, :]
        xprev = jnp.where(row == 0, jnp.where(is_start, sh_ref[0], tail), rolled)
    else:
        n_seq = tm // seq
        starts = jnp.broadcast_to(sh_ref[...][:, None, :], (n_seq, seq, x.shape[1])).reshape(x.shape)
        xprev = jnp.where(row % seq == 0, starts, rolled)
    dx = xprev - x
    xm = [(x + dx * mu_ref[p:p + 1, :]).astype(BF16) for p in range(6)]
    r_ref[...] = jnp.dot(xm[0], wrkv_ref[0], preferred_element_type=F32)
    k_ref[...] = jnp.dot(xm[1], wrkv_ref[1], preferred_element_type=F32)
    v_ref[...] = jnp.dot(xm[2], wrkv_ref[2], preferred_element_type=F32)
    wpre = w0_ref[...] + _dot(jnp.tanh(jnp.dot(xm[3], w1_ref[...], preferred_element_type=F32)), w2_ref[...])
    w_log = -_softplus(-wpre) - 0.5
    lw_ref[...] = -jnp.exp(w_log)
    a_ref[...] = _sigmoid(a0_ref[...] + _dot(jnp.dot(xm[4], a1_ref[...], preferred_element_type=F32), a2_ref[...]))
    g_ref[...] = _dot(_sigmoid(jnp.dot(xm[5], g1_ref[...], preferred_element_type=F32)), g2_ref[...])
    if has_vres:
        vg_ref[...] = _sigmoid(v0_ref[...] + _dot(jnp.dot(xm[2], v1_ref[...], preferred_element_type=F32),
                                                  v2_ref[...]))


def rwkv_proj(h, shift, p, seq, tm):
    m, c = h.shape
    assert tm % SHIFT_HALO == 0 and (seq % tm == 0 or tm % seq == 0)
    has_vres = "v1" in p
    row = pl.BlockSpec((tm, c), lambda i: (i, 0))
    halo = pl.BlockSpec((SHIFT_HALO, c), lambda i: (jnp.maximum(i * (tm // SHIFT_HALO) - 1, 0), 0))
    if seq >= tm:
        shift_arg = shift.reshape(-1, 1, c)
        shift_spec = pl.BlockSpec((1, 1, c), lambda i: (i // (seq // tm), 0, 0))
    else:
        shift_arg = shift
        shift_spec = pl.BlockSpec((tm // seq, c), lambda i: (i, 0))
    full = lambda arr: pl.BlockSpec(arr.shape, lambda i: (0,) * arr.ndim)
    names = ["norm_mix", "mu", "w_rkv", "w0", "w1", "w2", "a0", "a1", "a2", "g1", "g2"]
    if has_vres:
        names += ["v0", "v1", "v2"]
    weights = [p[n] for n in names]
    n_out = 7 if has_vres else 6
    return pl.pallas_call(
        functools.partial(_rwkv_proj_kernel, has_vres, seq, tm),
        grid=(m // tm,),
        in_specs=[row, halo, shift_spec] + [full(w) for w in weights],
        out_specs=[row] * n_out,
        out_shape=[jax.ShapeDtypeStruct((m, c), F32)] * n_out,
        compiler_params=_params("parallel"),
        name="rwkv_proj",
    )(h, h, shift_arg, *weights)


PAIR = 2 * RW_HEAD


def _dot_hilo(a, b, dims=(((1,), (0,)), ((), ()))):
    ah = a.astype(BF16)
    al = (a - ah.astype(F32)).astype(BF16)
    bh = b.astype(BF16)
    bl = (b - bh.astype(F32)).astype(BF16)
    f = lambda x, y: lax.dot_general(x, y, dims, preferred_element_type=F32)
    return f(ah, bh) + f(ah, bl) + f(al, bh)


def _dot_tn_hilo(a, b):
    return _dot_hilo(a, b, (((0,), (0,)), ((), ())))


def _wkv_kernel(has_vres, npair, ln, *refs):
    if has_vres:
        (r_ref, lw_ref, k_ref, v_ref, a_ref, vf_ref, vg_ref, s0_ref,
         kk_ref, ka_ref, rk_ref, lnw_ref, lnb_ref, y_ref, sout_ref, s_ref) = refs
    else:
        (r_ref, lw_ref, k_ref, v_ref, a_ref, s0_ref,
         kk_ref, ka_ref, rk_ref, lnw_ref, lnb_ref, y_ref, sout_ref, s_ref) = refs
    c = pl.program_id(2)
    n = RW_HEAD
    rows = 2 * ln

    @pl.when(c == 0)
    def _():
        zero = jnp.zeros((n, n), F32)
        for p in range(npair):
            s_ref[p] = jnp.concatenate([jnp.concatenate([s0_ref[0, 2 * p], zero], axis=1),
                                        jnp.concatenate([zero, s0_ref[0, 2 * p + 1]], axis=1)], axis=0)

    first = lax.broadcasted_iota(jnp.int32, (ln, PAIR), 1) < n
    ri = lax.broadcasted_iota(jnp.int32, (rows, rows), 0)
    ci = lax.broadcasted_iota(jnp.int32, (rows, rows), 1)
    same = (ri // ln) == (ci // ln)
    incl = same & (ci <= ri)
    strict = same & (ci < ri)
    eye = jnp.where(ri == ci, 1.0, 0.0)
    ti = lax.broadcasted_iota(jnp.int32, (ln, ln), 0)
    tj = lax.broadcasted_iota(jnp.int32, (ln, ln), 1)
    tri = jnp.where(tj <= ti, 1.0, 0.0).astype(BF16)

    def seg_sum(x):
        s1 = jnp.sum(jnp.where(first, x, 0.0), axis=-1, keepdims=True)
        s2 = jnp.sum(jnp.where(first, 0.0, x), axis=-1, keepdims=True)
        return jnp.where(first, s1, s2)

    def by_head(x):
        return jnp.concatenate([jnp.where(first, x, 0.0), jnp.where(first, 0.0, x)], axis=0)

    pairs = range(npair)
    lanes = [slice(p * PAIR, (p + 1) * PAIR) for p in pairs]

    def prepare(p):
        r = r_ref[:, lanes[p]]
        lw = lw_ref[:, lanes[p]]
        k = k_ref[:, lanes[p]]
        v = v_ref[:, lanes[p]]
        a = a_ref[:, lanes[p]]
        if has_vres:
            v = v + (vf_ref[:, lanes[p]] - v) * vg_ref[:, lanes[p]]
        kk = k * kk_ref[:, lanes[p]]
        kk = kk / jnp.maximum(jnp.sqrt(seg_sum(kk * kk)), 1e-12)
        k2 = k * (1.0 + (a - 1.0) * ka_ref[:, lanes[p]])
        bv = kk * a
        hi, mid, lo = _split3(lw)
        cl3 = jnp.dot(tri, jnp.concatenate([hi, mid, lo], axis=1), preferred_element_type=F32)
        cl = cl3[:, :PAIR] + cl3[:, PAIR:2 * PAIR] + cl3[:, 2 * PAIR:]
        cl_end = cl[ln - 1:ln, :]
        g_inv = jnp.exp(-cl)
        g_end = jnp.exp(cl_end - cl)
        rt = r * jnp.exp(cl)
        at = -kk * jnp.exp(cl - lw)
        lhs = jnp.concatenate([by_head(at), by_head(rt)], axis=0)
        bh = bv * g_inv
        kh = k2 * g_inv
        return dict(r=r, v=v, k2=k2, lhs=lhs, rhs=jnp.concatenate([bh, bh, kh, kh], axis=0),
                    decay=jnp.exp(cl_end), v_h=by_head(v),
                    ws=jnp.concatenate([by_head(bv * g_end), by_head(k2 * g_end)], axis=0))

    st = [prepare(p) for p in pairs]
    aa = [_dot_nt(st[p]["lhs"], st[p]["rhs"]) for p in pairs]
    a_ak = [jnp.where(strict, aa[p][:rows, rows:], 0.0) for p in pairs]
    a_r = [jnp.concatenate([jnp.where(incl, aa[p][rows:, :rows], 0.0),
                            jnp.where(incl, aa[p][rows:, rows:], 0.0)], axis=1) for p in pairs]
    akv = [_dot_hilo(a_ak[p], st[p]["v_h"]) for p in pairs]
    def same_block(size):
        return (ri // size) == (ci // size)

    below = ci < ri
    tinv = [eye + jnp.where(below & same_block(2), aa[p][:rows, :rows], 0.0) for p in pairs]
    for level in range(1, int(math.log2(ln))):
        size = 2 ** level
        coupling = below & same_block(2 * size) & jnp.logical_not(same_block(size))
        n_off = [jnp.where(coupling, aa[p][:rows, :rows], 0.0) for p in pairs]
        right = [_dot(n_off[p], tinv[p]) for p in pairs]
        tinv = [tinv[p] + _dot(tinv[p], right[p]) for p in pairs]

    s_prev = [s_ref[p] for p in pairs]
    xy = [_dot_nt(st[p]["lhs"], s_prev[p]) for p in pairs]
    u = [_dot_hilo(tinv[p], xy[p][:rows] + akv[p]) for p in pairs]
    zs = [jnp.concatenate([u[p], st[p]["v_h"]], axis=0) for p in pairs]
    for p in pairs:
        s_ref[p] = s_prev[p] * st[p]["decay"] + _dot_tn_hilo(zs[p], st[p]["ws"])
    y_h = [xy[p][rows:] + _dot(a_r[p], zs[p]) for p in pairs]
    for p in pairs:
        y = y_h[p][:ln] + y_h[p][ln:]
        mean = seg_sum(y) * (1.0 / n)
        yc = y - mean
        var = seg_sum(yc * yc) * (1.0 / n)
        y = yc * lax.rsqrt(var + RW_LNX_EPS) * lnw_ref[:, lanes[p]] + lnb_ref[:, lanes[p]]
        y_ref[:, lanes[p]] = y + seg_sum(st[p]["r"] * st[p]["k2"] * rk_ref[:, lanes[p]]) * st[p]["v"]

    @pl.when(c == pl.num_programs(2) - 1)
    def _():
        for p in range(npair):
            sout_ref[0, 2 * p] = s_ref[p, :n, :n]
            sout_ref[0, 2 * p + 1] = s_ref[p, n:, n:]


def wkv_scan(r, lw, k, v, a, s0, p, batch, seq, npair, ln, vres=None):
    nc = seq // ln
    has_vres = vres is not None
    width = npair * PAIR
    tok = pl.BlockSpec((ln, width), lambda b, q, c: (b * nc + c, q))
    par = pl.BlockSpec((1, width), lambda b, q, c: (0, q))
    st = pl.BlockSpec((1, 2 * npair, RW_HEAD, RW_HEAD), lambda b, q, c: (b, q, 0, 0))
    seqs = [r, lw, k, v, a] + (list(vres) if has_vres else [])
    pars = [p[nm].reshape(1, D_MODEL) for nm in ("k_k", "k_a", "r_k", "lnx_w", "lnx_b")]
    return pl.pallas_call(
        functools.partial(_wkv_kernel, has_vres, npair, ln),
        grid=(batch, D_MODEL // width, nc),
        in_specs=[tok] * len(seqs) + [st] + [par] * 5,
        out_specs=[tok, st],
        out_shape=[jax.ShapeDtypeStruct(r.shape, F32), jax.ShapeDtypeStruct(s0.shape, F32)],
        scratch_shapes=[pltpu.VMEM((npair, PAIR, PAIR), F32)],
        compiler_params=_params("parallel", "parallel", "arbitrary"),
        name="wkv_scan",
    )(*seqs, s0, *pars)


def prep_rwkv(w, j, i):
    row = lambda z: z.reshape(1, -1).astype(F32)
    p = {
        "norm_mix": row(w["norm_mix"][i]),
        "mu": w["rwkv_mu"][j],
        "w_rkv": w["rwkv_w_rkv"][j].astype(BF16),
        "w0": row(w["rwkv_w0"][j]), "w1": w["rwkv_w1"][j].astype(BF16), "w2": w["rwkv_w2"][j].astype(BF16),
        "a0": row(w["rwkv_a0"][j]), "a1": w["rwkv_a1"][j].astype(BF16), "a2": w["rwkv_a2"][j].astype(BF16),
        "g1": w["rwkv_g1"][j].astype(BF16), "g2": w["rwkv_g2"][j].astype(BF16),
        "k_k": w["rwkv_k_k"][j], "k_a": w["rwkv_k_a"][j], "r_k": w["rwkv_r_k"][j],
        "lnx_w": w["rwkv_lnx_w"][j], "lnx_b": w["rwkv_lnx_b"][j],
        "w_o": w["rwkv_w_o"][j].astype(BF16),
    }
    if j > 0:
        p["v0"] = row(w["rwkv_v0"][j - 1])
        p["v1"] = w["rwkv_v1"][j - 1].astype(BF16)
        p["v2"] = w["rwkv_v2"][j - 1].astype(BF16)
    return p


def rwkv_layer(h, shift, s0, vfirst, p, batch, seq, tm, npair, ln):
    m, c = h.shape
    outs = rwkv_proj(h, shift, p, seq, tm)
    r, lw, k, v, a, g = outs[:6]
    vres = None if vfirst is None else (vfirst, outs[6])
    y, s_new = wkv_scan(r, lw, k, v, a, s0, p, batch, seq, npair, ln, vres)
    shift_new = rmsnorm_rows(h.reshape(batch, seq, c)[:, -1], p["norm_mix"], batch)
    h = linear_res(y, p["w_o"], h, tm, gate=g)
    return h, s_new, shift_new, (v if vfirst is None else vfirst)


SEG_PAD = 128


def _dot_hilo_rhs(x, ones):
    hi = x.astype(BF16)
    lo = (x - hi.astype(F32)).astype(BF16)
    return jnp.dot(hi, ones, preferred_element_type=F32) + jnp.dot(lo, ones, preferred_element_type=F32)


def _seg_rms_scale(x, seg_ref, segt_ref, invw_ref):
    ssq = _dot(x * x, seg_ref[...])
    return _dot_hilo_rhs(lax.rsqrt(ssq * invw_ref[...] + NORM_EPS), segt_ref[...])


def _swap_halves(x, group):
    half = group // 2
    width = x.shape[-1]
    lane = lax.broadcasted_iota(jnp.int32, x.shape, x.ndim - 1)
    return jnp.where((lane % group) < half,
                     pltpu.roll(x, width - half, axis=x.ndim - 1),
                     pltpu.roll(x, half, axis=x.ndim - 1))


SLOT = 128


def _mla_proj_kernel(with_kv, *refs):
    (h_ref, cos_ref, sin_ref, gmix_ref, win_ref, gq_ref, gkv_ref, gkr_ref, wuq_ref, gqc_ref,
     seg_ref, segt_ref, invw_ref) = refs[:13]
    if with_kv:
        wuk_ref, wuv_ref, gkn_ref, qc_ref, c_ref, kp_ref, kc_ref, v_ref = refs[13:]
    else:
        qc_ref, c_ref, kp_ref = refs[13:]
    x = _rms(h_ref[...], gmix_ref[...])
    hp = _dot(x, win_ref[...])
    q_a = _rms(hp[:, :MLA_Q_LORA], gq_ref[...])
    c = _rms(hp[:, MLA_Q_LORA:MLA_Q_LORA + MLA_KV_LORA], gkv_ref[...])
    c_ref[...] = c

    cos = cos_ref[...]
    sin = sin_ref[...]

    def rope(z, cs, sn):
        return z * cs + _swap_halves(z, MLA_ROPE) * sn

    kp_raw = hp[:, MLA_Q_LORA + MLA_KV_LORA:]
    kp_scale = lax.rsqrt(jnp.sum(kp_raw * kp_raw, axis=-1, keepdims=True) * (1.0 / MLA_ROPE) + NORM_EPS)
    kp = rope(pltpu.roll(kp_raw, MLA_NOPE, axis=1) * gkr_ref[...], cos, sin) * kp_scale
    kp_ref[...] = kp[:, MLA_NOPE:MLA_NOPE + MLA_ROPE]

    q = _dot(q_a, wuq_ref[...])
    cos_h = jnp.tile(cos, (1, MLA_HEADS))
    sin_h = jnp.tile(sin, (1, MLA_HEADS))
    qc = rope(q * gqc_ref[...], cos_h, sin_h) * _seg_rms_scale(q, seg_ref, segt_ref, invw_ref)
    qc_ref[...] = qc.astype(BF16)
    if with_kv:
        kraw = _dot(c, wuk_ref[...])
        kn = kraw * _seg_rms_scale(kraw, seg_ref, segt_ref, invw_ref) * gkn_ref[...]
        kc_ref[...] = (kn + jnp.tile(kp, (1, MLA_HEADS))).astype(BF16)
        v_ref[0] = _dot_nt(wuv_ref[...], c).astype(BF16)


def _slot_seg_matrix():
    lane = jnp.arange(MLA_HEADS * SLOT)
    head, off = lane // SLOT, lane % SLOT
    col = jnp.where(off < MLA_NOPE, head, jnp.where(off < MLA_NOPE + MLA_ROPE, MLA_HEADS + head, SEG_PAD))
    return (col[:, None] == jnp.arange(SEG_PAD)[None, :]).astype(BF16)


def _slot(nope, rope):
    pad = jnp.zeros(nope.shape[:-1] + (SLOT - MLA_NOPE - MLA_ROPE,), nope.dtype)
    return jnp.concatenate([nope, rope, pad], axis=-1)


def _rope_tables(pos):
    half = MLA_ROPE // 2
    inv = ROPE_THETA ** (-jnp.arange(half, dtype=F32) / half)
    ang = pos.astype(F32)[:, None] * inv[None, :]
    cos = jnp.cos(ang)
    sin = jnp.sin(ang)
    ones = jnp.ones((pos.shape[0], MLA_NOPE), F32)
    return (_slot(ones, jnp.concatenate([cos, cos], axis=1)),
            _slot(jnp.zeros_like(ones), jnp.concatenate([-sin, sin], axis=1)))


def mla_proj(h, pos, p, tm, with_kv):
    m, c = h.shape
    seq = pos.shape[0]
    cos, sin = _rope_tables(pos)
    if seq >= tm:
        nrep = seq // tm
    else:
        cos, sin = jnp.tile(cos, (tm // seq, 1)), jnp.tile(sin, (tm // seq, 1))
        nrep = 1
    row = lambda width: pl.BlockSpec((tm, width), lambda i: (i, 0))
    tab = pl.BlockSpec((tm, SLOT), lambda i: (i % nrep, 0))
    full = lambda arr: pl.BlockSpec(arr.shape, lambda i: (0,) * arr.ndim)
    seg = _slot_seg_matrix()
    invw = jnp.concatenate([jnp.full((MLA_HEADS,), 1.0 / MLA_NOPE, F32), jnp.full((MLA_HEADS,), 1.0 / MLA_ROPE, F32),
                            jnp.ones((SEG_PAD - 2 * MLA_HEADS,), F32)]).reshape(1, SEG_PAD)
    weights = [p["norm_mix"], p["w_in"], p["q_norm"], p["kv_norm"], p["kr_norm"], p["w_uq"],
               p["q_gain"] if with_kv else p["q_gain_abs"], seg, seg.T, invw]
    wide = MLA_HEADS * SLOT
    out_shape = [jax.ShapeDtypeStruct((m, wide), BF16),
                 jax.ShapeDtypeStruct((m, MLA_KV_LORA), F32), jax.ShapeDtypeStruct((m, MLA_ROPE), F32)]
    out_specs = [row(wide), row(MLA_KV_LORA), row(MLA_ROPE)]
    if with_kv:
        weights += [p["w_uk"], p["w_uv_t"], p["kn_gain"]]
        out_shape += [jax.ShapeDtypeStruct((m, wide), BF16),
                      jax.ShapeDtypeStruct((m // tm, MLA_HEADS * MLA_V, tm), BF16)]
        out_specs += [row(wide), pl.BlockSpec((1, MLA_HEADS * MLA_V, tm), lambda i: (i, 0, 0))]
    return pl.pallas_call(
        functools.partial(_mla_proj_kernel, with_kv),
        grid=(m // tm,),
        in_specs=[row(c), tab, tab] + [full(w) for w in weights],
        out_specs=out_specs,
        out_shape=out_shape,
        compiler_params=_params("parallel"),
        name="mla_proj",
    )(h, cos, sin, *weights)


def prep_mla(w, j, i):
    row = lambda z: z.reshape(1, -1).astype(F32)
    w_in = w["mla_w_in"][j]
    pad = jnp.zeros((D_MODEL, 128 - MLA_ROPE), F32)
    w_uq = w["mla_w_uq"][j]
    w_uk = w["mla_w_uk"][j]
    zero_n = jnp.zeros((MLA_NOPE,), F32)
    zero_r = jnp.zeros((MLA_ROPE,), F32)
    qn, qr, kn = w["mla_qn_norm"][j], w["mla_qr_norm"][j], w["mla_kn_norm"][j]
    return {
        "norm_mix": row(w["norm_mix"][i]),
        "w_in": jnp.concatenate([w_in, pad], axis=1).astype(BF16),
        "q_norm": row(w["mla_q_norm"][j]), "kv_norm": row(w["mla_kv_norm"][j]),
        "kr_norm": row(_slot(zero_n, w["mla_kr_norm"][j])),
        "w_uq": _slot(w_uq[:, :, :MLA_NOPE], w_uq[:, :, MLA_NOPE:]).reshape(MLA_Q_LORA, -1).astype(BF16),
        "q_gain": row(jnp.tile(_slot(qn, qr) * MLA_SCALE, MLA_HEADS)),
        "q_gain_abs": row(jnp.tile(_slot(qn * kn, qr) * MLA_SCALE, MLA_HEADS)),
        "kn_gain": row(jnp.tile(_slot(kn, zero_r), MLA_HEADS)),
        "w_uk": _slot(w_uk, jnp.zeros(w_uk.shape[:2] + (MLA_ROPE,), F32)).reshape(MLA_KV_LORA, -1).astype(BF16),
        "w_uv": w["mla_w_uv"][j].reshape(MLA_KV_LORA, -1).astype(BF16),
        "w_uv_t": w["mla_w_uv"][j].reshape(MLA_KV_LORA, -1).T.astype(BF16),
        "w_uk_t": w["mla_w_uk"][j].reshape(MLA_KV_LORA, -1).T.astype(BF16),
        "w_uk_heads": jnp.transpose(w["mla_w_uk"][j], (1, 2, 0)).astype(BF16),
        "w_o": w["mla_w_o"][j].astype(BF16),
    }


ATTN_HEADS_PER_STEP = 8
NEG_BIG = -1e30


def _flash_kernel(tq, qc_ref, kc_ref, vt_ref, o_ref, m_ref, l_ref, acc_ref):
    qi = pl.program_id(2)
    g = ATTN_HEADS_PER_STEP
    heads = range(g)
    key_i = lax.broadcasted_iota(jnp.int32, (tq, tq), 0)
    qry_i = lax.broadcasted_iota(jnp.int32, (tq, tq), 1)
    m_ref[...] = jnp.full_like(m_ref, NEG_BIG)
    l_ref[...] = jnp.zeros_like(l_ref)
    acc_ref[...] = jnp.zeros_like(acc_ref)

    def block(j, masked):
        keys = pl.ds(pl.multiple_of(j * tq, tq), tq)
        s = [_dot_nt(kc_ref[0, keys, hh * SLOT:(hh + 1) * SLOT], qc_ref[0, :, hh * SLOT:(hh + 1) * SLOT])
             for hh in heads]
        if masked:
            s = [jnp.where(key_i <= qry_i, s[hh], NEG_BIG) for hh in heads]
        m_prev = [m_ref[hh] for hh in heads]
        m_new = [jnp.maximum(m_prev[hh], jnp.max(s[hh], axis=0, keepdims=True)) for hh in heads]
        alpha = [jnp.exp(m_prev[hh] - m_new[hh]) for hh in heads]
        pr = [jnp.exp(s[hh] - m_new[hh]) for hh in heads]
        pv = [jnp.dot(vt_ref[j, hh * MLA_V:(hh + 1) * MLA_V, :], pr[hh].astype(BF16), preferred_element_type=F32)
              for hh in heads]
        for hh in heads:
            l_ref[hh] = alpha[hh] * l_ref[hh] + jnp.sum(pr[hh], axis=0, keepdims=True)
            acc_ref[hh] = alpha[hh] * acc_ref[hh] + pv[hh]
            m_ref[hh] = m_new[hh]

    def body(j, carry):
        block(j, False)
        return carry

    lax.fori_loop(0, qi, body, 0)
    block(qi, True)
    o_t = jnp.concatenate([acc_ref[hh] / l_ref[hh] for hh in heads], axis=0)
    o_ref[0] = o_t.T


def flash_prompt(qc, kc, vt, batch, seq, tq):
    g = ATTN_HEADS_PER_STEP
    nq = seq // tq
    r3 = lambda z: z.reshape(batch, seq, -1)
    return pl.pallas_call(
        functools.partial(_flash_kernel, tq),
        grid=(batch, MLA_HEADS // g, nq),
        in_specs=[pl.BlockSpec((1, tq, g * SLOT), lambda b, hq, i: (b, i, hq)),
                  pl.BlockSpec((1, seq, g * SLOT), lambda b, hq, i: (b, 0, hq)),
                  pl.BlockSpec((nq, g * MLA_V, tq), lambda b, hq, i: (b, hq, 0))],
        out_specs=pl.BlockSpec((1, tq, g * MLA_V), lambda b, hq, i: (b, i, hq)),
        out_shape=jax.ShapeDtypeStruct((batch, seq, MLA_HEADS * MLA_V), F32),
        scratch_shapes=[pltpu.VMEM((g, 1, tq), F32), pltpu.VMEM((g, 1, tq), F32),
                        pltpu.VMEM((g, MLA_V, tq), F32)],
        compiler_params=_params("parallel", "parallel", "parallel"),
        name="flash_prompt",
    )(r3(qc), r3(kc), vt).reshape(batch * seq, -1)


def _bmm_kernel(a_ref, b_ref, o_ref):
    o_ref[0] = _dot(a_ref[0], b_ref[0]).astype(o_ref.dtype)


def bmm(a, b, out_dtype):
    g, m, k = a.shape
    n = b.shape[2]
    return pl.pallas_call(
        _bmm_kernel,
        grid=(g,),
        in_specs=[pl.BlockSpec((1, m, k), lambda i: (i, 0, 0)), pl.BlockSpec((1, k, n), lambda i: (i, 0, 0))],
        out_specs=pl.BlockSpec((1, m, n), lambda i: (i, 0, 0)),
        out_shape=jax.ShapeDtypeStruct((g, m, n), out_dtype),
        compiler_params=_params("parallel"),
        name="bmm",
    )(a, b)


def _paged_attn_kernel(pp, seq, *refs):
    pt_ref = refs[0]
    qa_ref, qp_ref, cn_ref, kpn_ref, wukt_ref, wuv_ref = refs[1:7]
    c_refs = refs[7:7 + pp]
    kp_refs = refs[7 + pp:7 + 2 * pp]
    o_ref, m_ref, l_ref, acc_ref, lhs_ref = refs[7 + 2 * pp:]
    del pt_ref
    s_idx = pl.program_id(1)
    nq = seq * MLA_HEADS
    nup = MLA_HEADS * MLA_NOPE

    @pl.when(s_idx == 0)
    def _():
        m_ref[...] = jnp.full_like(m_ref, NEG_BIG)
        l_ref[...] = jnp.zeros_like(l_ref)
        acc_ref[...] = jnp.zeros_like(acc_ref)
        lhs_ref[:nup, :] = wukt_ref[...]
        lhs_ref[nup:, :] = qa_ref[0]

    qp = qp_ref[0]

    def attend(c_blks, kp_blks, mask):
        subs = range(len(c_blks))
        nk = c_blks[0].shape[0]
        c_bf = [c_blks[i].astype(BF16) for i in subs]
        both = [_dot_nt(lhs_ref[...], c_bf[i]) for i in subs]
        ssq = [jnp.sum(jnp.square(both[i][:nup]).reshape(MLA_HEADS, MLA_NOPE, nk), axis=1) for i in subs]
        rs = [lax.rsqrt(ssq[i] * (1.0 / MLA_NOPE) + NORM_EPS) for i in subs]
        s = [both[i][nup:] * jnp.concatenate([rs[i]] * seq, axis=0) + _dot_nt(qp, kp_blks[i]) for i in subs]
        if mask is not None:
            s = [jnp.where(mask, s[i], NEG_BIG) for i in subs]
        m_prev = m_ref[...]
        m_new = m_prev
        for i in subs:
            m_new = jnp.maximum(m_new, jnp.max(s[i], axis=-1, keepdims=True))
        alpha = jnp.exp(m_prev - m_new)
        pr = [jnp.exp(s[i] - m_new) for i in subs]
        l_new = alpha * l_ref[...]
        acc = alpha * acc_ref[...]
        for i in subs:
            l_new = l_new + jnp.sum(pr[i], axis=-1, keepdims=True)
            acc = acc + _dot(pr[i], c_bf[i])
        l_ref[...] = l_new
        acc_ref[...] = acc
        m_ref[...] = m_new

    pages_per_sub = 2
    groups = [range(i, i + pages_per_sub) for i in range(0, pp, pages_per_sub)]
    attend([jnp.concatenate([c_refs[i][0] for i in grp], axis=0) for grp in groups],
           [jnp.concatenate([kp_refs[i][0] for i in grp], axis=0) for grp in groups], None)

    @pl.when(s_idx == pl.num_programs(1) - 1)
    def _():
        qtok = lax.broadcasted_iota(jnp.int32, (nq, seq), 0) // MLA_HEADS
        ktok = lax.broadcasted_iota(jnp.int32, (nq, seq), 1)
        attend([cn_ref[0]], [kpn_ref[0]], ktok <= qtok)
        o_lat = acc_ref[...] / l_ref[...]
        full = _dot(o_lat, wuv_ref[...])
        rhead = lax.broadcasted_iota(jnp.int32, full.shape, 0) % MLA_HEADS
        lhead = lax.broadcasted_iota(jnp.int32, full.shape, 1) // MLA_V
        full = jnp.where(rhead == lhead, full, 0.0)
        o_ref[0] = jnp.sum(full.reshape(seq, MLA_HEADS, MLA_HEADS * MLA_V), axis=1)


def paged_attn(q_abs, qp, c_new, kp_new, pool_c, pool_kp, page_table, p, batch, seq, pp):
    n_pages = page_table.shape[1]
    ns = n_pages // pp
    nq = seq * MLA_HEADS
    per_b = lambda shp: pl.BlockSpec((1,) + shp, lambda b, s, pt: (b, 0, 0))
    full = lambda arr: pl.BlockSpec(arr.shape, lambda b, s, pt: (0,) * arr.ndim)

    def page(width, i):
        return pl.BlockSpec((1, PAGE_SIZE, width), lambda b, s, pt: (pt[b * n_pages + s * pp + i], 0, 0))

    grid_spec = pltpu.PrefetchScalarGridSpec(
        num_scalar_prefetch=1,
        grid=(batch, ns),
        in_specs=[per_b((nq, MLA_KV_LORA)), per_b((nq, MLA_ROPE)), per_b((seq, MLA_KV_LORA)),
                  per_b((seq, MLA_ROPE)), full(p["w_uk_t"]), full(p["w_uv"])]
                 + [page(MLA_KV_LORA, i) for i in range(pp)] + [page(MLA_ROPE, i) for i in range(pp)],
        out_specs=per_b((seq, MLA_HEADS * MLA_V)),
        scratch_shapes=[pltpu.VMEM((nq, 1), F32), pltpu.VMEM((nq, 1), F32), pltpu.VMEM((nq, MLA_KV_LORA), F32),
                        pltpu.VMEM((MLA_HEADS * MLA_NOPE + nq, MLA_KV_LORA), BF16)],
    )
    return pl.pallas_call(
        functools.partial(_paged_attn_kernel, pp, seq),
        grid_spec=grid_spec,
        out_shape=jax.ShapeDtypeStruct((batch, seq, MLA_HEADS * MLA_V), F32),
        compiler_params=_params("parallel", "arbitrary"),
        name="paged_attn",
    )(page_table.reshape(-1), q_abs, qp, c_new.reshape(batch, seq, -1), kp_new.reshape(batch, seq, -1),
      p["w_uk_t"], p["w_uv"], *([pool_c] * pp), *([pool_kp] * pp))


def mla_layer_prompt(h, p, batch, seq, tm, tq):
    assert tm == tq, "the value tiles written by mla_proj are the key blocks of flash_prompt"
    qc, c, kp, kc, vt = mla_proj(h, jnp.arange(seq), p, tm, True)
    o = flash_prompt(qc, kc, vt, batch, seq, tq)
    return linear_res(o, p["w_o"], h, tm), c, kp


def mla_layer_sample(h, pool_c, pool_kp, page_table, p, batch, seq, past_len, tm, pp):
    m = batch * seq
    qc, c, kp = mla_proj(h, past_len + jnp.arange(seq), p, tm, False)
    qc = qc.reshape(m, MLA_HEADS, SLOT)
    q_heads = jnp.swapaxes(qc[:, :, :MLA_NOPE], 0, 1)
    q_abs = bmm(q_heads, p["w_uk_heads"], BF16)
    q_abs = jnp.swapaxes(q_abs, 0, 1).reshape(batch, seq * MLA_HEADS, MLA_KV_LORA)
    qp = qc[:, :, MLA_NOPE:MLA_NOPE + MLA_ROPE].reshape(batch, seq * MLA_HEADS, MLA_ROPE)
    o = paged_attn(q_abs, qp, c, kp, pool_c, pool_kp, page_table, p, batch, seq, pp)
    return linear_res(o.reshape(m, -1), p["w_o"], h, tm), c, kp


DT_PAD = 128
CONV_HALO = 8
MB_GN = MB_GROUPS * MB_STATE
MB_GROUP_INNER = MB_INNER // MB_GROUPS
MB_HEADS_PER_GROUP = MB_HEADS // MB_GROUPS


def _ssd_kernel(ck, xbc_ref, prev_ref, cs_ref, z_ref, dtr_ref, h0_ref, cw_ref, cb_ref, dtb_ref, alog_ref,
                dskip_ref, nw_ref, expand_ref, y_ref, hout_ref, h_ref):
    c = pl.program_id(1)

    @pl.when(c == 0)
    def _():
        h_ref[...] = h0_ref[0]

    halo = jnp.where(c == 0, cs_ref[0], prev_ref[0])
    xext = jnp.concatenate([halo, xbc_ref[0]], axis=0)
    conv = cb_ref[...] + xext[CONV_HALO:, :] * cw_ref[MB_CONV - 1:MB_CONV, :]
    for j in range(MB_CONV - 1):
        shifted = pltpu.roll(xext, MB_CONV - 1 - j, axis=0)[CONV_HALO:, :]
        conv = conv + shifted * cw_ref[j:j + 1, :]
    xbc = _silu(conv)
    xs = xbc[:, :MB_INNER]
    bm = xbc[:, MB_INNER:MB_INNER + MB_GN]
    cm = xbc[:, MB_INNER + MB_GN:]

    dt = _softplus(dtr_ref[0] + dtb_ref[...])
    da = dt * (-jnp.exp(alog_ref[...]))
    ri = lax.broadcasted_iota(jnp.int32, (ck, ck), 0)
    ci = lax.broadcasted_iota(jnp.int32, (ck, ck), 1)
    causal = ci <= ri
    tri = jnp.where(causal, 1.0, 0.0).astype(BF16)
    hi, mid, lo = _split3(da)
    acum = (jnp.dot(tri, hi, preferred_element_type=F32) + jnp.dot(tri, mid, preferred_element_type=F32)
            + jnp.dot(tri, lo, preferred_element_type=F32))
    tn = (((0,), (0,)), ((), ()))
    tri_t = jnp.where(ri <= ci, 1.0, 0.0).astype(BF16)
    acum_t = (lax.dot_general(hi, tri_t, tn, preferred_element_type=F32)
              + lax.dot_general(mid, tri_t, tn, preferred_element_type=F32)
              + lax.dot_general(lo, tri_t, tn, preferred_element_type=F32))
    e_last = jnp.exp(acum[ck - 1:ck, :])

    spread = expand_ref[...]
    dt_hi = dt.astype(BF16)
    dt_lo = (dt - dt_hi.astype(F32)).astype(BF16)
    dt_x = jnp.dot(dt_hi, spread, preferred_element_type=F32) + jnp.dot(dt_lo, spread, preferred_element_type=F32)
    ah, am, al = _split3(acum)
    acum_x = (jnp.dot(ah, spread, preferred_element_type=F32) + jnp.dot(am, spread, preferred_element_type=F32)
              + jnp.dot(al, spread, preferred_element_type=F32))
    xdt = xs * dt_x
    xdt_end = xdt * jnp.exp(acum_x[ck - 1:ck, :] - acum_x)
    e_cum_x = jnp.exp(acum_x)

    pair_w = 2 * MB_HEAD
    first = lax.broadcasted_iota(jnp.int32, (ck, pair_w), 1) < MB_HEAD
    upper = lax.broadcasted_iota(jnp.int32, (pair_w, MB_STATE), 0) < MB_HEAD
    pairs_per_group = MB_HEADS_PER_GROUP // 2
    ys = []
    for g in range(MB_GROUPS):
        b_g = bm[:, g * MB_STATE:(g + 1) * MB_STATE]
        c_g = cm[:, g * MB_STATE:(g + 1) * MB_STATE]
        cb = _dot_nt(c_g, b_g)
        cols = slice(g * MB_GROUP_INNER, (g + 1) * MB_GROUP_INNER)
        y_state = _dot_nt(c_g, h_ref[cols, :]) * e_cum_x[:, cols]
        lmat = []
        for hh in range(MB_HEADS_PER_GROUP):
            hd = g * MB_HEADS_PER_GROUP + hh
            seg = acum[:, hd:hd + 1] - acum_t[hd:hd + 1, :]
            lmat.append(cb * jnp.where(causal, jnp.exp(jnp.where(causal, seg, 0.0)), 0.0))
        for pr in range(pairs_per_group):
            lanes = slice(g * MB_GROUP_INNER + pr * pair_w, g * MB_GROUP_INNER + (pr + 1) * pair_w)
            x_p = xdt[:, lanes]
            y_p = jnp.where(first, _dot(lmat[2 * pr], x_p), _dot(lmat[2 * pr + 1], x_p))
            ys.append(y_p + y_state[:, pr * pair_w:(pr + 1) * pair_w])
            hd = g * MB_HEADS_PER_GROUP + 2 * pr
            decay = jnp.where(upper, e_last[:, hd:hd + 1], e_last[:, hd + 1:hd + 2])
            h_ref[lanes, :] = h_ref[lanes, :] * decay + _dot_tn(xdt_end[:, lanes], b_g)

    y = jnp.concatenate(ys, axis=1) + dskip_ref[...] * xs
    yz = y * _silu(z_ref[0])
    outs = []
    for g in range(MB_GROUPS):
        yg = yz[:, g * MB_GROUP_INNER:(g + 1) * MB_GROUP_INNER]
        outs.append(yg * lax.rsqrt(jnp.mean(yg * yg, axis=-1, keepdims=True) + NORM_EPS))
    y_ref[0] = (jnp.concatenate(outs, axis=1) * nw_ref[...]).astype(y_ref.dtype)

    @pl.when(c == pl.num_programs(1) - 1)
    def _():
        hout_ref[0] = h_ref[...]


def ssd_scan(xbc, z, dt_raw, conv_state, h0, p, batch, seq, ck):
    assert CONV_HALO % 8 == 0 and ck % CONV_HALO == 0
    nc = seq // ck
    halo_blocks = ck // CONV_HALO
    cs8 = jnp.concatenate([jnp.zeros((batch, CONV_HALO - (MB_CONV - 1), MB_CONV_DIM), F32), conv_state], axis=1)
    chunk = lambda width: pl.BlockSpec((1, ck, width), lambda b, c: (b, c, 0))
    full = lambda arr: pl.BlockSpec(arr.shape, lambda b, c: (0,) * arr.ndim)
    expand = (jnp.arange(DT_PAD)[:, None] == jnp.arange(MB_INNER)[None, :] // MB_HEAD).astype(BF16)
    weights = [p["conv_w"], p["conv_b"], p["dt_bias"], p["a_log"], p["d_skip"], p["norm_w"], expand]
    state = pl.BlockSpec((1, MB_INNER, MB_STATE), lambda b, c: (b, 0, 0))
    return pl.pallas_call(
        functools.partial(_ssd_kernel, ck),
        grid=(batch, nc),
        in_specs=[chunk(MB_CONV_DIM),
                  pl.BlockSpec((1, CONV_HALO, MB_CONV_DIM),
                               lambda b, c: (b, jnp.maximum(c * halo_blocks - 1, 0), 0)),
                  pl.BlockSpec((1, CONV_HALO, MB_CONV_DIM), lambda b, c: (b, 0, 0)),
                  chunk(MB_INNER), chunk(DT_PAD), state] + [full(w) for w in weights],
        out_specs=[chunk(MB_INNER), state],
        out_shape=[jax.ShapeDtypeStruct((batch, seq, MB_INNER), BF16),
                   jax.ShapeDtypeStruct((batch, MB_INNER, MB_STATE), F32)],
        scratch_shapes=[pltpu.VMEM((MB_INNER, MB_STATE), F32)],
        compiler_params=_params("parallel", "arbitrary"),
        name="ssd_scan",
    )(xbc, xbc, cs8, z, dt_raw, h0, *weights)


def prep_mamba(w, j, i):
    row = lambda z: z.reshape(1, -1).astype(F32)
    w_in = w["mamba_w_in"][j]
    padv = lambda z: jnp.concatenate([z.astype(F32), jnp.zeros((DT_PAD - MB_HEADS,), F32)])
    return {
        "norm_mix": w["norm_mix"][i],
        "w_z": w_in[:, :MB_INNER].astype(BF16),
        "w_xbc": w_in[:, MB_INNER:MB_INNER + MB_CONV_DIM].astype(BF16),
        "w_dt": jnp.concatenate([w_in[:, MB_INNER + MB_CONV_DIM:], jnp.zeros((D_MODEL, DT_PAD - MB_HEADS), F32)],
                                axis=1).astype(BF16),
        "conv_w": w["mamba_conv_w"][j], "conv_b": row(w["mamba_conv_b"][j]),
        "dt_bias": row(padv(w["mamba_dt_bias"][j])), "a_log": row(padv(w["mamba_a_log"][j])),
        "d_skip": row(jnp.repeat(w["mamba_d"][j].astype(F32), MB_HEAD)),
        "norm_w": row(w["mamba_norm"][j]),
        "w_o": w["mamba_w_o"][j].astype(BF16),
    }


def mamba_layer(h, conv_state, h0, p, batch, seq, tm, ck):
    m = batch * seq
    assert seq >= MB_CONV - 1
    tm_in = min(TM_FFN, m)
    z = norm_linear(h, p["norm_mix"], p["w_z"], tm_in, 1024)
    xbc = norm_linear(h, p["norm_mix"], p["w_xbc"], tm_in, 1024)
    dt_raw = norm_linear(h, p["norm_mix"], p["w_dt"], tm_in, DT_PAD)
    xbc3 = xbc.reshape(batch, seq, MB_CONV_DIM)
    y, h_new = ssd_scan(xbc3, z.reshape(batch, seq, MB_INNER), dt_raw.reshape(batch, seq, DT_PAD),
                        conv_state, h0.reshape(batch, MB_INNER, MB_STATE), p, batch, seq, ck)
    h = linear_res(y.reshape(m, MB_INNER), p["w_o"], h, tm)
    return h, xbc3[:, seq - (MB_CONV - 1):], h_new.reshape(batch, MB_HEADS, MB_HEAD, MB_STATE)


N_MIXERS = 3
TM_PROJ = 256
TM_ROWS = 512
TM_FFN = 1024
TH_FFN = 512
WKV_CHUNK = 64
WKV_PAIRS_PER_STEP = 8
ATTN_TQ = 256
PAGES_PER_STEP = 8


def kernel(x_prompt, x_sample, cache_mla_ckv, cache_mla_kpe, state_rwkv_wkv, state_rwkv_shift, state_ssm, state_conv, page_table, norm_mix, norm_ffn, ffn_w1, ffn_w2, rwkv_mu, rwkv_w_rkv, rwkv_w0, rwkv_w1, rwkv_w2, rwkv_a0, rwkv_a1, rwkv_a2, rwkv_v0, rwkv_v1, rwkv_v2, rwkv_g1, rwkv_g2, rwkv_k_k, rwkv_k_a, rwkv_r_k, rwkv_lnx_w, rwkv_lnx_b, rwkv_w_o, mla_w_in, mla_q_norm, mla_kv_norm, mla_w_uq, mla_w_uk, mla_w_uv, mla_qn_norm, mla_qr_norm, mla_kn_norm, mla_kr_norm, mla_w_o, mamba_w_in, mamba_conv_w, mamba_conv_b, mamba_dt_bias, mamba_a_log, mamba_d, mamba_norm, mamba_w_o):
    w = dict(locals())
    bp, tp, c = x_prompt.shape
    bs, ts, _ = x_sample.shape
    depth = norm_mix.shape[0]
    past_len = page_table.shape[1] * PAGE_SIZE
    hp = x_prompt.reshape(bp * tp, c)
    hs = x_sample.reshape(bs * ts, c)
    vf_p = vf_s = None
    out = {k: [] for k in ("ckv_p", "kpe_p", "ckv_s", "kpe_s", "wkv_p", "sh_p", "wkv_s", "sh_s",
                           "ssm_p", "conv_p", "ssm_s", "conv_s")}
    for i in range(depth):
        kind, j = i % N_MIXERS, i // N_MIXERS
        if kind == 0:
            p = prep_rwkv(w, j, i)
            hp, s_p, l_p, vf_p = rwkv_layer(hp, jnp.zeros((bp, c), F32), jnp.zeros((bp, RW_HEADS, RW_HEAD, RW_HEAD), F32),
                                            vf_p, p, bp, tp, TM_PROJ, WKV_PAIRS_PER_STEP, WKV_CHUNK)
            hs, s_s, l_s, vf_s = rwkv_layer(hs, state_rwkv_shift[j], state_rwkv_wkv[j], vf_s, p, bs, ts,
                                            TM_PROJ, WKV_PAIRS_PER_STEP, ts)
            out["wkv_p"].append(s_p); out["sh_p"].append(l_p); out["wkv_s"].append(s_s); out["sh_s"].append(l_s)
        elif kind == 1:
            p = prep_mla(w, j, i)
            hp, c_p, k_p = mla_layer_prompt(hp, p, bp, tp, TM_PROJ, ATTN_TQ)
            hs, c_s, k_s = mla_layer_sample(hs, cache_mla_ckv[j], cache_mla_kpe[j], page_table, p, bs, ts,
                                            past_len, TM_PROJ, PAGES_PER_STEP)
            out["ckv_p"].append(c_p.reshape(bp, tp, -1)); out["kpe_p"].append(k_p.reshape(bp, tp, -1))
            out["ckv_s"].append(c_s.reshape(bs, ts, -1)); out["kpe_s"].append(k_s.reshape(bs, ts, -1))
        else:
            p = prep_mamba(w, j, i)
            hp, cv_p, h_p = mamba_layer(hp, jnp.zeros((bp, MB_CONV - 1, MB_CONV_DIM), F32),
                                        jnp.zeros((bp, MB_HEADS, MB_HEAD, MB_STATE), F32), p, bp, tp,
                                        TM_ROWS, MB_CHUNK)
            hs, cv_s, h_s = mamba_layer(hs, state_conv[j], state_ssm[j], p, bs, ts, TM_ROWS, math.gcd(ts, MB_CHUNK))
            out["ssm_p"].append(h_p); out["conv_p"].append(cv_p); out["ssm_s"].append(h_s); out["conv_s"].append(cv_s)
        w1, w2 = ffn_w1[i].astype(BF16), ffn_w2[i].astype(BF16)
        hp = ffn_res(hp, norm_ffn[i], w1, w2, TM_FFN, TH_FFN)
        hs = ffn_res(hs, norm_ffn[i], w1, w2, TM_FFN, TH_FFN)
    stack = lambda k: jnp.stack(out[k])
    return (hp.reshape(bp, tp, c), hs.reshape(bs, ts, c),
            stack("ckv_p"), stack("kpe_p"), stack("ckv_s"), stack("kpe_s"),
            stack("wkv_p"), stack("sh_p"), stack("wkv_s"), stack("sh_s"),
            stack("ssm_p"), stack("conv_p"), stack("ssm_s"), stack("conv_s"))
```

```python
import functools
import math

import jax
import jax.numpy as jnp
from jax import lax
from jax.experimental import pallas as pl
from jax.experimental.pallas import tpu as pltpu

F32 = jnp.float32
BF16 = jnp.bfloat16

D_MODEL = 1024
NORM_EPS = 1e-6

RW_HEAD = 64
RW_HEADS = D_MODEL // RW_HEAD
RW_LNX_EPS = 64e-5

MLA_HEADS = 16
MLA_Q_LORA = 512
MLA_KV_LORA = 256
MLA_NOPE = 64
MLA_ROPE = 32
MLA_V = 64
MLA_SCALE = 1.0 / math.sqrt(MLA_NOPE + MLA_ROPE)
ROPE_THETA = 10000.0
PAGE_SIZE = 128

MB_INNER = 2 * D_MODEL
MB_HEAD = 64
MB_HEADS = MB_INNER // MB_HEAD
MB_GROUPS = 4
MB_STATE = 128
MB_CONV = 4
MB_CONV_DIM = MB_INNER + 2 * MB_GROUPS * MB_STATE
MB_CHUNK = 128

FFN_HIDDEN = 4 * D_MODEL

VMEM_LIMIT_BYTES = 56 * 2**20


def _params(*sem):
    return pltpu.CompilerParams(dimension_semantics=sem, vmem_limit_bytes=VMEM_LIMIT_BYTES)


def _dot(a, b):
    return jnp.dot(a.astype(BF16), b.astype(BF16), preferred_element_type=F32)


def _dot_nt(a, b):
    return lax.dot_general(a.astype(BF16), b.astype(BF16), (((1,), (1,)), ((), ())),
                           preferred_element_type=F32)


def _dot_tn(a, b):
    return lax.dot_general(a.astype(BF16), b.astype(BF16), (((0,), (0,)), ((), ())),
                           preferred_element_type=F32)


def _split3(a):
    hi = a.astype(BF16)
    r1 = a - hi.astype(F32)
    mid = r1.astype(BF16)
    lo = (r1 - mid.astype(F32)).astype(BF16)
    return hi, mid, lo


def _dot_exact_lhs(ones, x):
    hi, mid, lo = _split3(x)
    o = ones.astype(BF16)
    return (jnp.dot(o, hi, preferred_element_type=F32)
            + jnp.dot(o, mid, preferred_element_type=F32)
            + jnp.dot(o, lo, preferred_element_type=F32))


def _dot_exact_rhs(x, ones):
    hi, mid, lo = _split3(x)
    o = ones.astype(BF16)
    return (jnp.dot(hi, o, preferred_element_type=F32)
            + jnp.dot(mid, o, preferred_element_type=F32)
            + jnp.dot(lo, o, preferred_element_type=F32))


def _rms(x, g):
    return x * lax.rsqrt(jnp.mean(x * x, axis=-1, keepdims=True) + NORM_EPS) * g


def _softplus(z):
    return jnp.maximum(z, 0.0) + jnp.log(1.0 + jnp.exp(-jnp.abs(z)))


def _sigmoid(z):
    return 0.5 * jnp.tanh(0.5 * z) + 0.5


def _silu(z):
    return z * _sigmoid(z)


def _rmsnorm_kernel(x_ref, g_ref, o_ref):
    o_ref[...] = _rms(x_ref[...], g_ref[...])


def rmsnorm_rows(x, g, tm):
    m, c = x.shape
    return pl.pallas_call(
        _rmsnorm_kernel,
        grid=(m // tm,),
        in_specs=[pl.BlockSpec((tm, c), lambda i: (i, 0)), pl.BlockSpec((1, c), lambda i: (0, 0))],
        out_specs=pl.BlockSpec((tm, c), lambda i: (i, 0)),
        out_shape=jax.ShapeDtypeStruct((m, c), F32),
        compiler_params=_params("parallel"),
        name="rmsnorm_rows",
    )(x, g.reshape(1, c))


def _linear_res_kernel(x_ref, w_ref, h_ref, o_ref):
    o_ref[...] = h_ref[...] + _dot(x_ref[...], w_ref[...])


def _gated_linear_res_kernel(x_ref, gate_ref, w_ref, h_ref, o_ref):
    o_ref[...] = h_ref[...] + _dot(x_ref[...] * gate_ref[...], w_ref[...])


def linear_res(x, w, h, tm, gate=None):
    m, k = x.shape
    n = w.shape[1]
    row = lambda width: pl.BlockSpec((tm, width), lambda i: (i, 0))
    wspec = pl.BlockSpec((k, n), lambda i: (0, 0))
    if gate is None:
        kern, args, specs = _linear_res_kernel, (x, w, h), [row(k), wspec, row(n)]
    else:
        kern, args, specs = _gated_linear_res_kernel, (x, gate, w, h), [row(k), row(k), wspec, row(n)]
    return pl.pallas_call(
        kern,
        grid=(m // tm,),
        in_specs=specs,
        out_specs=row(n),
        out_shape=jax.ShapeDtypeStruct((m, n), F32),
        compiler_params=_params("parallel"),
        name="linear_res",
    )(*args)


def _norm_linear_kernel(x_ref, g_ref, w_ref, o_ref, xn_ref):
    @pl.when(pl.program_id(1) == 0)
    def _():
        xn_ref[...] = _rms(x_ref[...], g_ref[...]).astype(BF16)

    o_ref[...] = jnp.dot(xn_ref[...], w_ref[...], preferred_element_type=F32)


def norm_linear(x, g, w, tm, tn):
    m, k = x.shape
    n = w.shape[1]
    return pl.pallas_call(
        _norm_linear_kernel,
        grid=(m // tm, n // tn),
        in_specs=[pl.BlockSpec((tm, k), lambda i, j: (i, 0)),
                  pl.BlockSpec((1, k), lambda i, j: (0, 0)),
                  pl.BlockSpec((k, tn), lambda i, j: (0, j))],
        out_specs=pl.BlockSpec((tm, tn), lambda i, j: (i, j)),
        out_shape=jax.ShapeDtypeStruct((m, n), F32),
        scratch_shapes=[pltpu.VMEM((tm, k), BF16)],
        compiler_params=_params("parallel", "arbitrary"),
        name="norm_linear",
    )(x, g.reshape(1, k), w)


def _ffn_kernel(h_ref, g_ref, w1_ref, w2_ref, o_ref, xn_ref, acc_ref):
    j = pl.program_id(1)

    @pl.when(j == 0)
    def _():
        xn_ref[...] = _rms(h_ref[...], g_ref[...]).astype(BF16)
        acc_ref[...] = jnp.zeros_like(acc_ref)

    u = jnp.dot(xn_ref[...], w1_ref[...], preferred_element_type=F32)
    u = jnp.square(jnp.maximum(u, 0.0))
    acc_ref[...] += jnp.dot(u.astype(BF16), w2_ref[...], preferred_element_type=F32)

    @pl.when(j == pl.num_programs(1) - 1)
    def _():
        o_ref[...] = h_ref[...] + acc_ref[...]


def ffn_res(h, g, w1, w2, tm, th):
    m, c = h.shape
    hid = w1.shape[1]
    return pl.pallas_call(
        _ffn_kernel,
        grid=(m // tm, hid // th),
        in_specs=[pl.BlockSpec((tm, c), lambda i, j: (i, 0)),
                  pl.BlockSpec((1, c), lambda i, j: (0, 0)),
                  pl.BlockSpec((c, th), lambda i, j: (0, j)),
                  pl.BlockSpec((th, c), lambda i, j: (j, 0))],
        out_specs=pl.BlockSpec((tm, c), lambda i, j: (i, 0)),
        out_shape=jax.ShapeDtypeStruct((m, c), F32),
        scratch_shapes=[pltpu.VMEM((tm, c), BF16), pltpu.VMEM((tm, c), F32)],
        compiler_params=_params("parallel", "arbitrary"),
        name="ffn_res",
    )(h, g.reshape(1, c), w1, w2)


SHIFT_HALO = 8


def _rwkv_proj_kernel(has_vres, seq, tm, *refs):
    if has_vres:
        (h_ref, halo_ref, sh_ref, gmix_ref, mu_ref, wrkv_ref, w0_ref, w1_ref, w2_ref, a0_ref, a1_ref, a2_ref,
         g1_ref, g2_ref, v0_ref, v1_ref, v2_ref,
         r_ref, lw_ref, k_ref, v_ref, a_ref, g_ref, vg_ref) = refs
    else:
        (h_ref, halo_ref, sh_ref, gmix_ref, mu_ref, wrkv_ref, w0_ref, w1_ref, w2_ref, a0_ref, a1_ref, a2_ref,
         g1_ref, g2_ref,
         r_ref, lw_ref, k_ref, v_ref, a_ref, g_ref) = refs
    x = _rms(h_ref[...], gmix_ref[...])
    rolled = pltpu.roll(x, 1, axis=0)
    row = lax.broadcasted_iota(jnp.int32, x.shape, 0)
    if seq >= tm:
        is_start = (pl.program_id(0) % (seq // tm)) == 0
        tail = _rms(halo_ref[...], gmix_ref[...])[SHIFT_HALO - 1:SHIFT_HALO, :]
        xprev = jnp.where(row == 0, jnp.where(is_start, sh_ref[0], tail), rolled)
    else:
        n_seq = tm // seq
        starts = jnp.broadcast_to(sh_ref[...][:, None, :], (n_seq, seq, x.shape[1])).reshape(x.shape)
        xprev = jnp.where(row % seq == 0, starts, rolled)
    dx = xprev - x
    xm = [(x + dx * mu_ref[p:p + 1, :]).astype(BF16) for p in range(6)]
    r_ref[...] = jnp.dot(xm[0], wrkv_ref[0], preferred_element_type=F32)
    k_ref[...] = jnp.dot(xm[1], wrkv_ref[1], preferred_element_type=F32)
    v_ref[...] = jnp.dot(xm[2], wrkv_ref[2], preferred_element_type=F32)
    wpre = w0_ref[...] + _dot(jnp.tanh(jnp.dot(xm[3], w1_ref[...], preferred_element_type=F32)), w2_ref[...])
    w_log = -_softplus(-wpre) - 0.5
    lw_ref[...] = -jnp.exp(w_log)
    a_ref[...] = _sigmoid(a0_ref[...] + _dot(jnp.dot(xm[4], a1_ref[...], preferred_element_type=F32), a2_ref[...]))
    g_ref[...] = _dot(_sigmoid(jnp.dot(xm[5], g1_ref[...], preferred_element_type=F32)), g2_ref[...])
    if has_vres:
        vg_ref[...] = _sigmoid(v0_ref[...] + _dot(jnp.dot(xm[2], v1_ref[...], preferred_element_type=F32),
                                                  v2_ref[...]))


def rwkv_proj(h, shift, p, seq, tm):
    m, c = h.shape
    assert tm % SHIFT_HALO == 0 and (seq % tm == 0 or tm % seq == 0)
    has_vres = "v1" in p
    row = pl.BlockSpec((tm, c), lambda i: (i, 0))
    halo = pl.BlockSpec((SHIFT_HALO, c), lambda i: (jnp.maximum(i * (tm // SHIFT_HALO) - 1, 0), 0))
    if seq >= tm:
        shift_arg = shift.reshape(-1, 1, c)
        shift_spec = pl.BlockSpec((1, 1, c), lambda i: (i // (seq // tm), 0, 0))
    else:
        shift_arg = shift
        shift_spec = pl.BlockSpec((tm // seq, c), lambda i: (i, 0))
    full = lambda arr: pl.BlockSpec(arr.shape, lambda i: (0,) * arr.ndim, pipeline_mode=pl.Buffered(1))
    names = ["norm_mix", "mu", "w_rkv", "w0", "w1", "w2", "a0", "a1", "a2", "g1", "g2"]
    if has_vres:
        names += ["v0", "v1", "v2"]
    weights = [p[n] for n in names]
    n_out = 7 if has_vres else 6
    return pl.pallas_call(
        functools.partial(_rwkv_proj_kernel, has_vres, seq, tm),
        grid=(m // tm,),
        in_specs=[row, halo, shift_spec] + [full(w) for w in weights],
        out_specs=[row] * n_out,
        out_shape=[jax.ShapeDtypeStruct((m, c), F32)] * n_out,
        compiler_params=_params("parallel"),
        name="rwkv_proj",
    )(h, h, shift_arg, *weights)


PAIR = 2 * RW_HEAD


def _dot_hilo(a, b, dims=(((1,), (0,)), ((), ()))):
    ah = a.astype(BF16)
    al = (a - ah.astype(F32)).astype(BF16)
    bh = b.astype(BF16)
    bl = (b - bh.astype(F32)).astype(BF16)
    f = lambda x, y: lax.dot_general(x, y, dims, preferred_element_type=F32)
    return f(ah, bh) + f(ah, bl) + f(al, bh)


def _dot_tn_hilo(a, b):
    return _dot_hilo(a, b, (((0,), (0,)), ((), ())))


def _wkv_kernel(has_vres, npair, ln, *refs):
    if has_vres:
        (r_ref, lw_ref, k_ref, v_ref, a_ref, vf_ref, vg_ref, s0_ref,
         kk_ref, ka_ref, rk_ref, lnw_ref, lnb_ref, y_ref, sout_ref, s_ref) = refs
    else:
        (r_ref, lw_ref, k_ref, v_ref, a_ref, s0_ref,
         kk_ref, ka_ref, rk_ref, lnw_ref, lnb_ref, y_ref, sout_ref, s_ref) = refs
    c = pl.program_id(2)
    n = RW_HEAD
    rows = 2 * ln

    @pl.when(c == 0)
    def _():
        zero = jnp.zeros((n, n), F32)
        for p in range(npair):
            s_ref[p] = jnp.concatenate([jnp.concatenate([s0_ref[0, 2 * p], zero], axis=1),
                                        jnp.concatenate([zero, s0_ref[0, 2 * p + 1]], axis=1)], axis=0)

    first = lax.broadcasted_iota(jnp.int32, (ln, PAIR), 1) < n
    ri = lax.broadcasted_iota(jnp.int32, (rows, rows), 0)
    ci = lax.broadcasted_iota(jnp.int32, (rows, rows), 1)
    same = (ri // ln) == (ci // ln)
    incl = same & (ci <= ri)
    strict = same & (ci < ri)
    eye = jnp.where(ri == ci, 1.0, 0.0)
    ti = lax.broadcasted_iota(jnp.int32, (ln, ln), 0)
    tj = lax.broadcasted_iota(jnp.int32, (ln, ln), 1)
    tri = jnp.where(tj <= ti, 1.0, 0.0).astype(BF16)

    def seg_sum(x):
        s1 = jnp.sum(jnp.where(first, x, 0.0), axis=-1, keepdims=True)
        s2 = jnp.sum(jnp.where(first, 0.0, x), axis=-1, keepdims=True)
        return jnp.where(first, s1, s2)

    def by_head(x):
        return jnp.concatenate([jnp.where(first, x, 0.0), jnp.where(first, 0.0, x)], axis=0)

    pairs = range(npair)
    lanes = [slice(p * PAIR, (p + 1) * PAIR) for p in pairs]

    def prepare(p):
        r = r_ref[:, lanes[p]]
        lw = lw_ref[:, lanes[p]]
        k = k_ref[:, lanes[p]]
        v = v_ref[:, lanes[p]]
        a = a_ref[:, lanes[p]]
        if has_vres:
            v = v + (vf_ref[:, lanes[p]] - v) * vg_ref[:, lanes[p]]
        kk = k * kk_ref[:, lanes[p]]
        kk = kk / jnp.maximum(jnp.sqrt(seg_sum(kk * kk)), 1e-12)
        k2 = k * (1.0 + (a - 1.0) * ka_ref[:, lanes[p]])
        bv = kk * a
        hi, mid, lo = _split3(lw)
        cl3 = jnp.dot(tri, jnp.concatenate([hi, mid, lo], axis=1), preferred_element_type=F32)
        cl = cl3[:, :PAIR] + cl3[:, PAIR:2 * PAIR] + cl3[:, 2 * PAIR:]
        cl_end = cl[ln - 1:ln, :]
        g_inv = jnp.exp(-cl)
        g_end = jnp.exp(cl_end - cl)
        rt = r * jnp.exp(cl)
        at = -kk * jnp.exp(cl - lw)
        lhs = jnp.concatenate([by_head(at), by_head(rt)], axis=0)
        bh = bv * g_inv
        kh = k2 * g_inv
        return dict(r=r, v=v, k2=k2, lhs=lhs, rhs=jnp.concatenate([bh, bh, kh, kh], axis=0),
                    decay=jnp.exp(cl_end), v_h=by_head(v),
                    ws=jnp.concatenate([by_head(bv * g_end), by_head(k2 * g_end)], axis=0))

    st = [prepare(p) for p in pairs]
    aa = [_dot_nt(st[p]["lhs"], st[p]["rhs"]) for p in pairs]
    a_ak = [jnp.where(strict, aa[p][:rows, rows:], 0.0) for p in pairs]
    a_r = [jnp.concatenate([jnp.where(incl, aa[p][rows:, :rows], 0.0),
                            jnp.where(incl, aa[p][rows:, rows:], 0.0)], axis=1) for p in pairs]
    akv = [_dot_hilo(a_ak[p], st[p]["v_h"]) for p in pairs]
    def same_block(size):
        return (ri // size) == (ci // size)

    below = ci < ri
    tinv = [eye + jnp.where(below & same_block(2), aa[p][:rows, :rows], 0.0) for p in pairs]
    for level in range(1, int(math.log2(ln))):
        size = 2 ** level
        coupling = below & same_block(2 * size) & jnp.logical_not(same_block(size))
        n_off = [jnp.where(coupling, aa[p][:rows, :rows], 0.0) for p in pairs]
        right = [_dot(n_off[p], tinv[p]) for p in pairs]
        tinv = [tinv[p] + _dot(tinv[p], right[p]) for p in pairs]

    s_prev = [s_ref[p] for p in pairs]
    xy = [_dot_nt(st[p]["lhs"], s_prev[p]) for p in pairs]
    u = [_dot_hilo(tinv[p], xy[p][:rows] + akv[p]) for p in pairs]
    zs = [jnp.concatenate([u[p], st[p]["v_h"]], axis=0) for p in pairs]
    for p in pairs:
        s_ref[p] = s_prev[p] * st[p]["decay"] + _dot_tn_hilo(zs[p], st[p]["ws"])
    y_h = [xy[p][rows:] + _dot(a_r[p], zs[p]) for p in pairs]
    for p in pairs:
        y = y_h[p][:ln] + y_h[p][ln:]
        mean = seg_sum(y) * (1.0 / n)
        yc = y - mean
        var = seg_sum(yc * yc) * (1.0 / n)
        y = yc * lax.rsqrt(var + RW_LNX_EPS) * lnw_ref[:, lanes[p]] + lnb_ref[:, lanes[p]]
        y_ref[:, lanes[p]] = y + seg_sum(st[p]["r"] * st[p]["k2"] * rk_ref[:, lanes[p]]) * st[p]["v"]

    @pl.when(c == pl.num_programs(2) - 1)
    def _():
        for p in range(npair):
            sout_ref[0, 2 * p] = s_ref[p, :n, :n]
            sout_ref[0, 2 * p + 1] = s_ref[p, n:, n:]


def wkv_scan(r, lw, k, v, a, s0, p, batch, seq, npair, ln, vres=None):
    nc = seq // ln
    has_vres = vres is not None
    width = npair * PAIR
    tok = pl.BlockSpec((ln, width), lambda b, q, c: (b * nc + c, q))
    par = pl.BlockSpec((1, width), lambda b, q, c: (0, q))
    st = pl.BlockSpec((1, 2 * npair, RW_HEAD, RW_HEAD), lambda b, q, c: (b, q, 0, 0))
    seqs = [r, lw, k, v, a] + (list(vres) if has_vres else [])
    pars = [p[nm].reshape(1, D_MODEL) for nm in ("k_k", "k_a", "r_k", "lnx_w", "lnx_b")]
    return pl.pallas_call(
        functools.partial(_wkv_kernel, has_vres, npair, ln),
        grid=(batch, D_MODEL // width, nc),
        in_specs=[tok] * len(seqs) + [st] + [par] * 5,
        out_specs=[tok, st],
        out_shape=[jax.ShapeDtypeStruct(r.shape, F32), jax.ShapeDtypeStruct(s0.shape, F32)],
        scratch_shapes=[pltpu.VMEM((npair, PAIR, PAIR), F32)],
        compiler_params=_params("parallel", "parallel", "arbitrary"),
        name="wkv_scan",
    )(*seqs, s0, *pars)


def prep_rwkv(w, j, i):
    row = lambda z: z.reshape(1, -1).astype(F32)
    p = {
        "norm_mix": row(w["norm_mix"][i]),
        "mu": w["rwkv_mu"][j],
        "w_rkv": w["rwkv_w_rkv"][j].astype(BF16),
        "w0": row(w["rwkv_w0"][j]), "w1": w["rwkv_w1"][j].astype(BF16), "w2": w["rwkv_w2"][j].astype(BF16),
        "a0": row(w["rwkv_a0"][j]), "a1": w["rwkv_a1"][j].astype(BF16), "a2": w["rwkv_a2"][j].astype(BF16),
        "g1": w["rwkv_g1"][j].astype(BF16), "g2": w["rwkv_g2"][j].astype(BF16),
        "k_k": w["rwkv_k_k"][j], "k_a": w["rwkv_k_a"][j], "r_k": w["rwkv_r_k"][j],
        "lnx_w": w["rwkv_lnx_w"][j], "lnx_b": w["rwkv_lnx_b"][j],
        "w_o": w["rwkv_w_o"][j].astype(BF16),
    }
    if j > 0:
        p["v0"] = row(w["rwkv_v0"][j - 1])
        p["v1"] = w["rwkv_v1"][j - 1].astype(BF16)
        p["v2"] = w["rwkv_v2"][j - 1].astype(BF16)
    return p


def rwkv_layer(h, shift, s0, vfirst, p, batch, seq, tm, npair, ln):
    m, c = h.shape
    outs = rwkv_proj(h, shift, p, seq, tm)
    r, lw, k, v, a, g = outs[:6]
    vres = None if vfirst is None else (vfirst, outs[6])
    y, s_new = wkv_scan(r, lw, k, v, a, s0, p, batch, seq, npair, ln, vres)
    shift_new = rmsnorm_rows(h.reshape(batch, seq, c)[:, -1], p["norm_mix"], batch)
    h = linear_res(y, p["w_o"], h, tm, gate=g)
    return h, s_new, shift_new, (v if vfirst is None else vfirst)


SEG_PAD = 128


def _dot_hilo_rhs(x, ones):
    hi = x.astype(BF16)
    lo = (x - hi.astype(F32)).astype(BF16)
    return jnp.dot(hi, ones, preferred_element_type=F32) + jnp.dot(lo, ones, preferred_element_type=F32)


def _seg_rms_scale(x, seg_ref, segt_ref, invw_ref):
    ssq = _dot(x * x, seg_ref[...])
    return _dot_hilo_rhs(lax.rsqrt(ssq * invw_ref[...] + NORM_EPS), segt_ref[...])


def _swap_halves(x, group):
    half = group // 2
    width = x.shape[-1]
    lane = lax.broadcasted_iota(jnp.int32, x.shape, x.ndim - 1)
    return jnp.where((lane % group) < half,
                     pltpu.roll(x, width - half, axis=x.ndim - 1),
                     pltpu.roll(x, half, axis=x.ndim - 1))


SLOT = 128


def _mla_proj_kernel(with_kv, *refs):
    (h_ref, cos_ref, sin_ref, gmix_ref, win_ref, gq_ref, gkv_ref, gkr_ref, wuq_ref, gqc_ref,
     seg_ref, segt_ref, invw_ref) = refs[:13]
    if with_kv:
        wuk_ref, wuv_ref, gkn_ref, qc_ref, c_ref, kp_ref, kc_ref, v_ref = refs[13:]
    else:
        qc_ref, c_ref, kp_ref = refs[13:]
    x = _rms(h_ref[...], gmix_ref[...])
    hp = _dot(x, win_ref[...])
    q_a = _rms(hp[:, :MLA_Q_LORA], gq_ref[...])
    c = _rms(hp[:, MLA_Q_LORA:MLA_Q_LORA + MLA_KV_LORA], gkv_ref[...])
    c_ref[...] = c

    cos = cos_ref[...]
    sin = sin_ref[...]

    def rope(z, cs, sn):
        return z * cs + _swap_halves(z, MLA_ROPE) * sn

    kp_raw = hp[:, MLA_Q_LORA + MLA_KV_LORA:]
    kp_scale = lax.rsqrt(jnp.sum(kp_raw * kp_raw, axis=-1, keepdims=True) * (1.0 / MLA_ROPE) + NORM_EPS)
    kp = rope(pltpu.roll(kp_raw, MLA_NOPE, axis=1) * gkr_ref[...], cos, sin) * kp_scale
    kp_ref[...] = kp[:, MLA_NOPE:MLA_NOPE + MLA_ROPE]

    q = _dot(q_a, wuq_ref[...])
    cos_h = jnp.tile(cos, (1, MLA_HEADS))
    sin_h = jnp.tile(sin, (1, MLA_HEADS))
    qc = rope(q * gqc_ref[...], cos_h, sin_h) * _seg_rms_scale(q, seg_ref, segt_ref, invw_ref)
    qc_ref[...] = qc.astype(BF16)
    if with_kv:
        kraw = _dot(c, wuk_ref[...])
        kn = kraw * _seg_rms_scale(kraw, seg_ref, segt_ref, invw_ref) * gkn_ref[...]
        kc_ref[...] = (kn + jnp.tile(kp, (1, MLA_HEADS))).astype(BF16)
        v_ref[0] = _dot_nt(wuv_ref[...], c).astype(BF16)


def _slot_seg_matrix():
    lane = jnp.arange(MLA_HEADS * SLOT)
    head, off = lane // SLOT, lane % SLOT
    col = jnp.where(off < MLA_NOPE, head, jnp.where(off < MLA_NOPE + MLA_ROPE, MLA_HEADS + head, SEG_PAD))
    return (col[:, None] == jnp.arange(SEG_PAD)[None, :]).astype(BF16)


def _slot(nope, rope):
    pad = jnp.zeros(nope.shape[:-1] + (SLOT - MLA_NOPE - MLA_ROPE,), nope.dtype)
    return jnp.concatenate([nope, rope, pad], axis=-1)


def _rope_tables(pos):
    half = MLA_ROPE // 2
    inv = ROPE_THETA ** (-jnp.arange(half, dtype=F32) / half)
    ang = pos.astype(F32)[:, None] * inv[None, :]
    cos = jnp.cos(ang)
    sin = jnp.sin(ang)
    ones = jnp.ones((pos.shape[0], MLA_NOPE), F32)
    return (_slot(ones, jnp.concatenate([cos, cos], axis=1)),
            _slot(jnp.zeros_like(ones), jnp.concatenate([-sin, sin], axis=1)))


def mla_proj(h, pos, p, tm, with_kv):
    m, c = h.shape
    seq = pos.shape[0]
    cos, sin = _rope_tables(pos)
    if seq >= tm:
        nrep = seq // tm
    else:
        cos, sin = jnp.tile(cos, (tm // seq, 1)), jnp.tile(sin, (tm // seq, 1))
        nrep = 1
    row = lambda width: pl.BlockSpec((tm, width), lambda i: (i, 0))
    tab = pl.BlockSpec((tm, SLOT), lambda i: (i % nrep, 0))
    full = lambda arr: pl.BlockSpec(arr.shape, lambda i: (0,) * arr.ndim)
    seg = _slot_seg_matrix()
    invw = jnp.concatenate([jnp.full((MLA_HEADS,), 1.0 / MLA_NOPE, F32), jnp.full((MLA_HEADS,), 1.0 / MLA_ROPE, F32),
                            jnp.ones((SEG_PAD - 2 * MLA_HEADS,), F32)]).reshape(1, SEG_PAD)
    weights = [p["norm_mix"], p["w_in"], p["q_norm"], p["kv_norm"], p["kr_norm"], p["w_uq"],
               p["q_gain"] if with_kv else p["q_gain_abs"], seg, seg.T, invw]
    wide = MLA_HEADS * SLOT
    out_shape = [jax.ShapeDtypeStruct((m, wide), BF16),
                 jax.ShapeDtypeStruct((m, MLA_KV_LORA), F32), jax.ShapeDtypeStruct((m, MLA_ROPE), F32)]
    out_specs = [row(wide), row(MLA_KV_LORA), row(MLA_ROPE)]
    if with_kv:
        weights += [p["w_uk"], p["w_uv_t"], p["kn_gain"]]
        out_shape += [jax.ShapeDtypeStruct((m, wide), BF16),
                      jax.ShapeDtypeStruct((m // tm, MLA_HEADS * MLA_V, tm), BF16)]
        out_specs += [row(wide), pl.BlockSpec((1, MLA_HEADS * MLA_V, tm), lambda i: (i, 0, 0))]
    return pl.pallas_call(
        functools.partial(_mla_proj_kernel, with_kv),
        grid=(m // tm,),
        in_specs=[row(c), tab, tab] + [full(w) for w in weights],
        out_specs=out_specs,
        out_shape=out_shape,
        compiler_params=_params("parallel"),
        name="mla_proj",
    )(h, cos, sin, *weights)


def prep_mla(w, j, i):
    row = lambda z: z.reshape(1, -1).astype(F32)
    w_in = w["mla_w_in"][j]
    pad = jnp.zeros((D_MODEL, 128 - MLA_ROPE), F32)
    w_uq = w["mla_w_uq"][j]
    w_uk = w["mla_w_uk"][j]
    zero_n = jnp.zeros((MLA_NOPE,), F32)
    zero_r = jnp.zeros((MLA_ROPE,), F32)
    qn, qr, kn = w["mla_qn_norm"][j], w["mla_qr_norm"][j], w["mla_kn_norm"][j]
    return {
        "norm_mix": row(w["norm_mix"][i]),
        "w_in": jnp.concatenate([w_in, pad], axis=1).astype(BF16),
        "q_norm": row(w["mla_q_norm"][j]), "kv_norm": row(w["mla_kv_norm"][j]),
        "kr_norm": row(_slot(zero_n, w["mla_kr_norm"][j])),
        "w_uq": _slot(w_uq[:, :, :MLA_NOPE], w_uq[:, :, MLA_NOPE:]).reshape(MLA_Q_LORA, -1).astype(BF16),
        "q_gain": row(jnp.tile(_slot(qn, qr) * MLA_SCALE, MLA_HEADS)),
        "q_gain_abs": row(jnp.tile(_slot(qn * kn, qr) * MLA_SCALE, MLA_HEADS)),
        "kn_gain": row(jnp.tile(_slot(kn, zero_r), MLA_HEADS)),
        "w_uk": _slot(w_uk, jnp.zeros(w_uk.shape[:2] + (MLA_ROPE,), F32)).reshape(MLA_KV_LORA, -1).astype(BF16),
        "w_uv": w["mla_w_uv"][j].reshape(MLA_KV_LORA, -1).astype(BF16),
        "w_uv_t": w["mla_w_uv"][j].reshape(MLA_KV_LORA, -1).T.astype(BF16),
        "w_uk_t": w["mla_w_uk"][j].reshape(MLA_KV_LORA, -1).T.astype(BF16),
        "w_uk_heads": jnp.transpose(w["mla_w_uk"][j], (1, 2, 0)).astype(BF16),
        "w_o": w["mla_w_o"][j].astype(BF16),
    }


ATTN_HEADS_PER_STEP = 8
NEG_BIG = -1e30


def _flash_kernel(tq, qc_ref, kc_ref, vt_ref, o_ref, m_ref, l_ref, acc_ref):
    qi = pl.program_id(2)
    g = ATTN_HEADS_PER_STEP
    heads = range(g)
    key_i = lax.broadcasted_iota(jnp.int32, (tq, tq), 0)
    qry_i = lax.broadcasted_iota(jnp.int32, (tq, tq), 1)
    m_ref[...] = jnp.full_like(m_ref, NEG_BIG)
    l_ref[...] = jnp.zeros_like(l_ref)
    acc_ref[...] = jnp.zeros_like(acc_ref)

    def block(j, masked):
        keys = pl.ds(pl.multiple_of(j * tq, tq), tq)
        s = [_dot_nt(kc_ref[0, keys, hh * SLOT:(hh + 1) * SLOT], qc_ref[0, :, hh * SLOT:(hh + 1) * SLOT])
             for hh in heads]
        if masked:
            s = [jnp.where(key_i <= qry_i, s[hh], NEG_BIG) for hh in heads]
        m_prev = [m_ref[hh] for hh in heads]
        m_new = [jnp.maximum(m_prev[hh], jnp.max(s[hh], axis=0, keepdims=True)) for hh in heads]
        alpha = [jnp.exp(m_prev[hh] - m_new[hh]) for hh in heads]
        pr = [jnp.exp(s[hh] - m_new[hh]) for hh in heads]
        pv = [jnp.dot(vt_ref[j, hh * MLA_V:(hh + 1) * MLA_V, :], pr[hh].astype(BF16), preferred_element_type=F32)
              for hh in heads]
        for hh in heads:
            l_ref[hh] = alpha[hh] * l_ref[hh] + jnp.sum(pr[hh], axis=0, keepdims=True)
            acc_ref[hh] = alpha[hh] * acc_ref[hh] + pv[hh]
            m_ref[hh] = m_new[hh]

    def body(j, carry):
        block(j, False)
        return carry

    lax.fori_loop(0, qi, body, 0)
    block(qi, True)
    o_t = jnp.concatenate([acc_ref[hh] / l_ref[hh] for hh in heads], axis=0)
    o_ref[0] = o_t.T


def flash_prompt(qc, kc, vt, batch, seq, tq):
    g = ATTN_HEADS_PER_STEP
    nq = seq // tq
    r3 = lambda z: z.reshape(batch, seq, -1)
    return pl.pallas_call(
        functools.partial(_flash_kernel, tq),
        grid=(batch, MLA_HEADS // g, nq),
        in_specs=[pl.BlockSpec((1, tq, g * SLOT), lambda b, hq, i: (b, i, hq)),
                  pl.BlockSpec((1, seq, g * SLOT), lambda b, hq, i: (b, 0, hq)),
                  pl.BlockSpec((nq, g * MLA_V, tq), lambda b, hq, i: (b, hq, 0))],
        out_specs=pl.BlockSpec((1, tq, g * MLA_V), lambda b, hq, i: (b, i, hq)),
        out_shape=jax.ShapeDtypeStruct((batch, seq, MLA_HEADS * MLA_V), F32),
        scratch_shapes=[pltpu.VMEM((g, 1, tq), F32), pltpu.VMEM((g, 1, tq), F32),
                        pltpu.VMEM((g, MLA_V, tq), F32)],
        compiler_params=_params("parallel", "parallel", "parallel"),
        name="flash_prompt",
    )(r3(qc), r3(kc), vt).reshape(batch * seq, -1)


def _bmm_kernel(a_ref, b_ref, o_ref):
    o_ref[0] = _dot(a_ref[0], b_ref[0]).astype(o_ref.dtype)


def bmm(a, b, out_dtype):
    g, m, k = a.shape
    n = b.shape[2]
    return pl.pallas_call(
        _bmm_kernel,
        grid=(g,),
        in_specs=[pl.BlockSpec((1, m, k), lambda i: (i, 0, 0)), pl.BlockSpec((1, k, n), lambda i: (i, 0, 0))],
        out_specs=pl.BlockSpec((1, m, n), lambda i: (i, 0, 0)),
        out_shape=jax.ShapeDtypeStruct((g, m, n), out_dtype),
        compiler_params=_params("parallel"),
        name="bmm",
    )(a, b)


def _paged_attn_kernel(pp, seq, *refs):
    pt_ref = refs[0]
    qa_ref, qp_ref, cn_ref, kpn_ref, wukt_ref, wuv_ref = refs[1:7]
    c_refs = refs[7:7 + pp]
    kp_refs = refs[7 + pp:7 + 2 * pp]
    o_ref, m_ref, l_ref, acc_ref, lhs_ref = refs[7 + 2 * pp:]
    del pt_ref
    s_idx = pl.program_id(1)
    nq = seq * MLA_HEADS
    nup = MLA_HEADS * MLA_NOPE

    @pl.when(s_idx == 0)
    def _():
        m_ref[...] = jnp.full_like(m_ref, NEG_BIG)
        l_ref[...] = jnp.zeros_like(l_ref)
        acc_ref[...] = jnp.zeros_like(acc_ref)
        lhs_ref[:nup, :] = wukt_ref[...]
        lhs_ref[nup:, :] = qa_ref[0]

    qp = qp_ref[0]

    def attend(c_blks, kp_t_blks, mask):
        subs = range(len(c_blks))
        nk = c_blks[0].shape[0]
        c_bf = [c_blks[i].astype(BF16) for i in subs]
        both = [_dot_nt(lhs_ref[...], c_bf[i]) for i in subs]
        ssq = [jnp.sum(jnp.square(both[i][:nup]).reshape(MLA_HEADS, MLA_NOPE, nk), axis=1) for i in subs]
        rs = [lax.rsqrt(ssq[i] * (1.0 / MLA_NOPE) + NORM_EPS) for i in subs]
        s = [both[i][nup:] * jnp.concatenate([rs[i]] * seq, axis=0) + _dot(qp, kp_t_blks[i]) for i in subs]
        if mask is not None:
            s = [jnp.where(mask, s[i], NEG_BIG) for i in subs]
        m_prev = m_ref[...]
        m_new = m_prev
        for i in subs:
            m_new = jnp.maximum(m_new, jnp.max(s[i], axis=-1, keepdims=True))
        alpha = jnp.exp(m_prev - m_new)
        pr = [jnp.exp(s[i] - m_new) for i in subs]
        l_new = alpha * l_ref[...]
        acc = alpha * acc_ref[...]
        for i in subs:
            l_new = l_new + jnp.sum(pr[i], axis=-1, keepdims=True)
            acc = acc + _dot(pr[i], c_bf[i])
        l_ref[...] = l_new
        acc_ref[...] = acc
        m_ref[...] = m_new

    pages_per_sub = 2
    groups = [range(i, i + pages_per_sub) for i in range(0, pp, pages_per_sub)]
    attend([jnp.concatenate([c_refs[i][0] for i in grp], axis=0) for grp in groups],
           [jnp.concatenate([kp_refs[i][0] for i in grp], axis=1) for grp in groups], None)

    @pl.when(s_idx == pl.num_programs(1) - 1)
    def _():
        qtok = lax.broadcasted_iota(jnp.int32, (nq, seq), 0) // MLA_HEADS
        ktok = lax.broadcasted_iota(jnp.int32, (nq, seq), 1)
        attend([cn_ref[0]], [kpn_ref[0]], ktok <= qtok)
        o_lat = acc_ref[...] / l_ref[...]
        full = _dot(o_lat, wuv_ref[...])
        rhead = lax.broadcasted_iota(jnp.int32, full.shape, 0) % MLA_HEADS
        lhead = lax.broadcasted_iota(jnp.int32, full.shape, 1) // MLA_V
        full = jnp.where(rhead == lhead, full, 0.0)
        o_ref[0] = jnp.sum(full.reshape(seq, MLA_HEADS, MLA_HEADS * MLA_V), axis=1)


def paged_attn(q_abs, qp, c_new, kp_new, pool_c, pool_kp_t, page_table, p, batch, seq, pp):
    n_pages = page_table.shape[1]
    ns = n_pages // pp
    nq = seq * MLA_HEADS
    per_b = lambda shp: pl.BlockSpec((1,) + shp, lambda b, s, pt: (b, 0, 0))
    full = lambda arr: pl.BlockSpec(arr.shape, lambda b, s, pt: (0,) * arr.ndim)

    def page(shape, i):
        return pl.BlockSpec((1,) + shape, lambda b, s, pt: (pt[b * n_pages + s * pp + i], 0, 0))

    grid_spec = pltpu.PrefetchScalarGridSpec(
        num_scalar_prefetch=1,
        grid=(batch, ns),
        in_specs=[per_b((nq, MLA_KV_LORA)), per_b((nq, MLA_ROPE)), per_b((seq, MLA_KV_LORA)),
                  per_b((MLA_ROPE, seq)), full(p["w_uk_t"]), full(p["w_uv"])]
                 + [page((PAGE_SIZE, MLA_KV_LORA), i) for i in range(pp)]
                 + [page((MLA_ROPE, PAGE_SIZE), i) for i in range(pp)],
        out_specs=per_b((seq, MLA_HEADS * MLA_V)),
        scratch_shapes=[pltpu.VMEM((nq, 1), F32), pltpu.VMEM((nq, 1), F32), pltpu.VMEM((nq, MLA_KV_LORA), F32),
                        pltpu.VMEM((MLA_HEADS * MLA_NOPE + nq, MLA_KV_LORA), BF16)],
    )
    return pl.pallas_call(
        functools.partial(_paged_attn_kernel, pp, seq),
        grid_spec=grid_spec,
        out_shape=jax.ShapeDtypeStruct((batch, seq, MLA_HEADS * MLA_V), F32),
        compiler_params=_params("parallel", "arbitrary"),
        name="paged_attn",
    )(page_table.reshape(-1), q_abs, qp, c_new.reshape(batch, seq, -1),
      jnp.swapaxes(kp_new.reshape(batch, seq, -1), 1, 2),
      p["w_uk_t"], p["w_uv"], *([pool_c] * pp), *([pool_kp_t] * pp))


def mla_layer_prompt(h, p, batch, seq, tm, tq):
    assert tm == tq, "the value tiles written by mla_proj are the key blocks of flash_prompt"
    qc, c, kp, kc, vt = mla_proj(h, jnp.arange(seq), p, tm, True)
    o = flash_prompt(qc, kc, vt, batch, seq, tq)
    return linear_res(o, p["w_o"], h, tm), c, kp


def mla_layer_sample(h, pool_c, pool_kp, page_table, p, batch, seq, past_len, tm, pp):
    m = batch * seq
    qc, c, kp = mla_proj(h, past_len + jnp.arange(seq), p, tm, False)
    qc = qc.reshape(m, MLA_HEADS, SLOT)
    q_heads = jnp.swapaxes(qc[:, :, :MLA_NOPE], 0, 1)
    q_abs = bmm(q_heads, p["w_uk_heads"], BF16)
    q_abs = jnp.swapaxes(q_abs, 0, 1).reshape(batch, seq * MLA_HEADS, MLA_KV_LORA)
    qp = qc[:, :, MLA_NOPE:MLA_NOPE + MLA_ROPE].reshape(batch, seq * MLA_HEADS, MLA_ROPE)
    o = paged_attn(q_abs, qp, c, kp, pool_c, jnp.swapaxes(pool_kp, 1, 2), page_table, p, batch, seq, pp)
    return linear_res(o.reshape(m, -1), p["w_o"], h, tm), c, kp


DT_PAD = 128
CONV_HALO = 8
MB_GN = MB_GROUPS * MB_STATE
MB_GROUP_INNER = MB_INNER // MB_GROUPS
MB_HEADS_PER_GROUP = MB_HEADS // MB_GROUPS


def _ssd_kernel(ck, xbc_ref, prev_ref, cs_ref, z_ref, dtr_ref, h0_ref, cw_ref, cb_ref, dtb_ref, alog_ref,
                dskip_ref, nw_ref, expand_ref, y_ref, hout_ref, h_ref):
    c = pl.program_id(1)

    @pl.when(c == 0)
    def _():
        h_ref[...] = h0_ref[0]

    halo = jnp.where(c == 0, cs_ref[0], prev_ref[0])
    xext = jnp.concatenate([halo, xbc_ref[0]], axis=0)
    conv = cb_ref[...] + xext[CONV_HALO:, :] * cw_ref[MB_CONV - 1:MB_CONV, :]
    for j in range(MB_CONV - 1):
        shifted = pltpu.roll(xext, MB_CONV - 1 - j, axis=0)[CONV_HALO:, :]
        conv = conv + shifted * cw_ref[j:j + 1, :]
    xbc = _silu(conv)
    xs = xbc[:, :MB_INNER]
    bm = xbc[:, MB_INNER:MB_INNER + MB_GN]
    cm = xbc[:, MB_INNER + MB_GN:]

    dt = _softplus(dtr_ref[0] + dtb_ref[...])
    da = dt * (-jnp.exp(alog_ref[...]))
    ri = lax.broadcasted_iota(jnp.int32, (ck, ck), 0)
    ci = lax.broadcasted_iota(jnp.int32, (ck, ck), 1)
    causal = ci <= ri
    tri = jnp.where(causal, 1.0, 0.0).astype(BF16)
    hi, mid, lo = _split3(da)
    acum = (jnp.dot(tri, hi, preferred_element_type=F32) + jnp.dot(tri, mid, preferred_element_type=F32)
            + jnp.dot(tri, lo, preferred_element_type=F32))
    tn = (((0,), (0,)), ((), ()))
    tri_t = jnp.where(ri <= ci, 1.0, 0.0).astype(BF16)
    acum_t = (lax.dot_general(hi, tri_t, tn, preferred_element_type=F32)
              + lax.dot_general(mid, tri_t, tn, preferred_element_type=F32)
              + lax.dot_general(lo, tri_t, tn, preferred_element_type=F32))
    e_last = jnp.exp(acum[ck - 1:ck, :])

    spread = expand_ref[...]
    dt_hi = dt.astype(BF16)
    dt_lo = (dt - dt_hi.astype(F32)).astype(BF16)
    dt_x = jnp.dot(dt_hi, spread, preferred_element_type=F32) + jnp.dot(dt_lo, spread, preferred_element_type=F32)
    ah, am, al = _split3(acum)
    acum_x = (jnp.dot(ah, spread, preferred_element_type=F32) + jnp.dot(am, spread, preferred_element_type=F32)
              + jnp.dot(al, spread, preferred_element_type=F32))
    xdt = xs * dt_x
    xdt_end = xdt * jnp.exp(acum_x[ck - 1:ck, :] - acum_x)
    e_cum_x = jnp.exp(acum_x)

    pair_w = 2 * MB_HEAD
    first = lax.broadcasted_iota(jnp.int32, (ck, pair_w), 1) < MB_HEAD
    upper = lax.broadcasted_iota(jnp.int32, (pair_w, MB_STATE), 0) < MB_HEAD
    pairs_per_group = MB_HEADS_PER_GROUP // 2
    ys = []
    for g in range(MB_GROUPS):
        b_g = bm[:, g * MB_STATE:(g + 1) * MB_STATE]
        c_g = cm[:, g * MB_STATE:(g + 1) * MB_STATE]
        cb = _dot_nt(c_g, b_g)
        cols = slice(g * MB_GROUP_INNER, (g + 1) * MB_GROUP_INNER)
        y_state = _dot_nt(c_g, h_ref[cols, :]) * e_cum_x[:, cols]
        lmat = []
        for hh in range(MB_HEADS_PER_GROUP):
            hd = g * MB_HEADS_PER_GROUP + hh
            seg = acum[:, hd:hd + 1] - acum_t[hd:hd + 1, :]
            lmat.append(cb * jnp.where(causal, jnp.exp(jnp.where(causal, seg, 0.0)), 0.0))
        for pr in range(pairs_per_group):
            lanes = slice(g * MB_GROUP_INNER + pr * pair_w, g * MB_GROUP_INNER + (pr + 1) * pair_w)
            x_p = xdt[:, lanes]
            y_p = jnp.where(first, _dot(lmat[2 * pr], x_p), _dot(lmat[2 * pr + 1], x_p))
            ys.append(y_p + y_state[:, pr * pair_w:(pr + 1) * pair_w])
            hd = g * MB_HEADS_PER_GROUP + 2 * pr
            decay = jnp.where(upper, e_last[:, hd:hd + 1], e_last[:, hd + 1:hd + 2])
            h_ref[lanes, :] = h_ref[lanes, :] * decay + _dot_tn(xdt_end[:, lanes], b_g)

    y = jnp.concatenate(ys, axis=1) + dskip_ref[...] * xs
    yz = y * _silu(z_ref[0])
    outs = []
    for g in range(MB_GROUPS):
        yg = yz[:, g * MB_GROUP_INNER:(g + 1) * MB_GROUP_INNER]
        outs.append(yg * lax.rsqrt(jnp.mean(yg * yg, axis=-1, keepdims=True) + NORM_EPS))
    y_ref[0] = (jnp.concatenate(outs, axis=1) * nw_ref[...]).astype(y_ref.dtype)

    @pl.when(c == pl.num_programs(1) - 1)
    def _():
        hout_ref[0] = h_ref[...]


def ssd_scan(xbc, z, dt_raw, conv_state, h0, p, batch, seq, ck):
    assert CONV_HALO % 8 == 0 and ck % CONV_HALO == 0
    nc = seq // ck
    halo_blocks = ck // CONV_HALO
    cs8 = jnp.concatenate([jnp.zeros((batch, CONV_HALO - (MB_CONV - 1), MB_CONV_DIM), F32), conv_state], axis=1)
    chunk = lambda width: pl.BlockSpec((1, ck, width), lambda b, c: (b, c, 0))
    full = lambda arr: pl.BlockSpec(arr.shape, lambda b, c: (0,) * arr.ndim)
    expand = (jnp.arange(DT_PAD)[:, None] == jnp.arange(MB_INNER)[None, :] // MB_HEAD).astype(BF16)
    weights = [p["conv_w"], p["conv_b"], p["dt_bias"], p["a_log"], p["d_skip"], p["norm_w"], expand]
    state = pl.BlockSpec((1, MB_INNER, MB_STATE), lambda b, c: (b, 0, 0))
    return pl.pallas_call(
        functools.partial(_ssd_kernel, ck),
        grid=(batch, nc),
        in_specs=[chunk(MB_CONV_DIM),
                  pl.BlockSpec((1, CONV_HALO, MB_CONV_DIM),
                               lambda b, c: (b, jnp.maximum(c * halo_blocks - 1, 0), 0)),
                  pl.BlockSpec((1, CONV_HALO, MB_CONV_DIM), lambda b, c: (b, 0, 0)),
                  chunk(MB_INNER), chunk(DT_PAD), state] + [full(w) for w in weights],
        out_specs=[chunk(MB_INNER), state],
        out_shape=[jax.ShapeDtypeStruct((batch, seq, MB_INNER), BF16),
                   jax.ShapeDtypeStruct((batch, MB_INNER, MB_STATE), F32)],
        scratch_shapes=[pltpu.VMEM((MB_INNER, MB_STATE), F32)],
        compiler_params=_params("parallel", "arbitrary"),
        name="ssd_scan",
    )(xbc, xbc, cs8, z, dt_raw, h0, *weights)


def prep_mamba(w, j, i):
    row = lambda z: z.reshape(1, -1).astype(F32)
    w_in = w["mamba_w_in"][j]
    padv = lambda z: jnp.concatenate([z.astype(F32), jnp.zeros((DT_PAD - MB_HEADS,), F32)])
    return {
        "norm_mix": w["norm_mix"][i],
        "w_z": w_in[:, :MB_INNER].astype(BF16),
        "w_xbc": w_in[:, MB_INNER:MB_INNER + MB_CONV_DIM].astype(BF16),
        "w_dt": jnp.concatenate([w_in[:, MB_INNER + MB_CONV_DIM:], jnp.zeros((D_MODEL, DT_PAD - MB_HEADS), F32)],
                                axis=1).astype(BF16),
        "conv_w": w["mamba_conv_w"][j], "conv_b": row(w["mamba_conv_b"][j]),
        "dt_bias": row(padv(w["mamba_dt_bias"][j])), "a_log": row(padv(w["mamba_a_log"][j])),
        "d_skip": row(jnp.repeat(w["mamba_d"][j].astype(F32), MB_HEAD)),
        "norm_w": row(w["mamba_norm"][j]),
        "w_o": w["mamba_w_o"][j].astype(BF16),
    }


def mamba_layer(h, conv_state, h0, p, batch, seq, tm, ck):
    m = batch * seq
    assert seq >= MB_CONV - 1
    tm_in = min(TM_FFN, m)
    z = norm_linear(h, p["norm_mix"], p["w_z"], tm_in, 1024)
    xbc = norm_linear(h, p["norm_mix"], p["w_xbc"], tm_in, 1024)
    dt_raw = norm_linear(h, p["norm_mix"], p["w_dt"], tm_in, DT_PAD)
    xbc3 = xbc.reshape(batch, seq, MB_CONV_DIM)
    y, h_new = ssd_scan(xbc3, z.reshape(batch, seq, MB_INNER), dt_raw.reshape(batch, seq, DT_PAD),
                        conv_state, h0.reshape(batch, MB_INNER, MB_STATE), p, batch, seq, ck)
    h = linear_res(y.reshape(m, MB_INNER), p["w_o"], h, tm)
    return h, xbc3[:, seq - (MB_CONV - 1):], h_new.reshape(batch, MB_HEADS, MB_HEAD, MB_STATE)


N_MIXERS = 3
TM_PROJ = 256
TM_ROWS = 512
TM_FFN = 1024
TH_FFN = 1024
WKV_CHUNK = 64
WKV_PAIRS_PER_STEP = 8
ATTN_TQ = 256
PAGES_PER_STEP = 8


def kernel(x_prompt, x_sample, cache_mla_ckv, cache_mla_kpe, state_rwkv_wkv, state_rwkv_shift, state_ssm, state_conv, page_table, norm_mix, norm_ffn, ffn_w1, ffn_w2, rwkv_mu, rwkv_w_rkv, rwkv_w0, rwkv_w1, rwkv_w2, rwkv_a0, rwkv_a1, rwkv_a2, rwkv_v0, rwkv_v1, rwkv_v2, rwkv_g1, rwkv_g2, rwkv_k_k, rwkv_k_a, rwkv_r_k, rwkv_lnx_w, rwkv_lnx_b, rwkv_w_o, mla_w_in, mla_q_norm, mla_kv_norm, mla_w_uq, mla_w_uk, mla_w_uv, mla_qn_norm, mla_qr_norm, mla_kn_norm, mla_kr_norm, mla_w_o, mamba_w_in, mamba_conv_w, mamba_conv_b, mamba_dt_bias, mamba_a_log, mamba_d, mamba_norm, mamba_w_o):
    w = dict(locals())
    bp, tp, c = x_prompt.shape
    bs, ts, _ = x_sample.shape
    depth = norm_mix.shape[0]
    past_len = page_table.shape[1] * PAGE_SIZE
    hp = x_prompt.reshape(bp * tp, c)
    hs = x_sample.reshape(bs * ts, c)
    vf_p = vf_s = None
    out = {k: [] for k in ("ckv_p", "kpe_p", "ckv_s", "kpe_s", "wkv_p", "sh_p", "wkv_s", "sh_s",
                           "ssm_p", "conv_p", "ssm_s", "conv_s")}
    for i in range(depth):
        kind, j = i % N_MIXERS, i // N_MIXERS
        if kind == 0:
            p = prep_rwkv(w, j, i)
            hp, s_p, l_p, vf_p = rwkv_layer(hp, jnp.zeros((bp, c), F32), jnp.zeros((bp, RW_HEADS, RW_HEAD, RW_HEAD), F32),
                                            vf_p, p, bp, tp, TM_ROWS, WKV_PAIRS_PER_STEP, WKV_CHUNK)
            hs, s_s, l_s, vf_s = rwkv_layer(hs, state_rwkv_shift[j], state_rwkv_wkv[j], vf_s, p, bs, ts,
                                            TM_ROWS, WKV_PAIRS_PER_STEP, ts)
            out["wkv_p"].append(s_p); out["sh_p"].append(l_p); out["wkv_s"].append(s_s); out["sh_s"].append(l_s)
        elif kind == 1:
            p = prep_mla(w, j, i)
            hp, c_p, k_p = mla_layer_prompt(hp, p, bp, tp, TM_PROJ, ATTN_TQ)
            hs, c_s, k_s = mla_layer_sample(hs, cache_mla_ckv[j], cache_mla_kpe[j], page_table, p, bs, ts,
                                            past_len, TM_PROJ, PAGES_PER_STEP)
            out["ckv_p"].append(c_p.reshape(bp, tp, -1)); out["kpe_p"].append(k_p.reshape(bp, tp, -1))
            out["ckv_s"].append(c_s.reshape(bs, ts, -1)); out["kpe_s"].append(k_s.reshape(bs, ts, -1))
        else:
            p = prep_mamba(w, j, i)
            hp, cv_p, h_p = mamba_layer(hp, jnp.zeros((bp, MB_CONV - 1, MB_CONV_DIM), F32),
                                        jnp.zeros((bp, MB_HEADS, MB_HEAD, MB_STATE), F32), p, bp, tp,
                                        TM_ROWS, MB_CHUNK)
            hs, cv_s, h_s = mamba_layer(hs, state_conv[j], state_ssm[j], p, bs, ts, TM_ROWS, math.gcd(ts, MB_CHUNK))
            out["ssm_p"].append(h_p); out["conv_p"].append(cv_p); out["ssm_s"].append(h_s); out["conv_s"].append(cv_s)
        w1, w2 = ffn_w1[i].astype(BF16), ffn_w2[i].astype(BF16)
        hp = ffn_res(hp, norm_ffn[i], w1, w2, TM_FFN, TH_FFN)
        hs = ffn_res(hs, norm_ffn[i], w1, w2, TM_FFN, TH_FFN)
    stack = lambda k: jnp.stack(out[k])
    return (hp.reshape(bp, tp, c), hs.reshape(bs, ts, c),
            stack("ckv_p"), stack("kpe_p"), stack("ckv_s"), stack("kpe_s"),
            stack("wkv_p"), stack("sh_p"), stack("wkv_s"), stack("sh_s"),
            stack("ssm_p"), stack("conv_p"), stack("ssm_s"), stack("conv_s"))
```

```python
import functools
import math

import jax
import jax.numpy as jnp
from jax import lax
from jax.experimental import pallas as pl
from jax.experimental.pallas import tpu as pltpu

F32 = jnp.float32
BF16 = jnp.bfloat16

D_MODEL = 1024
NORM_EPS = 1e-6

RW_HEAD = 64
RW_HEADS = D_MODEL // RW_HEAD
RW_LNX_EPS = 64e-5

MLA_HEADS = 16
MLA_Q_LORA = 512
MLA_KV_LORA = 256
MLA_NOPE = 64
MLA_ROPE = 32
MLA_V = 64
MLA_SCALE = 1.0 / math.sqrt(MLA_NOPE + MLA_ROPE)
ROPE_THETA = 10000.0
PAGE_SIZE = 128

MB_INNER = 2 * D_MODEL
MB_HEAD = 64
MB_HEADS = MB_INNER // MB_HEAD
MB_GROUPS = 4
MB_STATE = 128
MB_CONV = 4
MB_CONV_DIM = MB_INNER + 2 * MB_GROUPS * MB_STATE
MB_CHUNK = 128

FFN_HIDDEN = 4 * D_MODEL

VMEM_LIMIT_BYTES = 56 * 2**20


def _params(*sem):
    return pltpu.CompilerParams(dimension_semantics=sem, vmem_limit_bytes=VMEM_LIMIT_BYTES)


def _dot(a, b):
    return jnp.dot(a.astype(BF16), b.astype(BF16), preferred_element_type=F32)


def _dot_nt(a, b):
    return lax.dot_general(a.astype(BF16), b.astype(BF16), (((1,), (1,)), ((), ())),
                           preferred_element_type=F32)


def _dot_tn(a, b):
    return lax.dot_general(a.astype(BF16), b.astype(BF16), (((0,), (0,)), ((), ())),
                           preferred_element_type=F32)


def _split3(a):
    hi = a.astype(BF16)
    r1 = a - hi.astype(F32)
    mid = r1.astype(BF16)
    lo = (r1 - mid.astype(F32)).astype(BF16)
    return hi, mid, lo


def _dot_exact_lhs(ones, x):
    hi, mid, lo = _split3(x)
    o = ones.astype(BF16)
    return (jnp.dot(o, hi, preferred_element_type=F32)
            + jnp.dot(o, mid, preferred_element_type=F32)
            + jnp.dot(o, lo, preferred_element_type=F32))


def _dot_exact_rhs(x, ones):
    hi, mid, lo = _split3(x)
    o = ones.astype(BF16)
    return (jnp.dot(hi, o, preferred_element_type=F32)
            + jnp.dot(mid, o, preferred_element_type=F32)
            + jnp.dot(lo, o, preferred_element_type=F32))


def _rms(x, g):
    return x * lax.rsqrt(jnp.mean(x * x, axis=-1, keepdims=True) + NORM_EPS) * g


def _softplus(z):
    return jnp.maximum(z, 0.0) + jnp.log(1.0 + jnp.exp(-jnp.abs(z)))


def _sigmoid(z):
    return 0.5 * jnp.tanh(0.5 * z) + 0.5


def _silu(z):
    return z * _sigmoid(z)


def _rmsnorm_kernel(x_ref, g_ref, o_ref):
    o_ref[...] = _rms(x_ref[...], g_ref[...])


def rmsnorm_rows(x, g, tm):
    m, c = x.shape
    return pl.pallas_call(
        _rmsnorm_kernel,
        grid=(m // tm,),
        in_specs=[pl.BlockSpec((tm, c), lambda i: (i, 0)), pl.BlockSpec((1, c), lambda i: (0, 0))],
        out_specs=pl.BlockSpec((tm, c), lambda i: (i, 0)),
        out_shape=jax.ShapeDtypeStruct((m, c), F32),
        compiler_params=_params("parallel"),
        name="rmsnorm_rows",
    )(x, g.reshape(1, c))


def _linear_res_kernel(x_ref, w_ref, h_ref, o_ref):
    o_ref[...] = h_ref[...] + _dot(x_ref[...], w_ref[...])


def _gated_linear_res_kernel(x_ref, gate_ref, w_ref, h_ref, o_ref):
    o_ref[...] = h_ref[...] + _dot(x_ref[...] * gate_ref[...], w_ref[...])


def linear_res(x, w, h, tm, gate=None):
    m, k = x.shape
    n = w.shape[1]
    row = lambda width: pl.BlockSpec((tm, width), lambda i: (i, 0))
    wspec = pl.BlockSpec((k, n), lambda i: (0, 0))
    if gate is None:
        kern, args, specs = _linear_res_kernel, (x, w, h), [row(k), wspec, row(n)]
    else:
        kern, args, specs = _gated_linear_res_kernel, (x, gate, w, h), [row(k), row(k), wspec, row(n)]
    return pl.pallas_call(
        kern,
        grid=(m // tm,),
        in_specs=specs,
        out_specs=row(n),
        out_shape=jax.ShapeDtypeStruct((m, n), F32),
        compiler_params=_params("parallel"),
        name="linear_res",
    )(*args)


def _norm_linear_kernel(x_ref, g_ref, w_ref, o_ref, xn_ref):
    @pl.when(pl.program_id(1) == 0)
    def _():
        xn_ref[...] = _rms(x_ref[...], g_ref[...]).astype(BF16)

    o_ref[...] = jnp.dot(xn_ref[...], w_ref[...], preferred_element_type=F32)


def norm_linear(x, g, w, tm, tn):
    m, k = x.shape
    n = w.shape[1]
    return pl.pallas_call(
        _norm_linear_kernel,
        grid=(m // tm, n // tn),
        in_specs=[pl.BlockSpec((tm, k), lambda i, j: (i, 0)),
                  pl.BlockSpec((1, k), lambda i, j: (0, 0)),
                  pl.BlockSpec((k, tn), lambda i, j: (0, j))],
        out_specs=pl.BlockSpec((tm, tn), lambda i, j: (i, j)),
        out_shape=jax.ShapeDtypeStruct((m, n), F32),
        scratch_shapes=[pltpu.VMEM((tm, k), BF16)],
        compiler_params=_params("parallel", "arbitrary"),
        name="norm_linear",
    )(x, g.reshape(1, k), w)


def _ffn_kernel(h_ref, g_ref, w1_ref, w2_ref, o_ref, xn_ref, acc_ref):
    j = pl.program_id(1)

    @pl.when(j == 0)
    def _():
        xn_ref[...] = _rms(h_ref[...], g_ref[...]).astype(BF16)
        acc_ref[...] = jnp.zeros_like(acc_ref)

    u = jnp.dot(xn_ref[...], w1_ref[...], preferred_element_type=F32)
    u = jnp.square(jnp.maximum(u, 0.0))
    acc_ref[...] += jnp.dot(u.astype(BF16), w2_ref[...], preferred_element_type=F32)

    @pl.when(j == pl.num_programs(1) - 1)
    def _():
        o_ref[...] = h_ref[...] + acc_ref[...]


def ffn_res(h, g, w1, w2, tm, th):
    m, c = h.shape
    hid = w1.shape[1]
    return pl.pallas_call(
        _ffn_kernel,
        grid=(m // tm, hid // th),
        in_specs=[pl.BlockSpec((tm, c), lambda i, j: (i, 0)),
                  pl.BlockSpec((1, c), lambda i, j: (0, 0)),
                  pl.BlockSpec((c, th), lambda i, j: (0, j)),
                  pl.BlockSpec((th, c), lambda i, j: (j, 0))],
        out_specs=pl.BlockSpec((tm, c), lambda i, j: (i, 0)),
        out_shape=jax.ShapeDtypeStruct((m, c), F32),
        scratch_shapes=[pltpu.VMEM((tm, c), BF16), pltpu.VMEM((tm, c), F32)],
        compiler_params=_params("parallel", "arbitrary"),
        name="ffn_res",
    )(h, g.reshape(1, c), w1, w2)


SHIFT_HALO = 8


def _rwkv_proj_kernel(has_vres, seq, tm, *refs):
    if has_vres:
        (h_ref, halo_ref, sh_ref, gmix_ref, mu_ref, wrkv_ref, w0_ref, w1_ref, w2_ref, a0_ref, a1_ref, a2_ref,
         g1_ref, g2_ref, v0_ref, v1_ref, v2_ref,
         r_ref, lw_ref, k_ref, v_ref, a_ref, g_ref, vg_ref) = refs
    else:
        (h_ref, halo_ref, sh_ref, gmix_ref, mu_ref, wrkv_ref, w0_ref, w1_ref, w2_ref, a0_ref, a1_ref, a2_ref,
         g1_ref, g2_ref,
         r_ref, lw_ref, k_ref, v_ref, a_ref, g_ref) = refs
    x = _rms(h_ref[...], gmix_ref[...])
    rolled = pltpu.roll(x, 1, axis=0)
    row = lax.broadcasted_iota(jnp.int32, x.shape, 0)
    if seq >= tm:
        is_start = (pl.program_id(0) % (seq // tm)) == 0
        tail = _rms(halo_ref[...], gmix_ref[...])[SHIFT_HALO - 1:SHIFT_HALO, :]
        xprev = jnp.where(row == 0, jnp.where(is_start, sh_ref[0], tail), rolled)
    else:
        n_seq = tm // seq
        starts = jnp.broadcast_to(sh_ref[...][:, None, :], (n_seq, seq, x.shape[1])).reshape(x.shape)
        xprev = jnp.where(row % seq == 0, starts, rolled)
    dx = xprev - x
    xm = [(x + dx * mu_ref[p:p + 1, :]).astype(BF16) for p in range(6)]
    r_ref[...] = jnp.dot(xm[0], wrkv_ref[0], preferred_element_type=F32)
    k_ref[...] = jnp.dot(xm[1], wrkv_ref[1], preferred_element_type=F32)
    v_ref[...] = jnp.dot(xm[2], wrkv_ref[2], preferred_element_type=F32)
    wpre = w0_ref[...] + _dot(jnp.tanh(jnp.dot(xm[3], w1_ref[...], preferred_element_type=F32)), w2_ref[...])
    w_log = -_softplus(-wpre) - 0.5
    lw_ref[...] = -jnp.exp(w_log)
    a_ref[...] = _sigmoid(a0_ref[...] + _dot(jnp.dot(xm[4], a1_ref[...], preferred_element_type=F32), a2_ref[...]))
    g_ref[...] = _dot(_sigmoid(jnp.dot(xm[5], g1_ref[...], preferred_element_type=F32)), g2_ref[...])
    if has_vres:
        vg_ref[...] = _sigmoid(v0_ref[...] + _dot(jnp.dot(xm[2], v1_ref[...], preferred_element_type=F32),
                                                  v2_ref[...]))


def rwkv_proj(h, shift, p, seq, tm):
    m, c = h.shape
    assert tm % SHIFT_HALO == 0 and (seq % tm == 0 or tm % seq == 0)
    has_vres = "v1" in p
    row = pl.BlockSpec((tm, c), lambda i: (i, 0))
    halo = pl.BlockSpec((SHIFT_HALO, c), lambda i: (jnp.maximum(i * (tm // SHIFT_HALO) - 1, 0), 0))
    if seq >= tm:
        shift_arg = shift.reshape(-1, 1, c)
        shift_spec = pl.BlockSpec((1, 1, c), lambda i: (i // (seq // tm), 0, 0))
    else:
        shift_arg = shift
        shift_spec = pl.BlockSpec((tm // seq, c), lambda i: (i, 0))
    full = lambda arr: pl.BlockSpec(arr.shape, lambda i: (0,) * arr.ndim, pipeline_mode=pl.Buffered(1))
    names = ["norm_mix", "mu", "w_rkv", "w0", "w1", "w2", "a0", "a1", "a2", "g1", "g2"]
    if has_vres:
        names += ["v0", "v1", "v2"]
    weights = [p[n] for n in names]
    n_out = 7 if has_vres else 6
    return pl.pallas_call(
        functools.partial(_rwkv_proj_kernel, has_vres, seq, tm),
        grid=(m // tm,),
        in_specs=[row, halo, shift_spec] + [full(w) for w in weights],
        out_specs=[row] * n_out,
        out_shape=[jax.ShapeDtypeStruct((m, c), F32)] * n_out,
        compiler_params=_params("parallel"),
        name="rwkv_proj",
    )(h, h, shift_arg, *weights)


PAIR = 2 * RW_HEAD


def _dot_hilo(a, b, dims=(((1,), (0,)), ((), ()))):
    ah = a.astype(BF16)
    al = (a - ah.astype(F32)).astype(BF16)
    bh = b.astype(BF16)
    bl = (b - bh.astype(F32)).astype(BF16)
    f = lambda x, y: lax.dot_general(x, y, dims, preferred_element_type=F32)
    return f(ah, bh) + f(ah, bl) + f(al, bh)


def _dot_tn_hilo(a, b):
    return _dot_hilo(a, b, (((0,), (0,)), ((), ())))


def _wkv_kernel(has_vres, npair, ln, *refs):
    if has_vres:
        (r_ref, lw_ref, k_ref, v_ref, a_ref, vf_ref, vg_ref, s0_ref,
         kk_ref, ka_ref, rk_ref, lnw_ref, lnb_ref, y_ref, sout_ref, s_ref) = refs
    else:
        (r_ref, lw_ref, k_ref, v_ref, a_ref, s0_ref,
         kk_ref, ka_ref, rk_ref, lnw_ref, lnb_ref, y_ref, sout_ref, s_ref) = refs
    c = pl.program_id(2)
    n = RW_HEAD
    rows = 2 * ln

    @pl.when(c == 0)
    def _():
        zero = jnp.zeros((n, n), F32)
        for p in range(npair):
            s_ref[p] = jnp.concatenate([jnp.concatenate([s0_ref[0, 2 * p], zero], axis=1),
                                        jnp.concatenate([zero, s0_ref[0, 2 * p + 1]], axis=1)], axis=0)

    first = lax.broadcasted_iota(jnp.int32, (ln, PAIR), 1) < n
    ri = lax.broadcasted_iota(jnp.int32, (rows, rows), 0)
    ci = lax.broadcasted_iota(jnp.int32, (rows, rows), 1)
    same = (ri // ln) == (ci // ln)
    incl = same & (ci <= ri)
    strict = same & (ci < ri)
    eye = jnp.where(ri == ci, 1.0, 0.0)
    ti = lax.broadcasted_iota(jnp.int32, (ln, ln), 0)
    tj = lax.broadcasted_iota(jnp.int32, (ln, ln), 1)
    tri = jnp.where(tj <= ti, 1.0, 0.0).astype(BF16)

    def seg_sum(x):
        s1 = jnp.sum(jnp.where(first, x, 0.0), axis=-1, keepdims=True)
        s2 = jnp.sum(jnp.where(first, 0.0, x), axis=-1, keepdims=True)
        return jnp.where(first, s1, s2)

    def by_head(x):
        return jnp.concatenate([jnp.where(first, x, 0.0), jnp.where(first, 0.0, x)], axis=0)

    pairs = range(npair)
    lanes = [slice(p * PAIR, (p + 1) * PAIR) for p in pairs]

    def prepare(p):
        r = r_ref[:, lanes[p]]
        lw = lw_ref[:, lanes[p]]
        k = k_ref[:, lanes[p]]
        v = v_ref[:, lanes[p]]
        a = a_ref[:, lanes[p]]
        if has_vres:
            v = v + (vf_ref[:, lanes[p]] - v) * vg_ref[:, lanes[p]]
        kk = k * kk_ref[:, lanes[p]]
        kk = kk / jnp.maximum(jnp.sqrt(seg_sum(kk * kk)), 1e-12)
        k2 = k * (1.0 + (a - 1.0) * ka_ref[:, lanes[p]])
        bv = kk * a
        hi, mid, lo = _split3(lw)
        cl3 = jnp.dot(tri, jnp.concatenate([hi, mid, lo], axis=1), preferred_element_type=F32)
        cl = cl3[:, :PAIR] + cl3[:, PAIR:2 * PAIR] + cl3[:, 2 * PAIR:]
        cl_end = cl[ln - 1:ln, :]
        g_inv = jnp.exp(-cl)
        g_end = jnp.exp(cl_end - cl)
        rt = r * jnp.exp(cl)
        at = -kk * jnp.exp(cl - lw)
        lhs = jnp.concatenate([by_head(at), by_head(rt)], axis=0)
        bh = bv * g_inv
        kh = k2 * g_inv
        return dict(r=r, v=v, k2=k2, lhs=lhs.astype(BF16),
                    rhs=jnp.concatenate([bh, bh, kh, kh], axis=0).astype(BF16),
                    decay=jnp.exp(cl_end), v_h=by_head(v),
                    ws=jnp.concatenate([by_head(bv * g_end), by_head(k2 * g_end)], axis=0))

    st = [prepare(p) for p in pairs]
    aa = [_dot_nt(st[p]["lhs"], st[p]["rhs"]) for p in pairs]
    a_ak = [jnp.where(strict, aa[p][:rows, rows:], 0.0) for p in pairs]
    a_r = [jnp.concatenate([jnp.where(incl, aa[p][rows:, :rows], 0.0),
                            jnp.where(incl, aa[p][rows:, rows:], 0.0)], axis=1).astype(BF16) for p in pairs]
    akv = [_dot_hilo(a_ak[p], st[p]["v_h"]) for p in pairs]
    def same_block(size):
        return (ri // size) == (ci // size)

    below = ci < ri
    tinv = [eye + jnp.where(below & same_block(2), aa[p][:rows, :rows], 0.0) for p in pairs]
    for level in range(1, int(math.log2(ln))):
        size = 2 ** level
        coupling = below & same_block(2 * size) & jnp.logical_not(same_block(size))
        n_off = [jnp.where(coupling, aa[p][:rows, :rows], 0.0).astype(BF16) for p in pairs]
        tinv_bf = [tinv[p].astype(BF16) for p in pairs]
        right = [_dot(n_off[p], tinv_bf[p]) for p in pairs]
        tinv = [tinv[p] + _dot(tinv_bf[p], right[p]) for p in pairs]

    s_prev = [s_ref[p] for p in pairs]
    xy = [_dot_nt(st[p]["lhs"], s_prev[p]) for p in pairs]
    u = [_dot_hilo(tinv[p], xy[p][:rows] + akv[p]) for p in pairs]
    zs = [jnp.concatenate([u[p], st[p]["v_h"]], axis=0) for p in pairs]
    for p in pairs:
        s_ref[p] = s_prev[p] * st[p]["decay"] + _dot_tn_hilo(zs[p], st[p]["ws"])
    y_h = [xy[p][rows:] + _dot(a_r[p], zs[p]) for p in pairs]
    for p in pairs:
        y = y_h[p][:ln] + y_h[p][ln:]
        mean = seg_sum(y) * (1.0 / n)
        yc = y - mean
        var = seg_sum(yc * yc) * (1.0 / n)
        y = yc * lax.rsqrt(var + RW_LNX_EPS) * lnw_ref[:, lanes[p]] + lnb_ref[:, lanes[p]]
        y_ref[:, lanes[p]] = y + seg_sum(st[p]["r"] * st[p]["k2"] * rk_ref[:, lanes[p]]) * st[p]["v"]

    @pl.when(c == pl.num_programs(2) - 1)
    def _():
        for p in range(npair):
            sout_ref[0, 2 * p] = s_ref[p, :n, :n]
            sout_ref[0, 2 * p + 1] = s_ref[p, n:, n:]


def wkv_scan(r, lw, k, v, a, s0, p, batch, seq, npair, ln, vres=None):
    nc = seq // ln
    has_vres = vres is not None
    width = npair * PAIR
    tok = pl.BlockSpec((ln, width), lambda b, q, c: (b * nc + c, q))
    par = pl.BlockSpec((1, width), lambda b, q, c: (0, q))
    st = pl.BlockSpec((1, 2 * npair, RW_HEAD, RW_HEAD), lambda b, q, c: (b, q, 0, 0))
    seqs = [r, lw, k, v, a] + (list(vres) if has_vres else [])
    pars = [p[nm].reshape(1, D_MODEL) for nm in ("k_k", "k_a", "r_k", "lnx_w", "lnx_b")]
    return pl.pallas_call(
        functools.partial(_wkv_kernel, has_vres, npair, ln),
        grid=(batch, D_MODEL // width, nc),
        in_specs=[tok] * len(seqs) + [st] + [par] * 5,
        out_specs=[tok, st],
        out_shape=[jax.ShapeDtypeStruct(r.shape, F32), jax.ShapeDtypeStruct(s0.shape, F32)],
        scratch_shapes=[pltpu.VMEM((npair, PAIR, PAIR), F32)],
        compiler_params=_params("parallel", "parallel", "arbitrary"),
        name="wkv_scan",
    )(*seqs, s0, *pars)


def prep_rwkv(w, j, i):
    row = lambda z: z.reshape(1, -1).astype(F32)
    p = {
        "norm_mix": row(w["norm_mix"][i]),
        "mu": w["rwkv_mu"][j],
        "w_rkv": w["rwkv_w_rkv"][j].astype(BF16),
        "w0": row(w["rwkv_w0"][j]), "w1": w["rwkv_w1"][j].astype(BF16), "w2": w["rwkv_w2"][j].astype(BF16),
        "a0": row(w["rwkv_a0"][j]), "a1": w["rwkv_a1"][j].astype(BF16), "a2": w["rwkv_a2"][j].astype(BF16),
        "g1": w["rwkv_g1"][j].astype(BF16), "g2": w["rwkv_g2"][j].astype(BF16),
        "k_k": w["rwkv_k_k"][j], "k_a": w["rwkv_k_a"][j], "r_k": w["rwkv_r_k"][j],
        "lnx_w": w["rwkv_lnx_w"][j], "lnx_b": w["rwkv_lnx_b"][j],
        "w_o": w["rwkv_w_o"][j].astype(BF16),
    }
    if j > 0:
        p["v0"] = row(w["rwkv_v0"][j - 1])
        p["v1"] = w["rwkv_v1"][j - 1].astype(BF16)
        p["v2"] = w["rwkv_v2"][j - 1].astype(BF16)
    return p


def rwkv_layer(h, shift, s0, vfirst, p, batch, seq, tm, npair, ln):
    m, c = h.shape
    outs = rwkv_proj(h, shift, p, seq, tm)
    r, lw, k, v, a, g = outs[:6]
    vres = None if vfirst is None else (vfirst, outs[6])
    y, s_new = wkv_scan(r, lw, k, v, a, s0, p, batch, seq, npair, ln, vres)
    shift_new = rmsnorm_rows(h.reshape(batch, seq, c)[:, -1], p["norm_mix"], batch)
    h = linear_res(y, p["w_o"], h, tm, gate=g)
    return h, s_new, shift_new, (v if vfirst is None else vfirst)


SEG_PAD = 128


def _dot_hilo_rhs(x, ones):
    hi = x.astype(BF16)
    lo = (x - hi.astype(F32)).astype(BF16)
    return jnp.dot(hi, ones, preferred_element_type=F32) + jnp.dot(lo, ones, preferred_element_type=F32)


def _seg_rms_scale(x, seg_ref, segt_ref, invw_ref):
    ssq = _dot(x * x, seg_ref[...])
    return _dot_hilo_rhs(lax.rsqrt(ssq * invw_ref[...] + NORM_EPS), segt_ref[...])


def _swap_halves(x, group):
    half = group // 2
    width = x.shape[-1]
    lane = lax.broadcasted_iota(jnp.int32, x.shape, x.ndim - 1)
    return jnp.where((lane % group) < half,
                     pltpu.roll(x, width - half, axis=x.ndim - 1),
                     pltpu.roll(x, half, axis=x.ndim - 1))


SLOT = 128


def _mla_proj_kernel(with_kv, *refs):
    (h_ref, cos_ref, sin_ref, gmix_ref, win_ref, gq_ref, gkv_ref, gkr_ref, wuq_ref, gqc_ref,
     seg_ref, segt_ref, invw_ref) = refs[:13]
    if with_kv:
        wuk_ref, wuv_ref, gkn_ref, qc_ref, c_ref, kp_ref, kc_ref, v_ref = refs[13:]
    else:
        qc_ref, c_ref, kp_ref = refs[13:]
    x = _rms(h_ref[...], gmix_ref[...])
    hp = _dot(x, win_ref[...])
    q_a = _rms(hp[:, :MLA_Q_LORA], gq_ref[...])
    c = _rms(hp[:, MLA_Q_LORA:MLA_Q_LORA + MLA_KV_LORA], gkv_ref[...])
    c_ref[...] = c

    cos = cos_ref[...]
    sin = sin_ref[...]

    def rope(z, cs, sn):
        return z * cs + _swap_halves(z, MLA_ROPE) * sn

    kp_raw = hp[:, MLA_Q_LORA + MLA_KV_LORA:]
    kp_scale = lax.rsqrt(jnp.sum(kp_raw * kp_raw, axis=-1, keepdims=True) * (1.0 / MLA_ROPE) + NORM_EPS)
    kp = rope(pltpu.roll(kp_raw, MLA_NOPE, axis=1) * gkr_ref[...], cos, sin) * kp_scale
    kp_ref[...] = kp[:, MLA_NOPE:MLA_NOPE + MLA_ROPE]

    q = _dot(q_a, wuq_ref[...])
    cos_h = jnp.tile(cos, (1, MLA_HEADS))
    sin_h = jnp.tile(sin, (1, MLA_HEADS))
    qc = rope(q * gqc_ref[...], cos_h, sin_h) * _seg_rms_scale(q, seg_ref, segt_ref, invw_ref)
    qc_ref[...] = qc.astype(BF16)
    if with_kv:
        kraw = _dot(c, wuk_ref[...])
        kn = kraw * _seg_rms_scale(kraw, seg_ref, segt_ref, invw_ref) * gkn_ref[...]
        kc_ref[...] = (kn + jnp.tile(kp, (1, MLA_HEADS))).astype(BF16)
        v_ref[0] = _dot_nt(wuv_ref[...], c).astype(BF16)


def _slot_seg_matrix():
    lane = jnp.arange(MLA_HEADS * SLOT)
    head, off = lane // SLOT, lane % SLOT
    col = jnp.where(off < MLA_NOPE, head, jnp.where(off < MLA_NOPE + MLA_ROPE, MLA_HEADS + head, SEG_PAD))
    return (col[:, None] == jnp.arange(SEG_PAD)[None, :]).astype(BF16)


def _slot(nope, rope):
    pad = jnp.zeros(nope.shape[:-1] + (SLOT - MLA_NOPE - MLA_ROPE,), nope.dtype)
    return jnp.concatenate([nope, rope, pad], axis=-1)


def _rope_tables(pos):
    half = MLA_ROPE // 2
    inv = ROPE_THETA ** (-jnp.arange(half, dtype=F32) / half)
    ang = pos.astype(F32)[:, None] * inv[None, :]
    cos = jnp.cos(ang)
    sin = jnp.sin(ang)
    ones = jnp.ones((pos.shape[0], MLA_NOPE), F32)
    return (_slot(ones, jnp.concatenate([cos, cos], axis=1)),
            _slot(jnp.zeros_like(ones), jnp.concatenate([-sin, sin], axis=1)))


def mla_proj(h, pos, p, tm, with_kv):
    m, c = h.shape
    seq = pos.shape[0]
    cos, sin = _rope_tables(pos)
    if seq >= tm:
        nrep = seq // tm
    else:
        cos, sin = jnp.tile(cos, (tm // seq, 1)), jnp.tile(sin, (tm // seq, 1))
        nrep = 1
    row = lambda width: pl.BlockSpec((tm, width), lambda i: (i, 0))
    tab = pl.BlockSpec((tm, SLOT), lambda i: (i % nrep, 0))
    full = lambda arr: pl.BlockSpec(arr.shape, lambda i: (0,) * arr.ndim)
    seg = _slot_seg_matrix()
    invw = jnp.concatenate([jnp.full((MLA_HEADS,), 1.0 / MLA_NOPE, F32), jnp.full((MLA_HEADS,), 1.0 / MLA_ROPE, F32),
                            jnp.ones((SEG_PAD - 2 * MLA_HEADS,), F32)]).reshape(1, SEG_PAD)
    weights = [p["norm_mix"], p["w_in"], p["q_norm"], p["kv_norm"], p["kr_norm"], p["w_uq"],
               p["q_gain"] if with_kv else p["q_gain_abs"], seg, seg.T, invw]
    wide = MLA_HEADS * SLOT
    out_shape = [jax.ShapeDtypeStruct((m, wide), BF16),
                 jax.ShapeDtypeStruct((m, MLA_KV_LORA), F32), jax.ShapeDtypeStruct((m, MLA_ROPE), F32)]
    out_specs = [row(wide), row(MLA_KV_LORA), row(MLA_ROPE)]
    if with_kv:
        weights += [p["w_uk"], p["w_uv_t"], p["kn_gain"]]
        out_shape += [jax.ShapeDtypeStruct((m, wide), BF16),
                      jax.ShapeDtypeStruct((m // tm, MLA_HEADS * MLA_V, tm), BF16)]
        out_specs += [row(wide), pl.BlockSpec((1, MLA_HEADS * MLA_V, tm), lambda i: (i, 0, 0))]
    return pl.pallas_call(
        functools.partial(_mla_proj_kernel, with_kv),
        grid=(m // tm,),
        in_specs=[row(c), tab, tab] + [full(w) for w in weights],
        out_specs=out_specs,
        out_shape=out_shape,
        compiler_params=_params("parallel"),
        name="mla_proj",
    )(h, cos, sin, *weights)


def prep_mla(w, j, i):
    row = lambda z: z.reshape(1, -1).astype(F32)
    w_in = w["mla_w_in"][j]
    pad = jnp.zeros((D_MODEL, 128 - MLA_ROPE), F32)
    w_uq = w["mla_w_uq"][j]
    w_uk = w["mla_w_uk"][j]
    zero_n = jnp.zeros((MLA_NOPE,), F32)
    zero_r = jnp.zeros((MLA_ROPE,), F32)
    qn, qr, kn = w["mla_qn_norm"][j], w["mla_qr_norm"][j], w["mla_kn_norm"][j]
    return {
        "norm_mix": row(w["norm_mix"][i]),
        "w_in": jnp.concatenate([w_in, pad], axis=1).astype(BF16),
        "q_norm": row(w["mla_q_norm"][j]), "kv_norm": row(w["mla_kv_norm"][j]),
        "kr_norm": row(_slot(zero_n, w["mla_kr_norm"][j])),
        "w_uq": _slot(w_uq[:, :, :MLA_NOPE], w_uq[:, :, MLA_NOPE:]).reshape(MLA_Q_LORA, -1).astype(BF16),
        "q_gain": row(jnp.tile(_slot(qn, qr) * MLA_SCALE, MLA_HEADS)),
        "q_gain_abs": row(jnp.tile(_slot(qn * kn, qr) * MLA_SCALE, MLA_HEADS)),
        "kn_gain": row(jnp.tile(_slot(kn, zero_r), MLA_HEADS)),
        "w_uk": _slot(w_uk, jnp.zeros(w_uk.shape[:2] + (MLA_ROPE,), F32)).reshape(MLA_KV_LORA, -1).astype(BF16),
        "w_uv": w["mla_w_uv"][j].reshape(MLA_KV_LORA, -1).astype(BF16),
        "w_uv_t": w["mla_w_uv"][j].reshape(MLA_KV_LORA, -1).T.astype(BF16),
        "w_uk_t": w["mla_w_uk"][j].reshape(MLA_KV_LORA, -1).T.astype(BF16),
        "w_uk_heads": jnp.transpose(w["mla_w_uk"][j], (1, 2, 0)).astype(BF16),
        "w_o": w["mla_w_o"][j].astype(BF16),
    }


ATTN_HEADS_PER_STEP = 8
NEG_BIG = -1e30


def _flash_kernel(tq, qc_ref, kc_ref, vt_ref, o_ref, m_ref, l_ref, acc_ref):
    qi = pl.program_id(2)
    g = ATTN_HEADS_PER_STEP
    heads = range(g)
    key_i = lax.broadcasted_iota(jnp.int32, (tq, tq), 0)
    qry_i = lax.broadcasted_iota(jnp.int32, (tq, tq), 1)
    m_ref[...] = jnp.full_like(m_ref, NEG_BIG)
    l_ref[...] = jnp.zeros_like(l_ref)
    acc_ref[...] = jnp.zeros_like(acc_ref)

    def block(j, masked):
        keys = pl.ds(pl.multiple_of(j * tq, tq), tq)
        s = [_dot_nt(kc_ref[0, keys, hh * SLOT:(hh + 1) * SLOT], qc_ref[0, :, hh * SLOT:(hh + 1) * SLOT])
             for hh in heads]
        if masked:
            s = [jnp.where(key_i <= qry_i, s[hh], NEG_BIG) for hh in heads]
        m_prev = [m_ref[hh] for hh in heads]
        m_new = [jnp.maximum(m_prev[hh], jnp.max(s[hh], axis=0, keepdims=True)) for hh in heads]
        alpha = [jnp.exp(m_prev[hh] - m_new[hh]) for hh in heads]
        pr = [jnp.exp(s[hh] - m_new[hh]) for hh in heads]
        pv = [jnp.dot(vt_ref[j, hh * MLA_V:(hh + 1) * MLA_V, :], pr[hh].astype(BF16), preferred_element_type=F32)
              for hh in heads]
        for hh in heads:
            l_ref[hh] = alpha[hh] * l_ref[hh] + jnp.sum(pr[hh], axis=0, keepdims=True)
            acc_ref[hh] = alpha[hh] * acc_ref[hh] + pv[hh]
            m_ref[hh] = m_new[hh]

    def body(j, carry):
        block(j, False)
        return carry

    lax.fori_loop(0, qi, body, 0)
    block(qi, True)
    o_t = jnp.concatenate([acc_ref[hh] / l_ref[hh] for hh in heads], axis=0)
    o_ref[0] = o_t.T


def flash_prompt(qc, kc, vt, batch, seq, tq):
    g = ATTN_HEADS_PER_STEP
    nq = seq // tq
    r3 = lambda z: z.reshape(batch, seq, -1)
    return pl.pallas_call(
        functools.partial(_flash_kernel, tq),
        grid=(batch, MLA_HEADS // g, nq),
        in_specs=[pl.BlockSpec((1, tq, g * SLOT), lambda b, hq, i: (b, i, hq)),
                  pl.BlockSpec((1, seq, g * SLOT), lambda b, hq, i: (b, 0, hq)),
                  pl.BlockSpec((nq, g * MLA_V, tq), lambda b, hq, i: (b, hq, 0))],
        out_specs=pl.BlockSpec((1, tq, g * MLA_V), lambda b, hq, i: (b, i, hq)),
        out_shape=jax.ShapeDtypeStruct((batch, seq, MLA_HEADS * MLA_V), F32),
        scratch_shapes=[pltpu.VMEM((g, 1, tq), F32), pltpu.VMEM((g, 1, tq), F32),
                        pltpu.VMEM((g, MLA_V, tq), F32)],
        compiler_params=_params("parallel", "parallel", "parallel"),
        name="flash_prompt",
    )(r3(qc), r3(kc), vt).reshape(batch * seq, -1)


def _bmm_kernel(a_ref, b_ref, o_ref):
    o_ref[0] = _dot(a_ref[0], b_ref[0]).astype(o_ref.dtype)


def bmm(a, b, out_dtype):
    g, m, k = a.shape
    n = b.shape[2]
    return pl.pallas_call(
        _bmm_kernel,
        grid=(g,),
        in_specs=[pl.BlockSpec((1, m, k), lambda i: (i, 0, 0)), pl.BlockSpec((1, k, n), lambda i: (i, 0, 0))],
        out_specs=pl.BlockSpec((1, m, n), lambda i: (i, 0, 0)),
        out_shape=jax.ShapeDtypeStruct((g, m, n), out_dtype),
        compiler_params=_params("parallel"),
        name="bmm",
    )(a, b)


def _paged_attn_kernel(pp, seq, n_pages, pt_ref, qa_ref, qp_ref, cn_ref, kpn_ref, wukt_ref, wuv_ref,
                       pool_c_ref, pool_kp_ref, o_ref, m_ref, l_ref, acc_ref, lhs_ref, c_buf, kp_buf, sems):
    b = pl.program_id(0)
    n_groups = n_pages // pp
    nq = seq * MLA_HEADS
    nup = MLA_HEADS * MLA_NOPE

    def group_copies(bb, g, slot):
        out = []
        for i in range(pp):
            page = pt_ref[bb * n_pages + g * pp + i]
            out.append(pltpu.make_async_copy(pool_c_ref.at[page], c_buf.at[slot, i], sems.at[0, slot]))
            out.append(pltpu.make_async_copy(pool_kp_ref.at[page], kp_buf.at[slot, i], sems.at[1, slot]))
        return out

    @pl.when(b == 0)
    def _():
        for cp in group_copies(b, 0, 0):
            cp.start()

    m_ref[...] = jnp.full_like(m_ref, NEG_BIG)
    l_ref[...] = jnp.zeros_like(l_ref)
    acc_ref[...] = jnp.zeros_like(acc_ref)
    lhs_ref[:nup, :] = wukt_ref[...]
    lhs_ref[nup:, :] = qa_ref[0]

    qp = qp_ref[0]

    def attend(c_blks, kp_t_blks, mask):
        subs = range(len(c_blks))
        nk = c_blks[0].shape[0]
        c_bf = [c_blks[i].astype(BF16) for i in subs]
        both = [_dot_nt(lhs_ref[...], c_bf[i]) for i in subs]
        ssq = [jnp.sum(jnp.square(both[i][:nup]).reshape(MLA_HEADS, MLA_NOPE, nk), axis=1) for i in subs]
        rs = [lax.rsqrt(ssq[i] * (1.0 / MLA_NOPE) + NORM_EPS) for i in subs]
        s = [both[i][nup:] * jnp.concatenate([rs[i]] * seq, axis=0) + _dot(qp, kp_t_blks[i]) for i in subs]
        if mask is not None:
            s = [jnp.where(mask, s[i], NEG_BIG) for i in subs]
        m_prev = m_ref[...]
        m_new = m_prev
        for i in subs:
            m_new = jnp.maximum(m_new, jnp.max(s[i], axis=-1, keepdims=True))
        alpha = jnp.exp(m_prev - m_new)
        pr = [jnp.exp(s[i] - m_new) for i in subs]
        l_new = alpha * l_ref[...]
        acc = alpha * acc_ref[...]
        for i in subs:
            l_new = l_new + jnp.sum(pr[i], axis=-1, keepdims=True)
            acc = acc + _dot(pr[i], c_bf[i])
        l_ref[...] = l_new
        acc_ref[...] = acc
        m_ref[...] = m_new

    pages_per_sub = 2

    def group_step(g, carry):
        slot = (b * n_groups + g) % 2
        last = g == n_groups - 1
        next_b = jnp.where(last, b + 1, b)
        next_g = jnp.where(last, 0, g + 1)

        @pl.when(next_b < pl.num_programs(0))
        def _():
            for cp in group_copies(next_b, next_g, 1 - slot):
                cp.start()

        for cp in group_copies(b, g, slot):
            cp.wait()
        subs = range(0, pp, pages_per_sub)
        attend([c_buf[slot, pl.ds(i, pages_per_sub)].reshape(pages_per_sub * PAGE_SIZE, MLA_KV_LORA) for i in subs],
               [jnp.concatenate([kp_buf[slot, i + k] for k in range(pages_per_sub)], axis=1) for i in subs], None)
        return carry

    lax.fori_loop(0, n_groups, group_step, 0)

    qtok = lax.broadcasted_iota(jnp.int32, (nq, seq), 0) // MLA_HEADS
    ktok = lax.broadcasted_iota(jnp.int32, (nq, seq), 1)
    attend([cn_ref[0]], [kpn_ref[0]], ktok <= qtok)
    o_lat = acc_ref[...] / l_ref[...]
    full = _dot(o_lat, wuv_ref[...])
    rhead = lax.broadcasted_iota(jnp.int32, full.shape, 0) % MLA_HEADS
    lhead = lax.broadcasted_iota(jnp.int32, full.shape, 1) // MLA_V
    full = jnp.where(rhead == lhead, full, 0.0)
    o_ref[0] = jnp.sum(full.reshape(seq, MLA_HEADS, MLA_HEADS * MLA_V), axis=1)


def paged_attn(q_abs, qp, c_new, kp_new, pool_c, pool_kp_t, page_table, p, batch, seq, pp):
    n_pages = page_table.shape[1]
    assert n_pages % pp == 0 and pp % 2 == 0
    nq = seq * MLA_HEADS
    per_b = lambda shp: pl.BlockSpec((1,) + shp, lambda b, pt: (b, 0, 0))
    full = lambda arr: pl.BlockSpec(arr.shape, lambda b, pt: (0,) * arr.ndim)
    in_hbm = pl.BlockSpec(memory_space=pl.ANY)
    grid_spec = pltpu.PrefetchScalarGridSpec(
        num_scalar_prefetch=1,
        grid=(batch,),
        in_specs=[per_b((nq, MLA_KV_LORA)), per_b((nq, MLA_ROPE)), per_b((seq, MLA_KV_LORA)),
                  per_b((MLA_ROPE, seq)), full(p["w_uk_t"]), full(p["w_uv"]), in_hbm, in_hbm],
        out_specs=per_b((seq, MLA_HEADS * MLA_V)),
        scratch_shapes=[pltpu.VMEM((nq, 1), F32), pltpu.VMEM((nq, 1), F32), pltpu.VMEM((nq, MLA_KV_LORA), F32),
                        pltpu.VMEM((MLA_HEADS * MLA_NOPE + nq, MLA_KV_LORA), BF16),
                        pltpu.VMEM((2, pp, PAGE_SIZE, MLA_KV_LORA), F32),
                        pltpu.VMEM((2, pp, MLA_ROPE, PAGE_SIZE), F32),
                        pltpu.SemaphoreType.DMA((2, 2))],
    )
    return pl.pallas_call(
        functools.partial(_paged_attn_kernel, pp, seq, n_pages),
        grid_spec=grid_spec,
        out_shape=jax.ShapeDtypeStruct((batch, seq, MLA_HEADS * MLA_V), F32),
        compiler_params=_params("arbitrary"),
        name="paged_attn",
    )(page_table.reshape(-1), q_abs, qp, c_new.reshape(batch, seq, -1),
      jnp.swapaxes(kp_new.reshape(batch, seq, -1), 1, 2),
      p["w_uk_t"], p["w_uv"], pool_c, pool_kp_t)


def mla_layer_prompt(h, p, batch, seq, tm, tq):
    assert tm == tq, "the value tiles written by mla_proj are the key blocks of flash_prompt"
    qc, c, kp, kc, vt = mla_proj(h, jnp.arange(seq), p, tm, True)
    o = flash_prompt(qc, kc, vt, batch, seq, tq)
    return linear_res(o, p["w_o"], h, tm), c, kp


def mla_layer_sample(h, pool_c, pool_kp, page_table, p, batch, seq, past_len, tm, pp):
    m = batch * seq
    qc, c, kp = mla_proj(h, past_len + jnp.arange(seq), p, tm, False)
    qc = qc.reshape(m, MLA_HEADS, SLOT)
    q_heads = jnp.swapaxes(qc[:, :, :MLA_NOPE], 0, 1)
    q_abs = bmm(q_heads, p["w_uk_heads"], BF16)
    q_abs = jnp.swapaxes(q_abs, 0, 1).reshape(batch, seq * MLA_HEADS, MLA_KV_LORA)
    qp = qc[:, :, MLA_NOPE:MLA_NOPE + MLA_ROPE].reshape(batch, seq * MLA_HEADS, MLA_ROPE)
    o = paged_attn(q_abs, qp, c, kp, pool_c, jnp.swapaxes(pool_kp, 1, 2), page_table, p, batch, seq, pp)
    return linear_res(o.reshape(m, -1), p["w_o"], h, tm), c, kp


DT_PAD = 128
CONV_HALO = 8
MB_GN = MB_GROUPS * MB_STATE
MB_GROUP_INNER = MB_INNER // MB_GROUPS
MB_HEADS_PER_GROUP = MB_HEADS // MB_GROUPS


def _ssd_kernel(ck, xbc_ref, prev_ref, cs_ref, z_ref, dtr_ref, h0_ref, cw_ref, cb_ref, dtb_ref, alog_ref,
                dskip_ref, nw_ref, expand_ref, y_ref, hout_ref, h_ref):
    c = pl.program_id(1)

    @pl.when(c == 0)
    def _():
        h_ref[...] = h0_ref[0]

    halo = jnp.where(c == 0, cs_ref[0], prev_ref[0])
    xext = jnp.concatenate([halo, xbc_ref[0]], axis=0)
    conv = cb_ref[...] + xext[CONV_HALO:, :] * cw_ref[MB_CONV - 1:MB_CONV, :]
    for j in range(MB_CONV - 1):
        shifted = pltpu.roll(xext, MB_CONV - 1 - j, axis=0)[CONV_HALO:, :]
        conv = conv + shifted * cw_ref[j:j + 1, :]
    xbc = _silu(conv)
    xs = xbc[:, :MB_INNER]
    bm = xbc[:, MB_INNER:MB_INNER + MB_GN]
    cm = xbc[:, MB_INNER + MB_GN:]

    dt = _softplus(dtr_ref[0] + dtb_ref[...])
    da = dt * (-jnp.exp(alog_ref[...]))
    ri = lax.broadcasted_iota(jnp.int32, (ck, ck), 0)
    ci = lax.broadcasted_iota(jnp.int32, (ck, ck), 1)
    causal = ci <= ri
    tri = jnp.where(causal, 1.0, 0.0).astype(BF16)
    hi, mid, lo = _split3(da)
    acum = (jnp.dot(tri, hi, preferred_element_type=F32) + jnp.dot(tri, mid, preferred_element_type=F32)
            + jnp.dot(tri, lo, preferred_element_type=F32))
    tn = (((0,), (0,)), ((), ()))
    tri_t = jnp.where(ri <= ci, 1.0, 0.0).astype(BF16)
    acum_t = (lax.dot_general(hi, tri_t, tn, preferred_element_type=F32)
              + lax.dot_general(mid, tri_t, tn, preferred_element_type=F32)
              + lax.dot_general(lo, tri_t, tn, preferred_element_type=F32))
    e_last = jnp.exp(acum[ck - 1:ck, :])

    spread = expand_ref[...]
    dt_hi = dt.astype(BF16)
    dt_lo = (dt - dt_hi.astype(F32)).astype(BF16)
    dt_x = jnp.dot(dt_hi, spread, preferred_element_type=F32) + jnp.dot(dt_lo, spread, preferred_element_type=F32)
    ah, am, al = _split3(acum)
    acum_x = (jnp.dot(ah, spread, preferred_element_type=F32) + jnp.dot(am, spread, preferred_element_type=F32)
              + jnp.dot(al, spread, preferred_element_type=F32))
    xdt = xs * dt_x
    xdt_end = xdt * jnp.exp(acum_x[ck - 1:ck, :] - acum_x)
    e_cum_x = jnp.exp(acum_x)

    pair_w = 2 * MB_HEAD
    first = lax.broadcasted_iota(jnp.int32, (ck, pair_w), 1) < MB_HEAD
    upper = lax.broadcasted_iota(jnp.int32, (pair_w, MB_STATE), 0) < MB_HEAD
    pairs_per_group = MB_HEADS_PER_GROUP // 2
    ys = []
    for g in range(MB_GROUPS):
        b_g = bm[:, g * MB_STATE:(g + 1) * MB_STATE]
        c_g = cm[:, g * MB_STATE:(g + 1) * MB_STATE]
        cb = _dot_nt(c_g, b_g)
        cols = slice(g * MB_GROUP_INNER, (g + 1) * MB_GROUP_INNER)
        y_state = _dot_nt(c_g, h_ref[cols, :]) * e_cum_x[:, cols]
        lmat = []
        for hh in range(MB_HEADS_PER_GROUP):
            hd = g * MB_HEADS_PER_GROUP + hh
            seg = acum[:, hd:hd + 1] - acum_t[hd:hd + 1, :]
            lmat.append(cb * jnp.where(causal, jnp.exp(jnp.where(causal, seg, 0.0)), 0.0))
        for pr in range(pairs_per_group):
            lanes = slice(g * MB_GROUP_INNER + pr * pair_w, g * MB_GROUP_INNER + (pr + 1) * pair_w)
            x_p = xdt[:, lanes]
            y_p = jnp.where(first, _dot(lmat[2 * pr], x_p), _dot(lmat[2 * pr + 1], x_p))
            ys.append(y_p + y_state[:, pr * pair_w:(pr + 1) * pair_w])
            hd = g * MB_HEADS_PER_GROUP + 2 * pr
            decay = jnp.where(upper, e_last[:, hd:hd + 1], e_last[:, hd + 1:hd + 2])
            h_ref[lanes, :] = h_ref[lanes, :] * decay + _dot_tn(xdt_end[:, lanes], b_g)

    y = jnp.concatenate(ys, axis=1) + dskip_ref[...] * xs
    yz = y * _silu(z_ref[0])
    outs = []
    for g in range(MB_GROUPS):
        yg = yz[:, g * MB_GROUP_INNER:(g + 1) * MB_GROUP_INNER]
        outs.append(yg * lax.rsqrt(jnp.mean(yg * yg, axis=-1, keepdims=True) + NORM_EPS))
    y_ref[0] = (jnp.concatenate(outs, axis=1) * nw_ref[...]).astype(y_ref.dtype)

    @pl.when(c == pl.num_programs(1) - 1)
    def _():
        hout_ref[0] = h_ref[...]


def ssd_scan(xbc, z, dt_raw, conv_state, h0, p, batch, seq, ck):
    assert CONV_HALO % 8 == 0 and ck % CONV_HALO == 0
    nc = seq // ck
    halo_blocks = ck // CONV_HALO
    cs8 = jnp.concatenate([jnp.zeros((batch, CONV_HALO - (MB_CONV - 1), MB_CONV_DIM), F32), conv_state], axis=1)
    chunk = lambda width: pl.BlockSpec((1, ck, width), lambda b, c: (b, c, 0))
    full = lambda arr: pl.BlockSpec(arr.shape, lambda b, c: (0,) * arr.ndim)
    expand = (jnp.arange(DT_PAD)[:, None] == jnp.arange(MB_INNER)[None, :] // MB_HEAD).astype(BF16)
    weights = [p["conv_w"], p["conv_b"], p["dt_bias"], p["a_log"], p["d_skip"], p["norm_w"], expand]
    state = pl.BlockSpec((1, MB_INNER, MB_STATE), lambda b, c: (b, 0, 0))
    return pl.pallas_call(
        functools.partial(_ssd_kernel, ck),
        grid=(batch, nc),
        in_specs=[chunk(MB_CONV_DIM),
                  pl.BlockSpec((1, CONV_HALO, MB_CONV_DIM),
                               lambda b, c: (b, jnp.maximum(c * halo_blocks - 1, 0), 0)),
                  pl.BlockSpec((1, CONV_HALO, MB_CONV_DIM), lambda b, c: (b, 0, 0)),
                  chunk(MB_INNER), chunk(DT_PAD), state] + [full(w) for w in weights],
        out_specs=[chunk(MB_INNER), state],
        out_shape=[jax.ShapeDtypeStruct((batch, seq, MB_INNER), BF16),
                   jax.ShapeDtypeStruct((batch, MB_INNER, MB_STATE), F32)],
        scratch_shapes=[pltpu.VMEM((MB_INNER, MB_STATE), F32)],
        compiler_params=_params("parallel", "arbitrary"),
        name="ssd_scan",
    )(xbc, xbc, cs8, z, dt_raw, h0, *weights)


def prep_mamba(w, j, i):
    row = lambda z: z.reshape(1, -1).astype(F32)
    w_in = w["mamba_w_in"][j]
    padv = lambda z: jnp.concatenate([z.astype(F32), jnp.zeros((DT_PAD - MB_HEADS,), F32)])
    return {
        "norm_mix": w["norm_mix"][i],
        "w_z": w_in[:, :MB_INNER].astype(BF16),
        "w_xbc": w_in[:, MB_INNER:MB_INNER + MB_CONV_DIM].astype(BF16),
        "w_dt": jnp.concatenate([w_in[:, MB_INNER + MB_CONV_DIM:], jnp.zeros((D_MODEL, DT_PAD - MB_HEADS), F32)],
                                axis=1).astype(BF16),
        "conv_w": w["mamba_conv_w"][j], "conv_b": row(w["mamba_conv_b"][j]),
        "dt_bias": row(padv(w["mamba_dt_bias"][j])), "a_log": row(padv(w["mamba_a_log"][j])),
        "d_skip": row(jnp.repeat(w["mamba_d"][j].astype(F32), MB_HEAD)),
        "norm_w": row(w["mamba_norm"][j]),
        "w_o": w["mamba_w_o"][j].astype(BF16),
    }


def mamba_layer(h, conv_state, h0, p, batch, seq, tm, ck):
    m = batch * seq
    assert seq >= MB_CONV - 1
    tm_in = min(TM_FFN, m)
    z = norm_linear(h, p["norm_mix"], p["w_z"], tm_in, 1024)
    xbc = norm_linear(h, p["norm_mix"], p["w_xbc"], tm_in, 1024)
    dt_raw = norm_linear(h, p["norm_mix"], p["w_dt"], tm_in, DT_PAD)
    xbc3 = xbc.reshape(batch, seq, MB_CONV_DIM)
    y, h_new = ssd_scan(xbc3, z.reshape(batch, seq, MB_INNER), dt_raw.reshape(batch, seq, DT_PAD),
                        conv_state, h0.reshape(batch, MB_INNER, MB_STATE), p, batch, seq, ck)
    h = linear_res(y.reshape(m, MB_INNER), p["w_o"], h, tm)
    return h, xbc3[:, seq - (MB_CONV - 1):], h_new.reshape(batch, MB_HEADS, MB_HEAD, MB_STATE)


N_MIXERS = 3
TM_PROJ = 256
TM_ROWS = 512
TM_FFN = 1024
TH_FFN = 1024
WKV_CHUNK = 64
WKV_PAIRS_PER_STEP = 8
ATTN_TQ = 256
PAGES_PER_STEP = 8


def kernel(x_prompt, x_sample, cache_mla_ckv, cache_mla_kpe, state_rwkv_wkv, state_rwkv_shift, state_ssm, state_conv, page_table, norm_mix, norm_ffn, ffn_w1, ffn_w2, rwkv_mu, rwkv_w_rkv, rwkv_w0, rwkv_w1, rwkv_w2, rwkv_a0, rwkv_a1, rwkv_a2, rwkv_v0, rwkv_v1, rwkv_v2, rwkv_g1, rwkv_g2, rwkv_k_k, rwkv_k_a, rwkv_r_k, rwkv_lnx_w, rwkv_lnx_b, rwkv_w_o, mla_w_in, mla_q_norm, mla_kv_norm, mla_w_uq, mla_w_uk, mla_w_uv, mla_qn_norm, mla_qr_norm, mla_kn_norm, mla_kr_norm, mla_w_o, mamba_w_in, mamba_conv_w, mamba_conv_b, mamba_dt_bias, mamba_a_log, mamba_d, mamba_norm, mamba_w_o):
    w = dict(locals())
    bp, tp, c = x_prompt.shape
    bs, ts, _ = x_sample.shape
    depth = norm_mix.shape[0]
    past_len = page_table.shape[1] * PAGE_SIZE
    hp = x_prompt.reshape(bp * tp, c)
    hs = x_sample.reshape(bs * ts, c)
    vf_p = vf_s = None
    out = {k: [] for k in ("ckv_p", "kpe_p", "ckv_s", "kpe_s", "wkv_p", "sh_p", "wkv_s", "sh_s",
                           "ssm_p", "conv_p", "ssm_s", "conv_s")}
    for i in range(depth):
        kind, j = i % N_MIXERS, i // N_MIXERS
        if kind == 0:
            p = prep_rwkv(w, j, i)
            hp, s_p, l_p, vf_p = rwkv_layer(hp, jnp.zeros((bp, c), F32), jnp.zeros((bp, RW_HEADS, RW_HEAD, RW_HEAD), F32),
                                            vf_p, p, bp, tp, TM_ROWS, WKV_PAIRS_PER_STEP, WKV_CHUNK)
            hs, s_s, l_s, vf_s = rwkv_layer(hs, state_rwkv_shift[j], state_rwkv_wkv[j], vf_s, p, bs, ts,
                                            TM_ROWS, WKV_PAIRS_PER_STEP, ts)
            out["wkv_p"].append(s_p); out["sh_p"].append(l_p); out["wkv_s"].append(s_s); out["sh_s"].append(l_s)
        elif kind == 1:
            p = prep_mla(w, j, i)
            hp, c_p, k_p = mla_layer_prompt(hp, p, bp, tp, TM_PROJ, ATTN_TQ)
            hs, c_s, k_s = mla_layer_sample(hs, cache_mla_ckv[j], cache_mla_kpe[j], page_table, p, bs, ts,
                                            past_len, TM_PROJ, PAGES_PER_STEP)
            out["ckv_p"].append(c_p.reshape(bp, tp, -1)); out["kpe_p"].append(k_p.reshape(bp, tp, -1))
            out["ckv_s"].append(c_s.reshape(bs, ts, -1)); out["kpe_s"].append(k_s.reshape(bs, ts, -1))
        else:
            p = prep_mamba(w, j, i)
            hp, cv_p, h_p = mamba_layer(hp, jnp.zeros((bp, MB_CONV - 1, MB_CONV_DIM), F32),
                                        jnp.zeros((bp, MB_HEADS, MB_HEAD, MB_STATE), F32), p, bp, tp,
                                        TM_ROWS, MB_CHUNK)
            hs, cv_s, h_s = mamba_layer(hs, state_conv[j], state_ssm[j], p, bs, ts, TM_ROWS, math.gcd(ts, MB_CHUNK))
            out["ssm_p"].append(h_p); out["conv_p"].append(cv_p); out["ssm_s"].append(h_s); out["conv_s"].append(cv_s)
        w1, w2 = ffn_w1[i].astype(BF16), ffn_w2[i].astype(BF16)
        hp = ffn_res(hp, norm_ffn[i], w1, w2, TM_FFN, TH_FFN)
        hs = ffn_res(hs, norm_ffn[i], w1, w2, TM_FFN, TH_FFN)
    stack = lambda k: jnp.stack(out[k])
    return (hp.reshape(bp, tp, c), hs.reshape(bs, ts, c),
            stack("ckv_p"), stack("kpe_p"), stack("ckv_s"), stack("kpe_s"),
            stack("wkv_p"), stack("sh_p"), stack("wkv_s"), stack("sh_s"),
            stack("ssm_p"), stack("conv_p"), stack("ssm_s"), stack("conv_s"))
```

```python
import functools
import math

import jax
import jax.numpy as jnp
from jax import lax
from jax.experimental import pallas as pl
from jax.experimental.pallas import tpu as pltpu

F32 = jnp.float32
BF16 = jnp.bfloat16

D_MODEL = 1024
NORM_EPS = 1e-6

RW_HEAD = 64
RW_HEADS = D_MODEL // RW_HEAD
RW_LNX_EPS = 64e-5

MLA_HEADS = 16
MLA_Q_LORA = 512
MLA_KV_LORA = 256
MLA_NOPE = 64
MLA_ROPE = 32
MLA_V = 64
MLA_SCALE = 1.0 / math.sqrt(MLA_NOPE + MLA_ROPE)
ROPE_THETA = 10000.0
PAGE_SIZE = 128

MB_INNER = 2 * D_MODEL
MB_HEAD = 64
MB_HEADS = MB_INNER // MB_HEAD
MB_GROUPS = 4
MB_STATE = 128
MB_CONV = 4
MB_CONV_DIM = MB_INNER + 2 * MB_GROUPS * MB_STATE
MB_CHUNK = 128

FFN_HIDDEN = 4 * D_MODEL

VMEM_LIMIT_BYTES = 56 * 2**20


def _params(*sem):
    return pltpu.CompilerParams(dimension_semantics=sem, vmem_limit_bytes=VMEM_LIMIT_BYTES)


def _dot(a, b):
    return jnp.dot(a.astype(BF16), b.astype(BF16), preferred_element_type=F32)


def _dot_nt(a, b):
    return lax.dot_general(a.astype(BF16), b.astype(BF16), (((1,), (1,)), ((), ())),
                           preferred_element_type=F32)


def _dot_tn(a, b):
    return lax.dot_general(a.astype(BF16), b.astype(BF16), (((0,), (0,)), ((), ())),
                           preferred_element_type=F32)


def _split3(a):
    hi = a.astype(BF16)
    r1 = a - hi.astype(F32)
    mid = r1.astype(BF16)
    lo = (r1 - mid.astype(F32)).astype(BF16)
    return hi, mid, lo


def _dot_exact_lhs(ones, x):
    hi, mid, lo = _split3(x)
    o = ones.astype(BF16)
    return (jnp.dot(o, hi, preferred_element_type=F32)
            + jnp.dot(o, mid, preferred_element_type=F32)
            + jnp.dot(o, lo, preferred_element_type=F32))


def _dot_exact_rhs(x, ones):
    hi, mid, lo = _split3(x)
    o = ones.astype(BF16)
    return (jnp.dot(hi, o, preferred_element_type=F32)
            + jnp.dot(mid, o, preferred_element_type=F32)
            + jnp.dot(lo, o, preferred_element_type=F32))


def _rms(x, g):
    return x * lax.rsqrt(jnp.mean(x * x, axis=-1, keepdims=True) + NORM_EPS) * g


def _softplus(z):
    return jnp.maximum(z, 0.0) + jnp.log(1.0 + jnp.exp(-jnp.abs(z)))


def _sigmoid(z):
    return 0.5 * jnp.tanh(0.5 * z) + 0.5


def _silu(z):
    return z * _sigmoid(z)


def _rmsnorm_kernel(x_ref, g_ref, o_ref):
    o_ref[...] = _rms(x_ref[...], g_ref[...])


def rmsnorm_rows(x, g, tm):
    m, c = x.shape
    return pl.pallas_call(
        _rmsnorm_kernel,
        grid=(m // tm,),
        in_specs=[pl.BlockSpec((tm, c), lambda i: (i, 0)), pl.BlockSpec((1, c), lambda i: (0, 0))],
        out_specs=pl.BlockSpec((tm, c), lambda i: (i, 0)),
        out_shape=jax.ShapeDtypeStruct((m, c), F32),
        compiler_params=_params("parallel"),
        name="rmsnorm_rows",
    )(x, g.reshape(1, c))


def _norm_linear_kernel(x_ref, g_ref, w_ref, o_ref, xn_ref):
    @pl.when(pl.program_id(1) == 0)
    def _():
        xn_ref[...] = _rms(x_ref[...], g_ref[...]).astype(BF16)

    o_ref[...] = jnp.dot(xn_ref[...], w_ref[...], preferred_element_type=F32)


def norm_linear(x, g, w, tm, tn):
    m, k = x.shape
    n = w.shape[1]
    return pl.pallas_call(
        _norm_linear_kernel,
        grid=(m // tm, n // tn),
        in_specs=[pl.BlockSpec((tm, k), lambda i, j: (i, 0)),
                  pl.BlockSpec((1, k), lambda i, j: (0, 0)),
                  pl.BlockSpec((k, tn), lambda i, j: (0, j))],
        out_specs=pl.BlockSpec((tm, tn), lambda i, j: (i, j)),
        out_shape=jax.ShapeDtypeStruct((m, n), F32),
        scratch_shapes=[pltpu.VMEM((tm, k), BF16)],
        compiler_params=_params("parallel", "arbitrary"),
        name="norm_linear",
    )(x, g.reshape(1, k), w)


def _mixer_ffn_kernel(h_ref, x_ref, wo_ref, g_ref, w1_ref, w2_ref, o_ref, xn_ref):
    @pl.when(pl.program_id(1) == 0)
    def _():
        mixed = h_ref[...] + _dot(x_ref[...], wo_ref[...])
        o_ref[...] = mixed
        xn_ref[...] = _rms(mixed, g_ref[...]).astype(BF16)

    u = jnp.dot(xn_ref[...], w1_ref[...], preferred_element_type=F32)
    u = jnp.square(jnp.maximum(u, 0.0))
    o_ref[...] += jnp.dot(u.astype(BF16), w2_ref[...], preferred_element_type=F32)


def mixer_ffn_res(h, x, w_o, g, w1, w2, tm, th):
    m, c = h.shape
    k = x.shape[1]
    hid = w1.shape[1]
    return pl.pallas_call(
        _mixer_ffn_kernel,
        grid=(m // tm, hid // th),
        in_specs=[pl.BlockSpec((tm, c), lambda i, j: (i, 0)),
                  pl.BlockSpec((tm, k), lambda i, j: (i, 0)),
                  pl.BlockSpec((k, c), lambda i, j: (0, 0), pipeline_mode=pl.Buffered(1)),
                  pl.BlockSpec((1, c), lambda i, j: (0, 0)),
                  pl.BlockSpec((c, th), lambda i, j: (0, j)),
                  pl.BlockSpec((th, c), lambda i, j: (j, 0))],
        out_specs=pl.BlockSpec((tm, c), lambda i, j: (i, 0)),
        out_shape=jax.ShapeDtypeStruct((m, c), F32),
        scratch_shapes=[pltpu.VMEM((tm, c), BF16)],
        compiler_params=_params("parallel", "arbitrary"),
        name="mixer_ffn_res",
    )(h, x, w_o, g.reshape(1, c), w1, w2)


SHIFT_HALO = 8


def _rwkv_proj_kernel(has_vres, seq, tm, *refs):
    if has_vres:
        (h_ref, halo_ref, sh_ref, gmix_ref, mu_ref, wrkv_ref, w0_ref, w1_ref, w2_ref, a0_ref, a1_ref, a2_ref,
         g1_ref, g2_ref, v0_ref, v1_ref, v2_ref,
         r_ref, lw_ref, k_ref, v_ref, a_ref, g_ref, vg_ref) = refs
    else:
        (h_ref, halo_ref, sh_ref, gmix_ref, mu_ref, wrkv_ref, w0_ref, w1_ref, w2_ref, a0_ref, a1_ref, a2_ref,
         g1_ref, g2_ref,
         r_ref, lw_ref, k_ref, v_ref, a_ref, g_ref) = refs
    x = _rms(h_ref[...], gmix_ref[...])
    rolled = pltpu.roll(x, 1, axis=0)
    row = lax.broadcasted_iota(jnp.int32, x.shape, 0)
    if seq >= tm:
        is_start = (pl.program_id(0) % (seq // tm)) == 0
        tail = _rms(halo_ref[...], gmix_ref[...])[SHIFT_HALO - 1:SHIFT_HALO, :]
        xprev = jnp.where(row == 0, jnp.where(is_start, sh_ref[0], tail), rolled)
    else:
        n_seq = tm // seq
        starts = jnp.broadcast_to(sh_ref[...][:, None, :], (n_seq, seq, x.shape[1])).reshape(x.shape)
        xprev = jnp.where(row % seq == 0, starts, rolled)
    dx = xprev - x
    xm = [(x + dx * mu_ref[p:p + 1, :]).astype(BF16) for p in range(6)]
    r_ref[...] = jnp.dot(xm[0], wrkv_ref[0], preferred_element_type=F32)
    k_ref[...] = jnp.dot(xm[1], wrkv_ref[1], preferred_element_type=F32)
    v_ref[...] = jnp.dot(xm[2], wrkv_ref[2], preferred_element_type=F32)
    wpre = w0_ref[...] + _dot(jnp.tanh(jnp.dot(xm[3], w1_ref[...], preferred_element_type=F32)), w2_ref[...])
    w_log = -_softplus(-wpre) - 0.5
    lw_ref[...] = -jnp.exp(w_log)
    a_ref[...] = _sigmoid(a0_ref[...] + _dot(jnp.dot(xm[4], a1_ref[...], preferred_element_type=F32), a2_ref[...]))
    g_ref[...] = _dot(_sigmoid(jnp.dot(xm[5], g1_ref[...], preferred_element_type=F32)), g2_ref[...])
    if has_vres:
        vg_ref[...] = _sigmoid(v0_ref[...] + _dot(jnp.dot(xm[2], v1_ref[...], preferred_element_type=F32),
                                                  v2_ref[...]))


def rwkv_proj(h, shift, p, seq, tm):
    m, c = h.shape
    assert tm % SHIFT_HALO == 0 and (seq % tm == 0 or tm % seq == 0)
    has_vres = "v1" in p
    row = pl.BlockSpec((tm, c), lambda i: (i, 0))
    halo = pl.BlockSpec((SHIFT_HALO, c), lambda i: (jnp.maximum(i * (tm // SHIFT_HALO) - 1, 0), 0))
    if seq >= tm:
        shift_arg = shift.reshape(-1, 1, c)
        shift_spec = pl.BlockSpec((1, 1, c), lambda i: (i // (seq // tm), 0, 0))
    else:
        shift_arg = shift
        shift_spec = pl.BlockSpec((tm // seq, c), lambda i: (i, 0))
    full = lambda arr: pl.BlockSpec(arr.shape, lambda i: (0,) * arr.ndim, pipeline_mode=pl.Buffered(1))
    names = ["norm_mix", "mu", "w_rkv", "w0", "w1", "w2", "a0", "a1", "a2", "g1", "g2"]
    if has_vres:
        names += ["v0", "v1", "v2"]
    weights = [p[n] for n in names]
    n_out = 7 if has_vres else 6
    return pl.pallas_call(
        functools.partial(_rwkv_proj_kernel, has_vres, seq, tm),
        grid=(m // tm,),
        in_specs=[row, halo, shift_spec] + [full(w) for w in weights],
        out_specs=[row] * n_out,
        out_shape=[jax.ShapeDtypeStruct((m, c), F32)] * n_out,
        compiler_params=_params("parallel"),
        name="rwkv_proj",
    )(h, h, shift_arg, *weights)


PAIR = 2 * RW_HEAD


def _dot_hilo(a, b, dims=(((1,), (0,)), ((), ()))):
    ah = a.astype(BF16)
    al = (a - ah.astype(F32)).astype(BF16)
    bh = b.astype(BF16)
    bl = (b - bh.astype(F32)).astype(BF16)
    f = lambda x, y: lax.dot_general(x, y, dims, preferred_element_type=F32)
    return f(ah, bh) + f(ah, bl) + f(al, bh)


def _dot_tn_hilo(a, b):
    return _dot_hilo(a, b, (((0,), (0,)), ((), ())))


def _wkv_kernel(has_vres, npair, ln, *refs):
    if has_vres:
        (r_ref, lw_ref, k_ref, v_ref, a_ref, g_ref, vf_ref, vg_ref, s0_ref,
         kk_ref, ka_ref, rk_ref, lnw_ref, lnb_ref, y_ref, sout_ref, s_ref) = refs
    else:
        (r_ref, lw_ref, k_ref, v_ref, a_ref, g_ref, s0_ref,
         kk_ref, ka_ref, rk_ref, lnw_ref, lnb_ref, y_ref, sout_ref, s_ref) = refs
    c = pl.program_id(2)
    n = RW_HEAD
    rows = 2 * ln

    @pl.when(c == 0)
    def _():
        zero = jnp.zeros((n, n), F32)
        for p in range(npair):
            s_ref[p] = jnp.concatenate([jnp.concatenate([s0_ref[0, 2 * p], zero], axis=1),
                                        jnp.concatenate([zero, s0_ref[0, 2 * p + 1]], axis=1)], axis=0)

    first = lax.broadcasted_iota(jnp.int32, (ln, PAIR), 1) < n
    ri = lax.broadcasted_iota(jnp.int32, (rows, rows), 0)
    ci = lax.broadcasted_iota(jnp.int32, (rows, rows), 1)
    same = (ri // ln) == (ci // ln)
    incl = same & (ci <= ri)
    strict = same & (ci < ri)
    eye = jnp.where(ri == ci, 1.0, 0.0)
    ti = lax.broadcasted_iota(jnp.int32, (ln, ln), 0)
    tj = lax.broadcasted_iota(jnp.int32, (ln, ln), 1)
    tri = jnp.where(tj <= ti, 1.0, 0.0).astype(BF16)

    def seg_sum(x):
        s1 = jnp.sum(jnp.where(first, x, 0.0), axis=-1, keepdims=True)
        s2 = jnp.sum(jnp.where(first, 0.0, x), axis=-1, keepdims=True)
        return jnp.where(first, s1, s2)

    def by_head(x):
        return jnp.concatenate([jnp.where(first, x, 0.0), jnp.where(first, 0.0, x)], axis=0)

    pairs = range(npair)
    lanes = [slice(p * PAIR, (p + 1) * PAIR) for p in pairs]

    def prepare(p):
        r = r_ref[:, lanes[p]]
        lw = lw_ref[:, lanes[p]]
        k = k_ref[:, lanes[p]]
        v = v_ref[:, lanes[p]]
        a = a_ref[:, lanes[p]]
        if has_vres:
            v = v + (vf_ref[:, lanes[p]] - v) * vg_ref[:, lanes[p]]
        kk = k * kk_ref[:, lanes[p]]
        kk = kk / jnp.maximum(jnp.sqrt(seg_sum(kk * kk)), 1e-12)
        k2 = k * (1.0 + (a - 1.0) * ka_ref[:, lanes[p]])
        bv = kk * a
        hi, mid, lo = _split3(lw)
        cl3 = jnp.dot(tri, jnp.concatenate([hi, mid, lo], axis=1), preferred_element_type=F32)
        cl = cl3[:, :PAIR] + cl3[:, PAIR:2 * PAIR] + cl3[:, 2 * PAIR:]
        cl_end = cl[ln - 1:ln, :]
        g_inv = jnp.exp(-cl)
        g_end = jnp.exp(cl_end - cl)
        rt = r * jnp.exp(cl)
        at = -kk * jnp.exp(cl - lw)
        lhs = jnp.concatenate([by_head(at), by_head(rt)], axis=0)
        bh = bv * g_inv
        kh = k2 * g_inv
        return dict(r=r, v=v, k2=k2, lhs=lhs.astype(BF16),
                    rhs=jnp.concatenate([bh, bh, kh, kh], axis=0).astype(BF16),
                    decay=jnp.exp(cl_end), v_h=by_head(v),
                    ws=jnp.concatenate([by_head(bv * g_end), by_head(k2 * g_end)], axis=0))

    st = [prepare(p) for p in pairs]
    aa = [_dot_nt(st[p]["lhs"], st[p]["rhs"]) for p in pairs]
    a_ak = [jnp.where(strict, aa[p][:rows, rows:], 0.0) for p in pairs]
    a_r = [jnp.concatenate([jnp.where(incl, aa[p][rows:, :rows], 0.0),
                            jnp.where(incl, aa[p][rows:, rows:], 0.0)], axis=1).astype(BF16) for p in pairs]
    akv = [_dot_hilo(a_ak[p], st[p]["v_h"]) for p in pairs]
    def same_block(size):
        return (ri // size) == (ci // size)

    below = ci < ri
    tinv = [eye + jnp.where(below & same_block(2), aa[p][:rows, :rows], 0.0) for p in pairs]
    for level in range(1, int(math.log2(ln))):
        size = 2 ** level
        coupling = below & same_block(2 * size) & jnp.logical_not(same_block(size))
        n_off = [jnp.where(coupling, aa[p][:rows, :rows], 0.0).astype(BF16) for p in pairs]
        tinv_bf = [tinv[p].astype(BF16) for p in pairs]
        right = [_dot(n_off[p], tinv_bf[p]) for p in pairs]
        tinv = [tinv[p] + _dot(tinv_bf[p], right[p]) for p in pairs]

    s_prev = [s_ref[p] for p in pairs]
    xy = [_dot_nt(st[p]["lhs"], s_prev[p]) for p in pairs]
    u = [_dot_hilo(tinv[p], xy[p][:rows] + akv[p]) for p in pairs]
    zs = [jnp.concatenate([u[p], st[p]["v_h"]], axis=0) for p in pairs]
    for p in pairs:
        s_ref[p] = s_prev[p] * st[p]["decay"] + _dot_tn_hilo(zs[p], st[p]["ws"])
    y_h = [xy[p][rows:] + _dot(a_r[p], zs[p]) for p in pairs]
    for p in pairs:
        y = y_h[p][:ln] + y_h[p][ln:]
        mean = seg_sum(y) * (1.0 / n)
        yc = y - mean
        var = seg_sum(yc * yc) * (1.0 / n)
        y = yc * lax.rsqrt(var + RW_LNX_EPS) * lnw_ref[:, lanes[p]] + lnb_ref[:, lanes[p]]
        y = y + seg_sum(st[p]["r"] * st[p]["k2"] * rk_ref[:, lanes[p]]) * st[p]["v"]
        y_ref[:, lanes[p]] = y * g_ref[:, lanes[p]]

    @pl.when(c == pl.num_programs(2) - 1)
    def _():
        for p in range(npair):
            sout_ref[0, 2 * p] = s_ref[p, :n, :n]
            sout_ref[0, 2 * p + 1] = s_ref[p, n:, n:]


def wkv_scan(r, lw, k, v, a, g, s0, p, batch, seq, npair, ln, vres=None):
    nc = seq // ln
    has_vres = vres is not None
    width = npair * PAIR
    tok = pl.BlockSpec((ln, width), lambda b, q, c: (b * nc + c, q))
    par = pl.BlockSpec((1, width), lambda b, q, c: (0, q))
    st = pl.BlockSpec((1, 2 * npair, RW_HEAD, RW_HEAD), lambda b, q, c: (b, q, 0, 0))
    seqs = [r, lw, k, v, a, g] + (list(vres) if has_vres else [])
    pars = [p[nm].reshape(1, D_MODEL) for nm in ("k_k", "k_a", "r_k", "lnx_w", "lnx_b")]
    return pl.pallas_call(
        functools.partial(_wkv_kernel, has_vres, npair, ln),
        grid=(batch, D_MODEL // width, nc),
        in_specs=[tok] * len(seqs) + [st] + [par] * 5,
        out_specs=[tok, st],
        out_shape=[jax.ShapeDtypeStruct(r.shape, F32), jax.ShapeDtypeStruct(s0.shape, F32)],
        scratch_shapes=[pltpu.VMEM((npair, PAIR, PAIR), F32)],
        compiler_params=_params("parallel", "parallel", "arbitrary"),
        name="wkv_scan",
    )(*seqs, s0, *pars)


def prep_rwkv(w, j, i):
    row = lambda z: z.reshape(1, -1).astype(F32)
    p = {
        "norm_mix": row(w["norm_mix"][i]),
        "mu": w["rwkv_mu"][j],
        "w_rkv": w["rwkv_w_rkv"][j].astype(BF16),
        "w0": row(w["rwkv_w0"][j]), "w1": w["rwkv_w1"][j].astype(BF16), "w2": w["rwkv_w2"][j].astype(BF16),
        "a0": row(w["rwkv_a0"][j]), "a1": w["rwkv_a1"][j].astype(BF16), "a2": w["rwkv_a2"][j].astype(BF16),
        "g1": w["rwkv_g1"][j].astype(BF16), "g2": w["rwkv_g2"][j].astype(BF16),
        "k_k": w["rwkv_k_k"][j], "k_a": w["rwkv_k_a"][j], "r_k": w["rwkv_r_k"][j],
        "lnx_w": w["rwkv_lnx_w"][j], "lnx_b": w["rwkv_lnx_b"][j],
        "w_o": w["rwkv_w_o"][j].astype(BF16),
    }
    if j > 0:
        p["v0"] = row(w["rwkv_v0"][j - 1])
        p["v1"] = w["rwkv_v1"][j - 1].astype(BF16)
        p["v2"] = w["rwkv_v2"][j - 1].astype(BF16)
    return p


def rwkv_layer(h, shift, s0, vfirst, p, batch, seq, tm, npair, ln):
    m, c = h.shape
    outs = rwkv_proj(h, shift, p, seq, tm)
    r, lw, k, v, a, g = outs[:6]
    vres = None if vfirst is None else (vfirst, outs[6])
    y, s_new = wkv_scan(r, lw, k, v, a, g, s0, p, batch, seq, npair, ln, vres)
    shift_new = rmsnorm_rows(h.reshape(batch, seq, c)[:, -1], p["norm_mix"], batch)
    return y, s_new, shift_new, (v if vfirst is None else vfirst)


SEG_PAD = 128


def _dot_hilo_rhs(x, ones):
    hi = x.astype(BF16)
    lo = (x - hi.astype(F32)).astype(BF16)
    return jnp.dot(hi, ones, preferred_element_type=F32) + jnp.dot(lo, ones, preferred_element_type=F32)


def _seg_rms_scale(x, seg_ref, segt_ref, invw_ref):
    ssq = _dot(x * x, seg_ref[...])
    return _dot_hilo_rhs(lax.rsqrt(ssq * invw_ref[...] + NORM_EPS), segt_ref[...])


def _swap_halves(x, group):
    half = group // 2
    width = x.shape[-1]
    lane = lax.broadcasted_iota(jnp.int32, x.shape, x.ndim - 1)
    return jnp.where((lane % group) < half,
                     pltpu.roll(x, width - half, axis=x.ndim - 1),
                     pltpu.roll(x, half, axis=x.ndim - 1))


SLOT = 128


def _mla_proj_kernel(with_kv, *refs):
    (h_ref, cos_ref, sin_ref, gmix_ref, win_ref, gq_ref, gkv_ref, gkr_ref, wuq_ref, gqc_ref,
     seg_ref, segt_ref, invw_ref) = refs[:13]
    if with_kv:
        wuk_ref, wuv_ref, gkn_ref, qc_ref, c_ref, kp_ref, kc_ref, v_ref = refs[13:]
    else:
        qc_ref, c_ref, kp_ref = refs[13:]
    x = _rms(h_ref[...], gmix_ref[...])
    hp = _dot(x, win_ref[...])
    q_a = _rms(hp[:, :MLA_Q_LORA], gq_ref[...])
    c = _rms(hp[:, MLA_Q_LORA:MLA_Q_LORA + MLA_KV_LORA], gkv_ref[...])
    c_ref[...] = c

    cos = cos_ref[...]
    sin = sin_ref[...]

    def rope(z, cs, sn):
        return z * cs + _swap_halves(z, MLA_ROPE) * sn

    kp_raw = hp[:, MLA_Q_LORA + MLA_KV_LORA:]
    kp_scale = lax.rsqrt(jnp.sum(kp_raw * kp_raw, axis=-1, keepdims=True) * (1.0 / MLA_ROPE) + NORM_EPS)
    kp = rope(pltpu.roll(kp_raw, MLA_NOPE, axis=1) * gkr_ref[...], cos, sin) * kp_scale
    kp_ref[...] = kp[:, MLA_NOPE:MLA_NOPE + MLA_ROPE]

    q = _dot(q_a, wuq_ref[...])
    cos_h = jnp.tile(cos, (1, MLA_HEADS))
    sin_h = jnp.tile(sin, (1, MLA_HEADS))
    qc = rope(q * gqc_ref[...], cos_h, sin_h) * _seg_rms_scale(q, seg_ref, segt_ref, invw_ref)
    qc_ref[...] = qc.astype(BF16)
    if with_kv:
        kraw = _dot(c, wuk_ref[...])
        kn = kraw * _seg_rms_scale(kraw, seg_ref, segt_ref, invw_ref) * gkn_ref[...]
        kc_ref[...] = (kn + jnp.tile(kp, (1, MLA_HEADS))).astype(BF16)
        v_ref[0] = _dot_nt(wuv_ref[...], c).astype(BF16)


def _slot_seg_matrix():
    lane = jnp.arange(MLA_HEADS * SLOT)
    head, off = lane // SLOT, lane % SLOT
    col = jnp.where(off < MLA_NOPE, head, jnp.where(off < MLA_NOPE + MLA_ROPE, MLA_HEADS + head, SEG_PAD))
    return (col[:, None] == jnp.arange(SEG_PAD)[None, :]).astype(BF16)


def _slot(nope, rope):
    pad = jnp.zeros(nope.shape[:-1] + (SLOT - MLA_NOPE - MLA_ROPE,), nope.dtype)
    return jnp.concatenate([nope, rope, pad], axis=-1)


def _rope_tables(pos):
    half = MLA_ROPE // 2
    inv = ROPE_THETA ** (-jnp.arange(half, dtype=F32) / half)
    ang = pos.astype(F32)[:, None] * inv[None, :]
    cos = jnp.cos(ang)
    sin = jnp.sin(ang)
    ones = jnp.ones((pos.shape[0], MLA_NOPE), F32)
    return (_slot(ones, jnp.concatenate([cos, cos], axis=1)),
            _slot(jnp.zeros_like(ones), jnp.concatenate([-sin, sin], axis=1)))


def mla_proj(h, pos, p, tm, with_kv):
    m, c = h.shape
    seq = pos.shape[0]
    cos, sin = _rope_tables(pos)
    if seq >= tm:
        nrep = seq // tm
    else:
        cos, sin = jnp.tile(cos, (tm // seq, 1)), jnp.tile(sin, (tm // seq, 1))
        nrep = 1
    row = lambda width: pl.BlockSpec((tm, width), lambda i: (i, 0))
    tab = pl.BlockSpec((tm, SLOT), lambda i: (i % nrep, 0))
    full = lambda arr: pl.BlockSpec(arr.shape, lambda i: (0,) * arr.ndim)
    seg = _slot_seg_matrix()
    invw = jnp.concatenate([jnp.full((MLA_HEADS,), 1.0 / MLA_NOPE, F32), jnp.full((MLA_HEADS,), 1.0 / MLA_ROPE, F32),
                            jnp.ones((SEG_PAD - 2 * MLA_HEADS,), F32)]).reshape(1, SEG_PAD)
    weights = [p["norm_mix"], p["w_in"], p["q_norm"], p["kv_norm"], p["kr_norm"], p["w_uq"],
               p["q_gain"] if with_kv else p["q_gain_abs"], seg, seg.T, invw]
    wide = MLA_HEADS * SLOT
    out_shape = [jax.ShapeDtypeStruct((m, wide), BF16),
                 jax.ShapeDtypeStruct((m, MLA_KV_LORA), F32), jax.ShapeDtypeStruct((m, MLA_ROPE), F32)]
    out_specs = [row(wide), row(MLA_KV_LORA), row(MLA_ROPE)]
    if with_kv:
        weights += [p["w_uk"], p["w_uv_t"], p["kn_gain"]]
        out_shape += [jax.ShapeDtypeStruct((m, wide), BF16),
                      jax.ShapeDtypeStruct((m // tm, MLA_HEADS * MLA_V, tm), BF16)]
        out_specs += [row(wide), pl.BlockSpec((1, MLA_HEADS * MLA_V, tm), lambda i: (i, 0, 0))]
    return pl.pallas_call(
        functools.partial(_mla_proj_kernel, with_kv),
        grid=(m // tm,),
        in_specs=[row(c), tab, tab] + [full(w) for w in weights],
        out_specs=out_specs,
        out_shape=out_shape,
        compiler_params=_params("parallel"),
        name="mla_proj",
    )(h, cos, sin, *weights)


def prep_mla(w, j, i):
    row = lambda z: z.reshape(1, -1).astype(F32)
    w_in = w["mla_w_in"][j]
    pad = jnp.zeros((D_MODEL, 128 - MLA_ROPE), F32)
    w_uq = w["mla_w_uq"][j]
    w_uk = w["mla_w_uk"][j]
    zero_n = jnp.zeros((MLA_NOPE,), F32)
    zero_r = jnp.zeros((MLA_ROPE,), F32)
    qn, qr, kn = w["mla_qn_norm"][j], w["mla_qr_norm"][j], w["mla_kn_norm"][j]
    return {
        "norm_mix": row(w["norm_mix"][i]),
        "w_in": jnp.concatenate([w_in, pad], axis=1).astype(BF16),
        "q_norm": row(w["mla_q_norm"][j]), "kv_norm": row(w["mla_kv_norm"][j]),
        "kr_norm": row(_slot(zero_n, w["mla_kr_norm"][j])),
        "w_uq": _slot(w_uq[:, :, :MLA_NOPE], w_uq[:, :, MLA_NOPE:]).reshape(MLA_Q_LORA, -1).astype(BF16),
        "q_gain": row(jnp.tile(_slot(qn, qr) * MLA_SCALE, MLA_HEADS)),
        "q_gain_abs": row(jnp.tile(_slot(qn * kn, qr) * MLA_SCALE, MLA_HEADS)),
        "kn_gain": row(jnp.tile(_slot(kn, zero_r), MLA_HEADS)),
        "w_uk": _slot(w_uk, jnp.zeros(w_uk.shape[:2] + (MLA_ROPE,), F32)).reshape(MLA_KV_LORA, -1).astype(BF16),
        "w_uv": w["mla_w_uv"][j].reshape(MLA_KV_LORA, -1).astype(BF16),
        "w_uv_t": w["mla_w_uv"][j].reshape(MLA_KV_LORA, -1).T.astype(BF16),
        "w_uk_t": w["mla_w_uk"][j].reshape(MLA_KV_LORA, -1).T.astype(BF16),
        "w_uk_heads": jnp.transpose(w["mla_w_uk"][j], (1, 2, 0)).astype(BF16),
        "w_o": w["mla_w_o"][j].astype(BF16),
    }


ATTN_HEADS_PER_STEP = 8
NEG_BIG = -1e30


def _flash_kernel(tq, qc_ref, kc_ref, vt_ref, o_ref, m_ref, l_ref, acc_ref):
    qi = pl.program_id(2)
    g = ATTN_HEADS_PER_STEP
    heads = range(g)
    key_i = lax.broadcasted_iota(jnp.int32, (tq, tq), 0)
    qry_i = lax.broadcasted_iota(jnp.int32, (tq, tq), 1)
    m_ref[...] = jnp.full_like(m_ref, NEG_BIG)
    l_ref[...] = jnp.zeros_like(l_ref)
    acc_ref[...] = jnp.zeros_like(acc_ref)

    def block(j, masked):
        keys = pl.ds(pl.multiple_of(j * tq, tq), tq)
        s = [_dot_nt(kc_ref[0, keys, hh * SLOT:(hh + 1) * SLOT], qc_ref[0, :, hh * SLOT:(hh + 1) * SLOT])
             for hh in heads]
        if masked:
            s = [jnp.where(key_i <= qry_i, s[hh], NEG_BIG) for hh in heads]
        m_prev = [m_ref[hh] for hh in heads]
        m_new = [jnp.maximum(m_prev[hh], jnp.max(s[hh], axis=0, keepdims=True)) for hh in heads]
        alpha = [jnp.exp(m_prev[hh] - m_new[hh]) for hh in heads]
        pr = [jnp.exp(s[hh] - m_new[hh]) for hh in heads]
        pv = [jnp.dot(vt_ref[j, hh * MLA_V:(hh + 1) * MLA_V, :], pr[hh].astype(BF16), preferred_element_type=F32)
              for hh in heads]
        for hh in heads:
            l_ref[hh] = alpha[hh] * l_ref[hh] + jnp.sum(pr[hh], axis=0, keepdims=True)
            acc_ref[hh] = alpha[hh] * acc_ref[hh] + pv[hh]
            m_ref[hh] = m_new[hh]

    def body(j, carry):
        block(j, False)
        return carry

    lax.fori_loop(0, qi, body, 0)
    block(qi, True)
    o_t = jnp.concatenate([acc_ref[hh] / l_ref[hh] for hh in heads], axis=0)
    o_ref[0] = o_t.T


def flash_prompt(qc, kc, vt, batch, seq, tq):
    g = ATTN_HEADS_PER_STEP
    nq = seq // tq
    r3 = lambda z: z.reshape(batch, seq, -1)
    return pl.pallas_call(
        functools.partial(_flash_kernel, tq),
        grid=(batch, MLA_HEADS // g, nq),
        in_specs=[pl.BlockSpec((1, tq, g * SLOT), lambda b, hq, i: (b, i, hq)),
                  pl.BlockSpec((1, seq, g * SLOT), lambda b, hq, i: (b, 0, hq)),
                  pl.BlockSpec((nq, g * MLA_V, tq), lambda b, hq, i: (b, hq, 0))],
        out_specs=pl.BlockSpec((1, tq, g * MLA_V), lambda b, hq, i: (b, i, hq)),
        out_shape=jax.ShapeDtypeStruct((batch, seq, MLA_HEADS * MLA_V), F32),
        scratch_shapes=[pltpu.VMEM((g, 1, tq), F32), pltpu.VMEM((g, 1, tq), F32),
                        pltpu.VMEM((g, MLA_V, tq), F32)],
        compiler_params=_params("parallel", "parallel", "parallel"),
        name="flash_prompt",
    )(r3(qc), r3(kc), vt).reshape(batch * seq, -1)


def _bmm_kernel(a_ref, b_ref, o_ref):
    o_ref[0] = _dot(a_ref[0], b_ref[0]).astype(o_ref.dtype)


def bmm(a, b, out_dtype):
    g, m, k = a.shape
    n = b.shape[2]
    return pl.pallas_call(
        _bmm_kernel,
        grid=(g,),
        in_specs=[pl.BlockSpec((1, m, k), lambda i: (i, 0, 0)), pl.BlockSpec((1, k, n), lambda i: (i, 0, 0))],
        out_specs=pl.BlockSpec((1, m, n), lambda i: (i, 0, 0)),
        out_shape=jax.ShapeDtypeStruct((g, m, n), out_dtype),
        compiler_params=_params("parallel"),
        name="bmm",
    )(a, b)


def _paged_attn_kernel(pp, seq, n_pages, pt_ref, qa_ref, qp_ref, cn_ref, kpn_ref, wukt_ref, wuv_ref,
                       pool_c_ref, pool_kp_ref, o_ref, m_ref, l_ref, acc_ref, lhs_ref, c_buf, kp_buf, sems):
    b = pl.program_id(0)
    n_groups = n_pages // pp
    nq = seq * MLA_HEADS
    nup = MLA_HEADS * MLA_NOPE

    def group_copies(bb, g, slot):
        out = []
        for i in range(pp):
            page = pt_ref[bb * n_pages + g * pp + i]
            out.append(pltpu.make_async_copy(pool_c_ref.at[page], c_buf.at[slot, i], sems.at[0, slot]))
            out.append(pltpu.make_async_copy(pool_kp_ref.at[page], kp_buf.at[slot, i], sems.at[1, slot]))
        return out

    @pl.when(b == 0)
    def _():
        for cp in group_copies(b, 0, 0):
            cp.start()

    m_ref[...] = jnp.full_like(m_ref, NEG_BIG)
    l_ref[...] = jnp.zeros_like(l_ref)
    acc_ref[...] = jnp.zeros_like(acc_ref)
    lhs_ref[:nup, :] = wukt_ref[...]
    lhs_ref[nup:, :] = qa_ref[0]

    qp = qp_ref[0]

    def attend(c_blks, kp_t_blks, mask):
        subs = range(len(c_blks))
        nk = c_blks[0].shape[0]
        c_bf = [c_blks[i].astype(BF16) for i in subs]
        both = [_dot_nt(lhs_ref[...], c_bf[i]) for i in subs]
        ssq = [jnp.sum(jnp.square(both[i][:nup]).reshape(MLA_HEADS, MLA_NOPE, nk), axis=1) for i in subs]
        rs = [lax.rsqrt(ssq[i] * (1.0 / MLA_NOPE) + NORM_EPS) for i in subs]
        s = [both[i][nup:] * jnp.concatenate([rs[i]] * seq, axis=0) + _dot(qp, kp_t_blks[i]) for i in subs]
        if mask is not None:
            s = [jnp.where(mask, s[i], NEG_BIG) for i in subs]
        m_prev = m_ref[...]
        m_new = m_prev
        for i in subs:
            m_new = jnp.maximum(m_new, jnp.max(s[i], axis=-1, keepdims=True))
        alpha = jnp.exp(m_prev - m_new)
        pr = [jnp.exp(s[i] - m_new) for i in subs]
        l_new = alpha * l_ref[...]
        acc = alpha * acc_ref[...]
        for i in subs:
            l_new = l_new + jnp.sum(pr[i], axis=-1, keepdims=True)
            acc = acc + _dot(pr[i], c_bf[i])
        l_ref[...] = l_new
        acc_ref[...] = acc
        m_ref[...] = m_new

    pages_per_sub = 2

    def group_step(g, carry):
        slot = (b * n_groups + g) % 2
        last = g == n_groups - 1
        next_b = jnp.where(last, b + 1, b)
        next_g = jnp.where(last, 0, g + 1)

        @pl.when(next_b < pl.num_programs(0))
        def _():
            for cp in group_copies(next_b, next_g, 1 - slot):
                cp.start()

        for cp in group_copies(b, g, slot):
            cp.wait()
        subs = range(0, pp, pages_per_sub)
        attend([c_buf[slot, pl.ds(i, pages_per_sub)].reshape(pages_per_sub * PAGE_SIZE, MLA_KV_LORA) for i in subs],
               [jnp.concatenate([kp_buf[slot, i + k] for k in range(pages_per_sub)], axis=1) for i in subs], None)
        return carry

    lax.fori_loop(0, n_groups, group_step, 0)

    qtok = lax.broadcasted_iota(jnp.int32, (nq, seq), 0) // MLA_HEADS
    ktok = lax.broadcasted_iota(jnp.int32, (nq, seq), 1)
    attend([cn_ref[0]], [kpn_ref[0]], ktok <= qtok)
    o_lat = acc_ref[...] / l_ref[...]
    full = _dot(o_lat, wuv_ref[...])
    rhead = lax.broadcasted_iota(jnp.int32, full.shape, 0) % MLA_HEADS
    lhead = lax.broadcasted_iota(jnp.int32, full.shape, 1) // MLA_V
    full = jnp.where(rhead == lhead, full, 0.0)
    o_ref[0] = jnp.sum(full.reshape(seq, MLA_HEADS, MLA_HEADS * MLA_V), axis=1)


def paged_attn(q_abs, qp, c_new, kp_new, pool_c, pool_kp_t, page_table, p, batch, seq, pp):
    n_pages = page_table.shape[1]
    assert n_pages % pp == 0 and pp % 2 == 0
    nq = seq * MLA_HEADS
    per_b = lambda shp: pl.BlockSpec((1,) + shp, lambda b, pt: (b, 0, 0))
    full = lambda arr: pl.BlockSpec(arr.shape, lambda b, pt: (0,) * arr.ndim)
    in_hbm = pl.BlockSpec(memory_space=pl.ANY)
    grid_spec = pltpu.PrefetchScalarGridSpec(
        num_scalar_prefetch=1,
        grid=(batch,),
        in_specs=[per_b((nq, MLA_KV_LORA)), per_b((nq, MLA_ROPE)), per_b((seq, MLA_KV_LORA)),
                  per_b((MLA_ROPE, seq)), full(p["w_uk_t"]), full(p["w_uv"]), in_hbm, in_hbm],
        out_specs=per_b((seq, MLA_HEADS * MLA_V)),
        scratch_shapes=[pltpu.VMEM((nq, 1), F32), pltpu.VMEM((nq, 1), F32), pltpu.VMEM((nq, MLA_KV_LORA), F32),
                        pltpu.VMEM((MLA_HEADS * MLA_NOPE + nq, MLA_KV_LORA), BF16),
                        pltpu.VMEM((2, pp, PAGE_SIZE, MLA_KV_LORA), F32),
                        pltpu.VMEM((2, pp, MLA_ROPE, PAGE_SIZE), F32),
                        pltpu.SemaphoreType.DMA((2, 2))],
    )
    return pl.pallas_call(
        functools.partial(_paged_attn_kernel, pp, seq, n_pages),
        grid_spec=grid_spec,
        out_shape=jax.ShapeDtypeStruct((batch, seq, MLA_HEADS * MLA_V), F32),
        compiler_params=_params("arbitrary"),
        name="paged_attn",
    )(page_table.reshape(-1), q_abs, qp, c_new.reshape(batch, seq, -1),
      jnp.swapaxes(kp_new.reshape(batch, seq, -1), 1, 2),
      p["w_uk_t"], p["w_uv"], pool_c, pool_kp_t)


def mla_layer_prompt(h, p, batch, seq, tm, tq):
    assert tm == tq, "the value tiles written by mla_proj are the key blocks of flash_prompt"
    qc, c, kp, kc, vt = mla_proj(h, jnp.arange(seq), p, tm, True)
    return flash_prompt(qc, kc, vt, batch, seq, tq), c, kp


def mla_layer_sample(h, pool_c, pool_kp, page_table, p, batch, seq, past_len, tm, pp):
    m = batch * seq
    qc, c, kp = mla_proj(h, past_len + jnp.arange(seq), p, tm, False)
    qc = qc.reshape(m, MLA_HEADS, SLOT)
    q_heads = jnp.swapaxes(qc[:, :, :MLA_NOPE], 0, 1)
    q_abs = bmm(q_heads, p["w_uk_heads"], BF16)
    q_abs = jnp.swapaxes(q_abs, 0, 1).reshape(batch, seq * MLA_HEADS, MLA_KV_LORA)
    qp = qc[:, :, MLA_NOPE:MLA_NOPE + MLA_ROPE].reshape(batch, seq * MLA_HEADS, MLA_ROPE)
    o = paged_attn(q_abs, qp, c, kp, pool_c, jnp.swapaxes(pool_kp, 1, 2), page_table, p, batch, seq, pp)
    return o.reshape(m, -1), c, kp


DT_PAD = 128
CONV_HALO = 8
MB_GN = MB_GROUPS * MB_STATE
MB_GROUP_INNER = MB_INNER // MB_GROUPS
MB_HEADS_PER_GROUP = MB_HEADS // MB_GROUPS


def _ssd_kernel(ck, xbc_ref, prev_ref, cs_ref, z_ref, dtr_ref, h0_ref, cw_ref, cb_ref, dtb_ref, alog_ref,
                dskip_ref, nw_ref, expand_ref, y_ref, hout_ref, h_ref):
    c = pl.program_id(1)

    @pl.when(c == 0)
    def _():
        h_ref[...] = h0_ref[0]

    halo = jnp.where(c == 0, cs_ref[0], prev_ref[0])
    xext = jnp.concatenate([halo, xbc_ref[0]], axis=0)
    conv = cb_ref[...] + xext[CONV_HALO:, :] * cw_ref[MB_CONV - 1:MB_CONV, :]
    for j in range(MB_CONV - 1):
        shifted = pltpu.roll(xext, MB_CONV - 1 - j, axis=0)[CONV_HALO:, :]
        conv = conv + shifted * cw_ref[j:j + 1, :]
    xbc = _silu(conv)
    xs = xbc[:, :MB_INNER]
    bm = xbc[:, MB_INNER:MB_INNER + MB_GN]
    cm = xbc[:, MB_INNER + MB_GN:]

    dt = _softplus(dtr_ref[0] + dtb_ref[...])
    da = dt * (-jnp.exp(alog_ref[...]))
    ri = lax.broadcasted_iota(jnp.int32, (ck, ck), 0)
    ci = lax.broadcasted_iota(jnp.int32, (ck, ck), 1)
    causal = ci <= ri
    tri = jnp.where(causal, 1.0, 0.0).astype(BF16)
    hi, mid, lo = _split3(da)
    acum = (jnp.dot(tri, hi, preferred_element_type=F32) + jnp.dot(tri, mid, preferred_element_type=F32)
            + jnp.dot(tri, lo, preferred_element_type=F32))
    tn = (((0,), (0,)), ((), ()))
    tri_t = jnp.where(ri <= ci, 1.0, 0.0).astype(BF16)
    acum_t = (lax.dot_general(hi, tri_t, tn, preferred_element_type=F32)
              + lax.dot_general(mid, tri_t, tn, preferred_element_type=F32)
              + lax.dot_general(lo, tri_t, tn, preferred_element_type=F32))
    e_last = jnp.exp(acum[ck - 1:ck, :])

    spread = expand_ref[...]
    dt_hi = dt.astype(BF16)
    dt_lo = (dt - dt_hi.astype(F32)).astype(BF16)
    dt_x = jnp.dot(dt_hi, spread, preferred_element_type=F32) + jnp.dot(dt_lo, spread, preferred_element_type=F32)
    ah, am, al = _split3(acum)
    acum_x = (jnp.dot(ah, spread, preferred_element_type=F32) + jnp.dot(am, spread, preferred_element_type=F32)
              + jnp.dot(al, spread, preferred_element_type=F32))
    xdt = xs * dt_x
    xdt_end = xdt * jnp.exp(acum_x[ck - 1:ck, :] - acum_x)
    e_cum_x = jnp.exp(acum_x)

    pair_w = 2 * MB_HEAD
    first = lax.broadcasted_iota(jnp.int32, (ck, pair_w), 1) < MB_HEAD
    upper = lax.broadcasted_iota(jnp.int32, (pair_w, MB_STATE), 0) < MB_HEAD
    pairs_per_group = MB_HEADS_PER_GROUP // 2
    ys = []
    for g in range(MB_GROUPS):
        b_g = bm[:, g * MB_STATE:(g + 1) * MB_STATE]
        c_g = cm[:, g * MB_STATE:(g + 1) * MB_STATE]
        cb = _dot_nt(c_g, b_g)
        cols = slice(g * MB_GROUP_INNER, (g + 1) * MB_GROUP_INNER)
        y_state = _dot_nt(c_g, h_ref[cols, :]) * e_cum_x[:, cols]
        lmat = []
        for hh in range(MB_HEADS_PER_GROUP):
            hd = g * MB_HEADS_PER_GROUP + hh
            seg = acum[:, hd:hd + 1] - acum_t[hd:hd + 1, :]
            lmat.append(cb * jnp.where(causal, jnp.exp(jnp.where(causal, seg, 0.0)), 0.0))
        for pr in range(pairs_per_group):
            lanes = slice(g * MB_GROUP_INNER + pr * pair_w, g * MB_GROUP_INNER + (pr + 1) * pair_w)
            x_p = xdt[:, lanes]
            y_p = jnp.where(first, _dot(lmat[2 * pr], x_p), _dot(lmat[2 * pr + 1], x_p))
            ys.append(y_p + y_state[:, pr * pair_w:(pr + 1) * pair_w])
            hd = g * MB_HEADS_PER_GROUP + 2 * pr
            decay = jnp.where(upper, e_last[:, hd:hd + 1], e_last[:, hd + 1:hd + 2])
            h_ref[lanes, :] = h_ref[lanes, :] * decay + _dot_tn(xdt_end[:, lanes], b_g)

    y = jnp.concatenate(ys, axis=1) + dskip_ref[...] * xs
    yz = y * _silu(z_ref[0])
    outs = []
    for g in range(MB_GROUPS):
        yg = yz[:, g * MB_GROUP_INNER:(g + 1) * MB_GROUP_INNER]
        outs.append(yg * lax.rsqrt(jnp.mean(yg * yg, axis=-1, keepdims=True) + NORM_EPS))
    y_ref[0] = (jnp.concatenate(outs, axis=1) * nw_ref[...]).astype(y_ref.dtype)

    @pl.when(c == pl.num_programs(1) - 1)
    def _():
        hout_ref[0] = h_ref[...]


def ssd_scan(xbc, z, dt_raw, conv_state, h0, p, batch, seq, ck):
    assert CONV_HALO % 8 == 0 and ck % CONV_HALO == 0
    nc = seq // ck
    halo_blocks = ck // CONV_HALO
    cs8 = jnp.concatenate([jnp.zeros((batch, CONV_HALO - (MB_CONV - 1), MB_CONV_DIM), F32), conv_state], axis=1)
    chunk = lambda width: pl.BlockSpec((1, ck, width), lambda b, c: (b, c, 0))
    full = lambda arr: pl.BlockSpec(arr.shape, lambda b, c: (0,) * arr.ndim)
    expand = (jnp.arange(DT_PAD)[:, None] == jnp.arange(MB_INNER)[None, :] // MB_HEAD).astype(BF16)
    weights = [p["conv_w"], p["conv_b"], p["dt_bias"], p["a_log"], p["d_skip"], p["norm_w"], expand]
    state = pl.BlockSpec((1, MB_INNER, MB_STATE), lambda b, c: (b, 0, 0))
    return pl.pallas_call(
        functools.partial(_ssd_kernel, ck),
        grid=(batch, nc),
        in_specs=[chunk(MB_CONV_DIM),
                  pl.BlockSpec((1, CONV_HALO, MB_CONV_DIM),
                               lambda b, c: (b, jnp.maximum(c * halo_blocks - 1, 0), 0)),
                  pl.BlockSpec((1, CONV_HALO, MB_CONV_DIM), lambda b, c: (b, 0, 0)),
                  chunk(MB_INNER), chunk(DT_PAD), state] + [full(w) for w in weights],
        out_specs=[chunk(MB_INNER), state],
        out_shape=[jax.ShapeDtypeStruct((batch, seq, MB_INNER), BF16),
                   jax.ShapeDtypeStruct((batch, MB_INNER, MB_STATE), F32)],
        scratch_shapes=[pltpu.VMEM((MB_INNER, MB_STATE), F32)],
        compiler_params=_params("parallel", "arbitrary"),
        name="ssd_scan",
    )(xbc, xbc, cs8, z, dt_raw, h0, *weights)


def prep_mamba(w, j, i):
    row = lambda z: z.reshape(1, -1).astype(F32)
    w_in = w["mamba_w_in"][j]
    padv = lambda z: jnp.concatenate([z.astype(F32), jnp.zeros((DT_PAD - MB_HEADS,), F32)])
    return {
        "norm_mix": w["norm_mix"][i],
        "w_z": w_in[:, :MB_INNER].astype(BF16),
        "w_xbc": w_in[:, MB_INNER:MB_INNER + MB_CONV_DIM].astype(BF16),
        "w_dt": jnp.concatenate([w_in[:, MB_INNER + MB_CONV_DIM:], jnp.zeros((D_MODEL, DT_PAD - MB_HEADS), F32)],
                                axis=1).astype(BF16),
        "conv_w": w["mamba_conv_w"][j], "conv_b": row(w["mamba_conv_b"][j]),
        "dt_bias": row(padv(w["mamba_dt_bias"][j])), "a_log": row(padv(w["mamba_a_log"][j])),
        "d_skip": row(jnp.repeat(w["mamba_d"][j].astype(F32), MB_HEAD)),
        "norm_w": row(w["mamba_norm"][j]),
        "w_o": w["mamba_w_o"][j].astype(BF16),
    }


def mamba_layer(h, conv_state, h0, p, batch, seq, ck):
    m = batch * seq
    assert seq >= MB_CONV - 1
    tm_in = min(TM_FFN, m)
    z = norm_linear(h, p["norm_mix"], p["w_z"], tm_in, 1024)
    xbc = norm_linear(h, p["norm_mix"], p["w_xbc"], tm_in, 1024)
    dt_raw = norm_linear(h, p["norm_mix"], p["w_dt"], tm_in, DT_PAD)
    xbc3 = xbc.reshape(batch, seq, MB_CONV_DIM)
    y, h_new = ssd_scan(xbc3, z.reshape(batch, seq, MB_INNER), dt_raw.reshape(batch, seq, DT_PAD),
                        conv_state, h0.reshape(batch, MB_INNER, MB_STATE), p, batch, seq, ck)
    return (y.reshape(m, MB_INNER), xbc3[:, seq - (MB_CONV - 1):],
            h_new.reshape(batch, MB_HEADS, MB_HEAD, MB_STATE))


N_MIXERS = 3
TM_PROJ = 256
TM_ROWS = 512
TM_FFN = 1024
TH_FFN = 1024
WKV_CHUNK = 64
WKV_PAIRS_PER_STEP = 8
ATTN_TQ = 256
PAGES_PER_STEP = 32


def kernel(x_prompt, x_sample, cache_mla_ckv, cache_mla_kpe, state_rwkv_wkv, state_rwkv_shift, state_ssm, state_conv, page_table, norm_mix, norm_ffn, ffn_w1, ffn_w2, rwkv_mu, rwkv_w_rkv, rwkv_w0, rwkv_w1, rwkv_w2, rwkv_a0, rwkv_a1, rwkv_a2, rwkv_v0, rwkv_v1, rwkv_v2, rwkv_g1, rwkv_g2, rwkv_k_k, rwkv_k_a, rwkv_r_k, rwkv_lnx_w, rwkv_lnx_b, rwkv_w_o, mla_w_in, mla_q_norm, mla_kv_norm, mla_w_uq, mla_w_uk, mla_w_uv, mla_qn_norm, mla_qr_norm, mla_kn_norm, mla_kr_norm, mla_w_o, mamba_w_in, mamba_conv_w, mamba_conv_b, mamba_dt_bias, mamba_a_log, mamba_d, mamba_norm, mamba_w_o):
    w = dict(locals())
    bp, tp, c = x_prompt.shape
    bs, ts, _ = x_sample.shape
    depth = norm_mix.shape[0]
    past_len = page_table.shape[1] * PAGE_SIZE
    hp = x_prompt.reshape(bp * tp, c)
    hs = x_sample.reshape(bs * ts, c)
    vf_p = vf_s = None
    out = {k: [] for k in ("ckv_p", "kpe_p", "ckv_s", "kpe_s", "wkv_p", "sh_p", "wkv_s", "sh_s",
                           "ssm_p", "conv_p", "ssm_s", "conv_s")}
    for i in range(depth):
        kind, j = i % N_MIXERS, i // N_MIXERS
        if kind == 0:
            p = prep_rwkv(w, j, i)
            xp, s_p, l_p, vf_p = rwkv_layer(hp, jnp.zeros((bp, c), F32), jnp.zeros((bp, RW_HEADS, RW_HEAD, RW_HEAD), F32),
                                            vf_p, p, bp, tp, TM_ROWS, WKV_PAIRS_PER_STEP, WKV_CHUNK)
            xs, s_s, l_s, vf_s = rwkv_layer(hs, state_rwkv_shift[j], state_rwkv_wkv[j], vf_s, p, bs, ts,
                                            TM_ROWS, WKV_PAIRS_PER_STEP, ts)
            out["wkv_p"].append(s_p); out["sh_p"].append(l_p); out["wkv_s"].append(s_s); out["sh_s"].append(l_s)
        elif kind == 1:
            p = prep_mla(w, j, i)
            xp, c_p, k_p = mla_layer_prompt(hp, p, bp, tp, TM_PROJ, ATTN_TQ)
            xs, c_s, k_s = mla_layer_sample(hs, cache_mla_ckv[j], cache_mla_kpe[j], page_table, p, bs, ts,
                                            past_len, TM_PROJ, PAGES_PER_STEP)
            out["ckv_p"].append(c_p.reshape(bp, tp, -1)); out["kpe_p"].append(k_p.reshape(bp, tp, -1))
            out["ckv_s"].append(c_s.reshape(bs, ts, -1)); out["kpe_s"].append(k_s.reshape(bs, ts, -1))
        else:
            p = prep_mamba(w, j, i)
            xp, cv_p, h_p = mamba_layer(hp, jnp.zeros((bp, MB_CONV - 1, MB_CONV_DIM), F32),
                                        jnp.zeros((bp, MB_HEADS, MB_HEAD, MB_STATE), F32), p, bp, tp, MB_CHUNK)
            xs, cv_s, h_s = mamba_layer(hs, state_conv[j], state_ssm[j], p, bs, ts, math.gcd(ts, MB_CHUNK))
            out["ssm_p"].append(h_p); out["conv_p"].append(cv_p); out["ssm_s"].append(h_s); out["conv_s"].append(cv_s)
        w1, w2 = ffn_w1[i].astype(BF16), ffn_w2[i].astype(BF16)
        hp = mixer_ffn_res(hp, xp, p["w_o"], norm_ffn[i], w1, w2, TM_FFN, TH_FFN)
        hs = mixer_ffn_res(hs, xs, p["w_o"], norm_ffn[i], w1, w2, TM_FFN, TH_FFN)
    stack = lambda k: jnp.stack(out[k])
    return (hp.reshape(bp, tp, c), hs.reshape(bs, ts, c),
            stack("ckv_p"), stack("kpe_p"), stack("ckv_s"), stack("kpe_s"),
            stack("wkv_p"), stack("sh_p"), stack("wkv_s"), stack("sh_s"),
            stack("ssm_p"), stack("conv_p"), stack("ssm_s"), stack("conv_s"))
```

```python
import functools
import math

import jax
import jax.numpy as jnp
from jax import lax
from jax.experimental import pallas as pl
from jax.experimental.pallas import tpu as pltpu

F32 = jnp.float32
BF16 = jnp.bfloat16

D_MODEL = 1024
NORM_EPS = 1e-6

RW_HEAD = 64
RW_HEADS = D_MODEL // RW_HEAD
RW_LNX_EPS = 64e-5

MLA_HEADS = 16
MLA_Q_LORA = 512
MLA_KV_LORA = 256
MLA_NOPE = 64
MLA_ROPE = 32
MLA_V = 64
MLA_SCALE = 1.0 / math.sqrt(MLA_NOPE + MLA_ROPE)
ROPE_THETA = 10000.0
PAGE_SIZE = 128

MB_INNER = 2 * D_MODEL
MB_HEAD = 64
MB_HEADS = MB_INNER // MB_HEAD
MB_GROUPS = 4
MB_STATE = 128
MB_CONV = 4
MB_CONV_DIM = MB_INNER + 2 * MB_GROUPS * MB_STATE
MB_CHUNK = 128

FFN_HIDDEN = 4 * D_MODEL

VMEM_LIMIT_BYTES = 56 * 2**20


def _params(*sem):
    return pltpu.CompilerParams(dimension_semantics=sem, vmem_limit_bytes=VMEM_LIMIT_BYTES)


def _dot(a, b):
    return jnp.dot(a.astype(BF16), b.astype(BF16), preferred_element_type=F32)


def _dot_nt(a, b):
    return lax.dot_general(a.astype(BF16), b.astype(BF16), (((1,), (1,)), ((), ())),
                           preferred_element_type=F32)


def _dot_tn(a, b):
    return lax.dot_general(a.astype(BF16), b.astype(BF16), (((0,), (0,)), ((), ())),
                           preferred_element_type=F32)


def _split3(a):
    hi = a.astype(BF16)
    r1 = a - hi.astype(F32)
    mid = r1.astype(BF16)
    lo = (r1 - mid.astype(F32)).astype(BF16)
    return hi, mid, lo


def _dot_exact_lhs(ones, x):
    hi, mid, lo = _split3(x)
    o = ones.astype(BF16)
    return (jnp.dot(o, hi, preferred_element_type=F32)
            + jnp.dot(o, mid, preferred_element_type=F32)
            + jnp.dot(o, lo, preferred_element_type=F32))


def _dot_exact_rhs(x, ones):
    hi, mid, lo = _split3(x)
    o = ones.astype(BF16)
    return (jnp.dot(hi, o, preferred_element_type=F32)
            + jnp.dot(mid, o, preferred_element_type=F32)
            + jnp.dot(lo, o, preferred_element_type=F32))


def _rms(x, g):
    return x * lax.rsqrt(jnp.mean(x * x, axis=-1, keepdims=True) + NORM_EPS) * g


def _softplus(z):
    return jnp.maximum(z, 0.0) + jnp.log(1.0 + jnp.exp(-jnp.abs(z)))


def _sigmoid(z):
    return 0.5 * jnp.tanh(0.5 * z) + 0.5


def _silu(z):
    return z * _sigmoid(z)


def _rmsnorm_kernel(x_ref, g_ref, o_ref):
    o_ref[...] = _rms(x_ref[...], g_ref[...])


def rmsnorm_rows(x, g, tm):
    m, c = x.shape
    return pl.pallas_call(
        _rmsnorm_kernel,
        grid=(m // tm,),
        in_specs=[pl.BlockSpec((tm, c), lambda i: (i, 0)), pl.BlockSpec((1, c), lambda i: (0, 0))],
        out_specs=pl.BlockSpec((tm, c), lambda i: (i, 0)),
        out_shape=jax.ShapeDtypeStruct((m, c), F32),
        compiler_params=_params("parallel"),
        name="rmsnorm_rows",
    )(x, g.reshape(1, c))


def _norm_linear_kernel(x_ref, g_ref, w_ref, o_ref, xn_ref):
    @pl.when(pl.program_id(1) == 0)
    def _():
        xn_ref[...] = _rms(x_ref[...], g_ref[...]).astype(BF16)

    o_ref[...] = jnp.dot(xn_ref[...], w_ref[...], preferred_element_type=F32)


def norm_linear(x, g, w, tm, tn):
    m, k = x.shape
    n = w.shape[1]
    return pl.pallas_call(
        _norm_linear_kernel,
        grid=(m // tm, n // tn),
        in_specs=[pl.BlockSpec((tm, k), lambda i, j: (i, 0)),
                  pl.BlockSpec((1, k), lambda i, j: (0, 0)),
                  pl.BlockSpec((k, tn), lambda i, j: (0, j))],
        out_specs=pl.BlockSpec((tm, tn), lambda i, j: (i, j)),
        out_shape=jax.ShapeDtypeStruct((m, n), F32),
        scratch_shapes=[pltpu.VMEM((tm, k), BF16)],
        compiler_params=_params("parallel", "arbitrary"),
        name="norm_linear",
    )(x, g.reshape(1, k), w)


def _mixer_ffn_kernel(h_ref, x_ref, wo_ref, g_ref, w1_ref, w2_ref, o_ref, xn_ref):
    @pl.when(pl.program_id(1) == 0)
    def _():
        mixed = h_ref[...] + _dot(x_ref[...], wo_ref[...])
        o_ref[...] = mixed
        xn_ref[...] = _rms(mixed, g_ref[...]).astype(BF16)

    u = jnp.dot(xn_ref[...], w1_ref[...], preferred_element_type=F32)
    u = jnp.square(jnp.maximum(u, 0.0))
    o_ref[...] += jnp.dot(u.astype(BF16), w2_ref[...], preferred_element_type=F32)


def mixer_ffn_res(h, x, w_o, g, w1, w2, tm, th):
    m, c = h.shape
    k = x.shape[1]
    hid = w1.shape[1]
    return pl.pallas_call(
        _mixer_ffn_kernel,
        grid=(m // tm, hid // th),
        in_specs=[pl.BlockSpec((tm, c), lambda i, j: (i, 0)),
                  pl.BlockSpec((tm, k), lambda i, j: (i, 0)),
                  pl.BlockSpec((k, c), lambda i, j: (0, 0), pipeline_mode=pl.Buffered(1)),
                  pl.BlockSpec((1, c), lambda i, j: (0, 0)),
                  pl.BlockSpec((c, th), lambda i, j: (0, j)),
                  pl.BlockSpec((th, c), lambda i, j: (j, 0))],
        out_specs=pl.BlockSpec((tm, c), lambda i, j: (i, 0)),
        out_shape=jax.ShapeDtypeStruct((m, c), F32),
        scratch_shapes=[pltpu.VMEM((tm, c), BF16)],
        compiler_params=_params("parallel", "arbitrary"),
        name="mixer_ffn_res",
    )(h, x, w_o, g.reshape(1, c), w1, w2)


SHIFT_HALO = 8


def _rwkv_proj_kernel(has_vres, seq, tm, *refs):
    if has_vres:
        (h_ref, halo_ref, sh_ref, gmix_ref, mu_ref, wrkv_ref, w0_ref, w1_ref, w2_ref, a0_ref, a1_ref, a2_ref,
         g1_ref, g2_ref, v0_ref, v1_ref, v2_ref,
         r_ref, lw_ref, k_ref, v_ref, a_ref, g_ref, vg_ref) = refs
    else:
        (h_ref, halo_ref, sh_ref, gmix_ref, mu_ref, wrkv_ref, w0_ref, w1_ref, w2_ref, a0_ref, a1_ref, a2_ref,
         g1_ref, g2_ref,
         r_ref, lw_ref, k_ref, v_ref, a_ref, g_ref) = refs
    x = _rms(h_ref[...], gmix_ref[...])
    rolled = pltpu.roll(x, 1, axis=0)
    row = lax.broadcasted_iota(jnp.int32, x.shape, 0)
    if seq >= tm:
        is_start = (pl.program_id(0) % (seq // tm)) == 0
        tail = _rms(halo_ref[...], gmix_ref[...])[SHIFT_HALO - 1:SHIFT_HALO, :]
        xprev = jnp.where(row == 0, jnp.where(is_start, sh_ref[0], tail), rolled)
    else:
        n_seq = tm // seq
        starts = jnp.broadcast_to(sh_ref[...][:, None, :], (n_seq, seq, x.shape[1])).reshape(x.shape)
        xprev = jnp.where(row % seq == 0, starts, rolled)
    dx = xprev - x
    xm = [(x + dx * mu_ref[p:p + 1, :]).astype(BF16) for p in range(6)]
    r_ref[...] = jnp.dot(xm[0], wrkv_ref[0], preferred_element_type=F32)
    k_ref[...] = jnp.dot(xm[1], wrkv_ref[1], preferred_element_type=F32)
    v_ref[...] = jnp.dot(xm[2], wrkv_ref[2], preferred_element_type=F32)
    wpre = w0_ref[...] + _dot(jnp.tanh(jnp.dot(xm[3], w1_ref[...], preferred_element_type=F32)), w2_ref[...])
    w_log = -_softplus(-wpre) - 0.5
    lw_ref[...] = -jnp.exp(w_log)
    a_ref[...] = _sigmoid(a0_ref[...] + _dot(jnp.dot(xm[4], a1_ref[...], preferred_element_type=F32), a2_ref[...]))
    g_ref[...] = _dot(_sigmoid(jnp.dot(xm[5], g1_ref[...], preferred_element_type=F32)), g2_ref[...])
    if has_vres:
        vg_ref[...] = _sigmoid(v0_ref[...] + _dot(jnp.dot(xm[2], v1_ref[...], preferred_element_type=F32),
                                                  v2_ref[...]))


def rwkv_proj(h, shift, p, seq, tm):
    m, c = h.shape
    assert tm % SHIFT_HALO == 0 and (seq % tm == 0 or tm % seq == 0)
    has_vres = "v1" in p
    row = pl.BlockSpec((tm, c), lambda i: (i, 0))
    halo = pl.BlockSpec((SHIFT_HALO, c), lambda i: (jnp.maximum(i * (tm // SHIFT_HALO) - 1, 0), 0))
    if seq >= tm:
        shift_arg = shift.reshape(-1, 1, c)
        shift_spec = pl.BlockSpec((1, 1, c), lambda i: (i // (seq // tm), 0, 0))
    else:
        shift_arg = shift
        shift_spec = pl.BlockSpec((tm // seq, c), lambda i: (i, 0))
    full = lambda arr: pl.BlockSpec(arr.shape, lambda i: (0,) * arr.ndim, pipeline_mode=pl.Buffered(1))
    names = ["norm_mix", "mu", "w_rkv", "w0", "w1", "w2", "a0", "a1", "a2", "g1", "g2"]
    if has_vres:
        names += ["v0", "v1", "v2"]
    weights = [p[n] for n in names]
    n_out = 7 if has_vres else 6
    return pl.pallas_call(
        functools.partial(_rwkv_proj_kernel, has_vres, seq, tm),
        grid=(m // tm,),
        in_specs=[row, halo, shift_spec] + [full(w) for w in weights],
        out_specs=[row] * n_out,
        out_shape=[jax.ShapeDtypeStruct((m, c), F32)] * n_out,
        compiler_params=_params("parallel"),
        name="rwkv_proj",
    )(h, h, shift_arg, *weights)


PAIR = 2 * RW_HEAD
WKV_SHORT_SEQS_PER_STEP = 4


def _dot_rhs_hilo(a, b):
    bh = b.astype(BF16)
    bl = (b - bh.astype(F32)).astype(BF16)
    ab = a.astype(BF16)
    return jnp.dot(ab, bh, preferred_element_type=F32) + jnp.dot(ab, bl, preferred_element_type=F32)


def _dot_tn_hilo(a, b):
    tn = (((0,), (0,)), ((), ()))
    ah = a.astype(BF16)
    al = (a - ah.astype(F32)).astype(BF16)
    bh = b.astype(BF16)
    bl = (b - bh.astype(F32)).astype(BF16)
    f = lambda x, y: lax.dot_general(x, y, tn, preferred_element_type=F32)
    return f(ah, bh) + f(ah, bl) + f(al, bh)


def _wkv_kernel(has_vres, nseq, nch, npair, ln, *refs):
    if has_vres:
        (r_ref, lw_ref, k_ref, v_ref, a_ref, g_ref, vf_ref, vg_ref, s0_ref,
         kk_ref, ka_ref, rk_ref, lnw_ref, lnb_ref, y_ref, sout_ref, s_ref) = refs
    else:
        (r_ref, lw_ref, k_ref, v_ref, a_ref, g_ref, s0_ref,
         kk_ref, ka_ref, rk_ref, lnw_ref, lnb_ref, y_ref, sout_ref, s_ref) = refs
    c = pl.program_id(2)
    n = RW_HEAD
    rows = 2 * ln

    @pl.when(c == 0)
    def _():
        zero = jnp.zeros((n, n), F32)
        for q in range(nseq * npair):
            bi, p = divmod(q, npair)
            s_ref[q] = jnp.concatenate([jnp.concatenate([s0_ref[bi, 2 * p], zero], axis=1),
                                        jnp.concatenate([zero, s0_ref[bi, 2 * p + 1]], axis=1)], axis=0)

    first = lax.broadcasted_iota(jnp.int32, (ln, PAIR), 1) < n
    ri = lax.broadcasted_iota(jnp.int32, (rows, rows), 0)
    ci = lax.broadcasted_iota(jnp.int32, (rows, rows), 1)
    same = (ri // ln) == (ci // ln)
    incl = same & (ci <= ri)
    strict = same & (ci < ri)
    eye = jnp.where(ri == ci, 1.0, 0.0)
    ti = lax.broadcasted_iota(jnp.int32, (ln, ln), 0)
    tj = lax.broadcasted_iota(jnp.int32, (ln, ln), 1)
    tri = jnp.where(tj <= ti, 1.0, 0.0).astype(BF16)

    def seg_sum(x):
        s1 = jnp.sum(jnp.where(first, x, 0.0), axis=-1, keepdims=True)
        s2 = jnp.sum(jnp.where(first, 0.0, x), axis=-1, keepdims=True)
        return jnp.where(first, s1, s2)

    def by_head(x):
        return jnp.concatenate([jnp.where(first, x, 0.0), jnp.where(first, 0.0, x)], axis=0)

    pairs = range(nseq * npair)
    lanes = [slice((q % npair) * PAIR, (q % npair + 1) * PAIR) for q in pairs]

    def tok_rows(ch, p):
        first_row = (ch * nseq + p // npair) * ln
        return slice(first_row, first_row + ln)

    def prepare(ch, p):
        toks = {p: tok_rows(ch, p)}
        r = r_ref[toks[p], lanes[p]]
        lw = lw_ref[toks[p], lanes[p]]
        k = k_ref[toks[p], lanes[p]]
        v = v_ref[toks[p], lanes[p]]
        a = a_ref[toks[p], lanes[p]]
        if has_vres:
            v = v + (vf_ref[toks[p], lanes[p]] - v) * vg_ref[toks[p], lanes[p]]
        kk = k * kk_ref[:, lanes[p]]
        kk = kk / jnp.maximum(jnp.sqrt(seg_sum(kk * kk)), 1e-12)
        k2 = k * (1.0 + (a - 1.0) * ka_ref[:, lanes[p]])
        bv = kk * a
        hi, mid, lo = _split3(lw)
        cl3 = jnp.dot(tri, jnp.concatenate([hi, mid, lo], axis=1), preferred_element_type=F32)
        cl = cl3[:, :PAIR] + cl3[:, PAIR:2 * PAIR] + cl3[:, 2 * PAIR:]
        cl_end = cl[ln - 1:ln, :]
        g_inv = jnp.exp(-cl)
        g_end = jnp.exp(cl_end - cl)
        rt = r * jnp.exp(cl)
        at = -kk * jnp.exp(cl - lw)
        lhs = jnp.concatenate([by_head(at), by_head(rt)], axis=0)
        bh = bv * g_inv
        kh = k2 * g_inv
        return dict(r=r, v=v, k2=k2, lhs=lhs.astype(BF16),
                    rhs=jnp.concatenate([bh, bh, kh, kh], axis=0).astype(BF16),
                    decay=jnp.exp(cl_end), v_h=by_head(v),
                    ws=jnp.concatenate([by_head(bv * g_end), by_head(k2 * g_end)], axis=0))

    def same_block(size):
        return (ri // size) == (ci // size)

    below = ci < ri
    levels = list(range(1, int(math.log2(ln))))

    def couple(st):
        aa = [_dot_nt(st[p]["lhs"], st[p]["rhs"]) for p in pairs]
        a_ak = [jnp.where(strict, aa[p][:rows, rows:], 0.0) for p in pairs]
        a_r = [jnp.concatenate([jnp.where(incl, aa[p][rows:, :rows], 0.0),
                                jnp.where(incl, aa[p][rows:, rows:], 0.0)], axis=1).astype(BF16) for p in pairs]
        akv = [_dot_rhs_hilo(a_ak[p], st[p]["v_h"]) for p in pairs]
        tinv = [eye + jnp.where(below & same_block(2), aa[p][:rows, :rows], 0.0) for p in pairs]
        return dict(n=[aa[p][:rows, :rows] for p in pairs], a_r=a_r, akv=akv, tinv=tinv)

    def merge(cp, level):
        size = 2 ** level
        coupling = below & same_block(2 * size) & jnp.logical_not(same_block(size))
        n_off = [jnp.where(coupling, cp["n"][p], 0.0).astype(BF16) for p in pairs]
        tinv_bf = [cp["tinv"][p].astype(BF16) for p in pairs]
        right = [_dot(n_off[p], tinv_bf[p]) for p in pairs]
        cp["tinv"] = [cp["tinv"][p] + _dot(tinv_bf[p], right[p]) for p in pairs]

    def advance(st, cp):
        s_prev = [s_ref[p] for p in pairs]
        xy = [_dot_nt(st[p]["lhs"], s_prev[p]) for p in pairs]
        u = [_dot_rhs_hilo(cp["tinv"][p], xy[p][:rows] + cp["akv"][p]) for p in pairs]
        zs = [jnp.concatenate([u[p], st[p]["v_h"]], axis=0) for p in pairs]
        for p in pairs:
            s_ref[p] = s_prev[p] * st[p]["decay"] + _dot_tn_hilo(zs[p], st[p]["ws"])
        return [xy[p][rows:] + _dot(cp["a_r"][p], zs[p]) for p in pairs]

    def finish(ch, p, st, y_h):
        y = y_h[p][:ln] + y_h[p][ln:]
        mean = seg_sum(y) * (1.0 / n)
        yc = y - mean
        var = seg_sum(yc * yc) * (1.0 / n)
        y = yc * lax.rsqrt(var + RW_LNX_EPS) * lnw_ref[:, lanes[p]] + lnb_ref[:, lanes[p]]
        y = y + seg_sum(st[p]["r"] * st[p]["k2"] * rk_ref[:, lanes[p]]) * st[p]["v"]
        y_ref[tok_rows(ch, p), lanes[p]] = y * g_ref[tok_rows(ch, p), lanes[p]]

    def shares(items, parts):
        per = -(-len(items) // parts)
        return [items[i * per:(i + 1) * per] for i in range(parts)]

    st_a = [prepare(0, p) for p in pairs]
    cp_a = couple(st_a)
    if nch == 1:
        for level in levels:
            merge(cp_a, level)
        y_a = advance(st_a, cp_a)
        for p in pairs:
            finish(0, p, st_a, y_a)
    else:
        st_b = []
        for level, group in zip(levels, shares(list(pairs), len(levels))):
            merge(cp_a, level)
            st_b += [prepare(1, p) for p in group]
        cp_b = couple(st_b)
        y_a = advance(st_a, cp_a)
        for level, group in zip(levels, shares(list(pairs), len(levels))):
            merge(cp_b, level)
            for p in group:
                finish(0, p, st_a, y_a)
        y_b = advance(st_b, cp_b)
        for p in pairs:
            finish(1, p, st_b, y_b)

    @pl.when(c == pl.num_programs(2) - 1)
    def _():
        for q in range(nseq * npair):
            bi, p = divmod(q, npair)
            sout_ref[bi, 2 * p] = s_ref[q, :n, :n]
            sout_ref[bi, 2 * p + 1] = s_ref[q, n:, n:]


def wkv_scan(r, lw, k, v, a, g, s0, p, batch, seq, npair, ln, vres=None):
    nc = seq // ln
    nseq = WKV_SHORT_SEQS_PER_STEP if (nc == 1 and batch % WKV_SHORT_SEQS_PER_STEP == 0) else 1
    has_vres = vres is not None
    width = npair * PAIR
    nch = 2 if (nseq == 1 and nc % 2 == 0) else 1
    steps = nc // nch
    tok = pl.BlockSpec((nseq * nch * ln, width), lambda b, q, c: (b * steps + c, q))
    par = pl.BlockSpec((1, width), lambda b, q, c: (0, q))
    st = pl.BlockSpec((nseq, 2 * npair, RW_HEAD, RW_HEAD), lambda b, q, c: (b, q, 0, 0))
    seqs = [r, lw, k, v, a, g] + (list(vres) if has_vres else [])
    pars = [p[nm].reshape(1, D_MODEL) for nm in ("k_k", "k_a", "r_k", "lnx_w", "lnx_b")]
    return pl.pallas_call(
        functools.partial(_wkv_kernel, has_vres, nseq, nch, npair, ln),
        grid=(batch // nseq, D_MODEL // width, steps),
        in_specs=[tok] * len(seqs) + [st] + [par] * 5,
        out_specs=[tok, st],
        out_shape=[jax.ShapeDtypeStruct(r.shape, F32), jax.ShapeDtypeStruct(s0.shape, F32)],
        scratch_shapes=[pltpu.VMEM((nseq * npair, PAIR, PAIR), F32)],
        compiler_params=_params("parallel", "parallel", "arbitrary"),
        name="wkv_scan",
    )(*seqs, s0, *pars)


def prep_rwkv(w, j, i):
    row = lambda z: z.reshape(1, -1).astype(F32)
    p = {
        "norm_mix": row(w["norm_mix"][i]),
        "mu": w["rwkv_mu"][j],
        "w_rkv": w["rwkv_w_rkv"][j].astype(BF16),
        "w0": row(w["rwkv_w0"][j]), "w1": w["rwkv_w1"][j].astype(BF16), "w2": w["rwkv_w2"][j].astype(BF16),
        "a0": row(w["rwkv_a0"][j]), "a1": w["rwkv_a1"][j].astype(BF16), "a2": w["rwkv_a2"][j].astype(BF16),
        "g1": w["rwkv_g1"][j].astype(BF16), "g2": w["rwkv_g2"][j].astype(BF16),
        "k_k": w["rwkv_k_k"][j], "k_a": w["rwkv_k_a"][j], "r_k": w["rwkv_r_k"][j],
        "lnx_w": w["rwkv_lnx_w"][j], "lnx_b": w["rwkv_lnx_b"][j],
        "w_o": w["rwkv_w_o"][j].astype(BF16),
    }
    if j > 0:
        p["v0"] = row(w["rwkv_v0"][j - 1])
        p["v1"] = w["rwkv_v1"][j - 1].astype(BF16)
        p["v2"] = w["rwkv_v2"][j - 1].astype(BF16)
    return p


def rwkv_layer(h, shift, s0, vfirst, p, batch, seq, tm, npair, ln):
    m, c = h.shape
    outs = rwkv_proj(h, shift, p, seq, tm)
    r, lw, k, v, a, g = outs[:6]
    vres = None if vfirst is None else (vfirst, outs[6])
    y, s_new = wkv_scan(r, lw, k, v, a, g, s0, p, batch, seq, npair, ln, vres)
    shift_new = rmsnorm_rows(h.reshape(batch, seq, c)[:, -1], p["norm_mix"], batch)
    return y, s_new, shift_new, (v if vfirst is None else vfirst)


SEG_PAD = 128


def _dot_hilo_rhs(x, ones):
    hi = x.astype(BF16)
    lo = (x - hi.astype(F32)).astype(BF16)
    return jnp.dot(hi, ones, preferred_element_type=F32) + jnp.dot(lo, ones, preferred_element_type=F32)


def _seg_rms_scale(x, seg_ref, segt_ref, invw_ref):
    ssq = _dot(x * x, seg_ref[...])
    return _dot_hilo_rhs(lax.rsqrt(ssq * invw_ref[...] + NORM_EPS), segt_ref[...])


def _swap_halves(x, group):
    half = group // 2
    width = x.shape[-1]
    lane = lax.broadcasted_iota(jnp.int32, x.shape, x.ndim - 1)
    return jnp.where((lane % group) < half,
                     pltpu.roll(x, width - half, axis=x.ndim - 1),
                     pltpu.roll(x, half, axis=x.ndim - 1))


SLOT = 128


def _mla_proj_kernel(with_kv, *refs):
    (h_ref, cos_ref, sin_ref, gmix_ref, win_ref, gq_ref, gkv_ref, gkr_ref, wuq_ref, gqc_ref,
     seg_ref, segt_ref, invw_ref) = refs[:13]
    if with_kv:
        wuk_ref, wuv_ref, gkn_ref, qc_ref, c_ref, kp_ref, kc_ref, v_ref = refs[13:]
    else:
        qc_ref, c_ref, kp_ref = refs[13:]
    x = _rms(h_ref[...], gmix_ref[...])
    hp = _dot(x, win_ref[...])
    q_a = _rms(hp[:, :MLA_Q_LORA], gq_ref[...])
    c = _rms(hp[:, MLA_Q_LORA:MLA_Q_LORA + MLA_KV_LORA], gkv_ref[...])
    c_ref[...] = c

    cos = cos_ref[...]
    sin = sin_ref[...]

    def rope(z, cs, sn):
        return z * cs + _swap_halves(z, MLA_ROPE) * sn

    kp_raw = hp[:, MLA_Q_LORA + MLA_KV_LORA:]
    kp_scale = lax.rsqrt(jnp.sum(kp_raw * kp_raw, axis=-1, keepdims=True) * (1.0 / MLA_ROPE) + NORM_EPS)
    kp = rope(pltpu.roll(kp_raw, MLA_NOPE, axis=1) * gkr_ref[...], cos, sin) * kp_scale
    kp_ref[...] = kp[:, MLA_NOPE:MLA_NOPE + MLA_ROPE]

    q = _dot(q_a, wuq_ref[...])
    cos_h = jnp.tile(cos, (1, MLA_HEADS))
    sin_h = jnp.tile(sin, (1, MLA_HEADS))
    qc = rope(q * gqc_ref[...], cos_h, sin_h) * _seg_rms_scale(q, seg_ref, segt_ref, invw_ref)
    qc_ref[...] = qc.astype(BF16)
    if with_kv:
        kraw = _dot(c, wuk_ref[...])
        kn = kraw * _seg_rms_scale(kraw, seg_ref, segt_ref, invw_ref) * gkn_ref[...]
        kc_ref[...] = (kn + jnp.tile(kp, (1, MLA_HEADS))).astype(BF16)
        v_ref[0] = _dot_nt(wuv_ref[...], c).astype(BF16)


def _slot_seg_matrix():
    lane = jnp.arange(MLA_HEADS * SLOT)
    head, off = lane // SLOT, lane % SLOT
    col = jnp.where(off < MLA_NOPE, head, jnp.where(off < MLA_NOPE + MLA_ROPE, MLA_HEADS + head, SEG_PAD))
    return (col[:, None] == jnp.arange(SEG_PAD)[None, :]).astype(BF16)


def _slot(nope, rope):
    pad = jnp.zeros(nope.shape[:-1] + (SLOT - MLA_NOPE - MLA_ROPE,), nope.dtype)
    return jnp.concatenate([nope, rope, pad], axis=-1)


def _rope_tables(pos):
    half = MLA_ROPE // 2
    inv = ROPE_THETA ** (-jnp.arange(half, dtype=F32) / half)
    ang = pos.astype(F32)[:, None] * inv[None, :]
    cos = jnp.cos(ang)
    sin = jnp.sin(ang)
    ones = jnp.ones((pos.shape[0], MLA_NOPE), F32)
    return (_slot(ones, jnp.concatenate([cos, cos], axis=1)),
            _slot(jnp.zeros_like(ones), jnp.concatenate([-sin, sin], axis=1)))


def mla_proj(h, pos, p, tm, with_kv):
    m, c = h.shape
    seq = pos.shape[0]
    cos, sin = _rope_tables(pos)
    if seq >= tm:
        nrep = seq // tm
    else:
        cos, sin = jnp.tile(cos, (tm // seq, 1)), jnp.tile(sin, (tm // seq, 1))
        nrep = 1
    row = lambda width: pl.BlockSpec((tm, width), lambda i: (i, 0))
    tab = pl.BlockSpec((tm, SLOT), lambda i: (i % nrep, 0))
    full = lambda arr: pl.BlockSpec(arr.shape, lambda i: (0,) * arr.ndim)
    seg = _slot_seg_matrix()
    invw = jnp.concatenate([jnp.full((MLA_HEADS,), 1.0 / MLA_NOPE, F32), jnp.full((MLA_HEADS,), 1.0 / MLA_ROPE, F32),
                            jnp.ones((SEG_PAD - 2 * MLA_HEADS,), F32)]).reshape(1, SEG_PAD)
    weights = [p["norm_mix"], p["w_in"], p["q_norm"], p["kv_norm"], p["kr_norm"], p["w_uq"],
               p["q_gain"] if with_kv else p["q_gain_abs"], seg, seg.T, invw]
    wide = MLA_HEADS * SLOT
    out_shape = [jax.ShapeDtypeStruct((m, wide), BF16),
                 jax.ShapeDtypeStruct((m, MLA_KV_LORA), F32), jax.ShapeDtypeStruct((m, MLA_ROPE), F32)]
    out_specs = [row(wide), row(MLA_KV_LORA), row(MLA_ROPE)]
    if with_kv:
        weights += [p["w_uk"], p["w_uv_t"], p["kn_gain"]]
        out_shape += [jax.ShapeDtypeStruct((m, wide), BF16),
                      jax.ShapeDtypeStruct((m // tm, MLA_HEADS * MLA_V, tm), BF16)]
        out_specs += [row(wide), pl.BlockSpec((1, MLA_HEADS * MLA_V, tm), lambda i: (i, 0, 0))]
    return pl.pallas_call(
        functools.partial(_mla_proj_kernel, with_kv),
        grid=(m // tm,),
        in_specs=[row(c), tab, tab] + [full(w) for w in weights],
        out_specs=out_specs,
        out_shape=out_shape,
        compiler_params=_params("parallel"),
        name="mla_proj",
    )(h, cos, sin, *weights)


def prep_mla(w, j, i):
    row = lambda z: z.reshape(1, -1).astype(F32)
    w_in = w["mla_w_in"][j]
    pad = jnp.zeros((D_MODEL, 128 - MLA_ROPE), F32)
    w_uq = w["mla_w_uq"][j]
    w_uk = w["mla_w_uk"][j]
    zero_n = jnp.zeros((MLA_NOPE,), F32)
    zero_r = jnp.zeros((MLA_ROPE,), F32)
    qn, qr, kn = w["mla_qn_norm"][j], w["mla_qr_norm"][j], w["mla_kn_norm"][j]
    return {
        "norm_mix": row(w["norm_mix"][i]),
        "w_in": jnp.concatenate([w_in, pad], axis=1).astype(BF16),
        "q_norm": row(w["mla_q_norm"][j]), "kv_norm": row(w["mla_kv_norm"][j]),
        "kr_norm": row(_slot(zero_n, w["mla_kr_norm"][j])),
        "w_uq": _slot(w_uq[:, :, :MLA_NOPE], w_uq[:, :, MLA_NOPE:]).reshape(MLA_Q_LORA, -1).astype(BF16),
        "q_gain": row(jnp.tile(_slot(qn, qr) * MLA_SCALE, MLA_HEADS)),
        "q_gain_abs": row(jnp.tile(_slot(qn * kn, qr) * MLA_SCALE, MLA_HEADS)),
        "kn_gain": row(jnp.tile(_slot(kn, zero_r), MLA_HEADS)),
        "w_uk": _slot(w_uk, jnp.zeros(w_uk.shape[:2] + (MLA_ROPE,), F32)).reshape(MLA_KV_LORA, -1).astype(BF16),
        "w_uv": w["mla_w_uv"][j].reshape(MLA_KV_LORA, -1).astype(BF16),
        "w_uv_t": w["mla_w_uv"][j].reshape(MLA_KV_LORA, -1).T.astype(BF16),
        "w_uk_t": w["mla_w_uk"][j].reshape(MLA_KV_LORA, -1).T.astype(BF16),
        "w_uk_heads": jnp.transpose(w["mla_w_uk"][j], (1, 2, 0)).astype(BF16),
        "w_o": w["mla_w_o"][j].astype(BF16),
    }


ATTN_HEADS_PER_STEP = 8
NEG_BIG = -1e30


def _flash_kernel(tq, qc_ref, kc_ref, vt_ref, o_ref, m_ref, l_ref, acc_ref):
    qi = pl.program_id(2)
    g = ATTN_HEADS_PER_STEP
    heads = range(g)
    key_i = lax.broadcasted_iota(jnp.int32, (tq, tq), 0)
    qry_i = lax.broadcasted_iota(jnp.int32, (tq, tq), 1)
    m_ref[...] = jnp.full_like(m_ref, NEG_BIG)
    l_ref[...] = jnp.zeros_like(l_ref)
    acc_ref[...] = jnp.zeros_like(acc_ref)

    def block(j, masked):
        keys = pl.ds(pl.multiple_of(j * tq, tq), tq)
        s = [_dot_nt(kc_ref[0, keys, hh * SLOT:(hh + 1) * SLOT], qc_ref[0, :, hh * SLOT:(hh + 1) * SLOT])
             for hh in heads]
        if masked:
            s = [jnp.where(key_i <= qry_i, s[hh], NEG_BIG) for hh in heads]
        m_prev = [m_ref[hh] for hh in heads]
        m_new = [jnp.maximum(m_prev[hh], jnp.max(s[hh], axis=0, keepdims=True)) for hh in heads]
        alpha = [jnp.exp(m_prev[hh] - m_new[hh]) for hh in heads]
        pr = [jnp.exp(s[hh] - m_new[hh]) for hh in heads]
        pv = [jnp.dot(vt_ref[j, hh * MLA_V:(hh + 1) * MLA_V, :], pr[hh].astype(BF16), preferred_element_type=F32)
              for hh in heads]
        for hh in heads:
            l_ref[hh] = alpha[hh] * l_ref[hh] + jnp.sum(pr[hh], axis=0, keepdims=True)
            acc_ref[hh] = alpha[hh] * acc_ref[hh] + pv[hh]
            m_ref[hh] = m_new[hh]

    def body(j, carry):
        block(j, False)
        return carry

    lax.fori_loop(0, qi, body, 0)
    block(qi, True)
    o_t = jnp.concatenate([acc_ref[hh] / l_ref[hh] for hh in heads], axis=0)
    o_ref[0] = o_t.T


def flash_prompt(qc, kc, vt, batch, seq, tq):
    g = ATTN_HEADS_PER_STEP
    nq = seq // tq
    r3 = lambda z: z.reshape(batch, seq, -1)
    return pl.pallas_call(
        functools.partial(_flash_kernel, tq),
        grid=(batch, MLA_HEADS // g, nq),
        in_specs=[pl.BlockSpec((1, tq, g * SLOT), lambda b, hq, i: (b, i, hq)),
                  pl.BlockSpec((1, seq, g * SLOT), lambda b, hq, i: (b, 0, hq)),
                  pl.BlockSpec((nq, g * MLA_V, tq), lambda b, hq, i: (b, hq, 0))],
        out_specs=pl.BlockSpec((1, tq, g * MLA_V), lambda b, hq, i: (b, i, hq)),
        out_shape=jax.ShapeDtypeStruct((batch, seq, MLA_HEADS * MLA_V), F32),
        scratch_shapes=[pltpu.VMEM((g, 1, tq), F32), pltpu.VMEM((g, 1, tq), F32),
                        pltpu.VMEM((g, MLA_V, tq), F32)],
        compiler_params=_params("parallel", "parallel", "parallel"),
        name="flash_prompt",
    )(r3(qc), r3(kc), vt).reshape(batch * seq, -1)


def _bmm_kernel(a_ref, b_ref, o_ref):
    o_ref[0] = _dot(a_ref[0], b_ref[0]).astype(o_ref.dtype)


def bmm(a, b, out_dtype):
    g, m, k = a.shape
    n = b.shape[2]
    return pl.pallas_call(
        _bmm_kernel,
        grid=(g,),
        in_specs=[pl.BlockSpec((1, m, k), lambda i: (i, 0, 0)), pl.BlockSpec((1, k, n), lambda i: (i, 0, 0))],
        out_specs=pl.BlockSpec((1, m, n), lambda i: (i, 0, 0)),
        out_shape=jax.ShapeDtypeStruct((g, m, n), out_dtype),
        compiler_params=_params("parallel"),
        name="bmm",
    )(a, b)


def _paged_attn_kernel(pp, seq, n_pages, pt_ref, qa_ref, qp_ref, cn_ref, kpn_ref, wukt_ref, wuv_ref,
                       pool_c_ref, pool_kp_ref, o_ref, m_ref, l_ref, acc_ref, lhs_ref, c_buf, kp_buf, sems):
    b = pl.program_id(0)
    n_groups = n_pages // pp
    nq = seq * MLA_HEADS
    nup = MLA_HEADS * MLA_NOPE

    def group_copies(bb, g, slot):
        out = []
        for i in range(pp):
            page = pt_ref[bb * n_pages + g * pp + i]
            out.append(pltpu.make_async_copy(pool_c_ref.at[page], c_buf.at[slot, i], sems.at[0, slot]))
            out.append(pltpu.make_async_copy(pool_kp_ref.at[page], kp_buf.at[slot, i], sems.at[1, slot]))
        return out

    @pl.when(b == 0)
    def _():
        for cp in group_copies(b, 0, 0):
            cp.start()

    m_ref[...] = jnp.full_like(m_ref, NEG_BIG)
    l_ref[...] = jnp.zeros_like(l_ref)
    acc_ref[...] = jnp.zeros_like(acc_ref)
    lhs_ref[:nup, :] = wukt_ref[...]
    lhs_ref[nup:, :] = qa_ref[0]

    qp = qp_ref[0]

    def attend(c_blks, kp_t_blks, mask):
        subs = range(len(c_blks))
        nk = c_blks[0].shape[0]
        c_bf = [c_blks[i].astype(BF16) for i in subs]
        both = [_dot_nt(lhs_ref[...], c_bf[i]) for i in subs]
        ssq = [jnp.sum(jnp.square(both[i][:nup]).reshape(MLA_HEADS, MLA_NOPE, nk), axis=1) for i in subs]
        rs = [lax.rsqrt(ssq[i] * (1.0 / MLA_NOPE) + NORM_EPS) for i in subs]
        s = [both[i][nup:] * jnp.concatenate([rs[i]] * seq, axis=0) + _dot(qp, kp_t_blks[i]) for i in subs]
        if mask is not None:
            s = [jnp.where(mask, s[i], NEG_BIG) for i in subs]
        m_prev = m_ref[...]
        m_new = m_prev
        for i in subs:
            m_new = jnp.maximum(m_new, jnp.max(s[i], axis=-1, keepdims=True))
        alpha = jnp.exp(m_prev - m_new)
        pr = [jnp.exp(s[i] - m_new) for i in subs]
        l_new = alpha * l_ref[...]
        acc = alpha * acc_ref[...]
        for i in subs:
            l_new = l_new + jnp.sum(pr[i], axis=-1, keepdims=True)
            acc = acc + _dot(pr[i], c_bf[i])
        l_ref[...] = l_new
        acc_ref[...] = acc
        m_ref[...] = m_new

    pages_per_sub = 2

    def group_step(g, carry):
        slot = (b * n_groups + g) % 2
        last = g == n_groups - 1
        next_b = jnp.where(last, b + 1, b)
        next_g = jnp.where(last, 0, g + 1)

        @pl.when(next_b < pl.num_programs(0))
        def _():
            for cp in group_copies(next_b, next_g, 1 - slot):
                cp.start()

        for cp in group_copies(b, g, slot):
            cp.wait()
        subs = range(0, pp, pages_per_sub)
        attend([c_buf[slot, pl.ds(i, pages_per_sub)].reshape(pages_per_sub * PAGE_SIZE, MLA_KV_LORA) for i in subs],
               [jnp.concatenate([kp_buf[slot, i + k] for k in range(pages_per_sub)], axis=1) for i in subs], None)
        return carry

    lax.fori_loop(0, n_groups, group_step, 0)

    qtok = lax.broadcasted_iota(jnp.int32, (nq, seq), 0) // MLA_HEADS
    ktok = lax.broadcasted_iota(jnp.int32, (nq, seq), 1)
    attend([cn_ref[0]], [kpn_ref[0]], ktok <= qtok)
    o_lat = acc_ref[...] / l_ref[...]
    full = _dot(o_lat, wuv_ref[...])
    rhead = lax.broadcasted_iota(jnp.int32, full.shape, 0) % MLA_HEADS
    lhead = lax.broadcasted_iota(jnp.int32, full.shape, 1) // MLA_V
    full = jnp.where(rhead == lhead, full, 0.0)
    o_ref[0] = jnp.sum(full.reshape(seq, MLA_HEADS, MLA_HEADS * MLA_V), axis=1)


def paged_attn(q_abs, qp, c_new, kp_new, pool_c, pool_kp_t, page_table, p, batch, seq, pp):
    n_pages = page_table.shape[1]
    assert n_pages % pp == 0 and pp % 2 == 0
    nq = seq * MLA_HEADS
    per_b = lambda shp: pl.BlockSpec((1,) + shp, lambda b, pt: (b, 0, 0))
    full = lambda arr: pl.BlockSpec(arr.shape, lambda b, pt: (0,) * arr.ndim)
    in_hbm = pl.BlockSpec(memory_space=pl.ANY)
    grid_spec = pltpu.PrefetchScalarGridSpec(
        num_scalar_prefetch=1,
        grid=(batch,),
        in_specs=[per_b((nq, MLA_KV_LORA)), per_b((nq, MLA_ROPE)), per_b((seq, MLA_KV_LORA)),
                  per_b((MLA_ROPE, seq)), full(p["w_uk_t"]), full(p["w_uv"]), in_hbm, in_hbm],
        out_specs=per_b((seq, MLA_HEADS * MLA_V)),
        scratch_shapes=[pltpu.VMEM((nq, 1), F32), pltpu.VMEM((nq, 1), F32), pltpu.VMEM((nq, MLA_KV_LORA), F32),
                        pltpu.VMEM((MLA_HEADS * MLA_NOPE + nq, MLA_KV_LORA), BF16),
                        pltpu.VMEM((2, pp, PAGE_SIZE, MLA_KV_LORA), F32),
                        pltpu.VMEM((2, pp, MLA_ROPE, PAGE_SIZE), F32),
                        pltpu.SemaphoreType.DMA((2, 2))],
    )
    return pl.pallas_call(
        functools.partial(_paged_attn_kernel, pp, seq, n_pages),
        grid_spec=grid_spec,
        out_shape=jax.ShapeDtypeStruct((batch, seq, MLA_HEADS * MLA_V), F32),
        compiler_params=_params("arbitrary"),
        name="paged_attn",
    )(page_table.reshape(-1), q_abs, qp, c_new.reshape(batch, seq, -1),
      jnp.swapaxes(kp_new.reshape(batch, seq, -1), 1, 2),
      p["w_uk_t"], p["w_uv"], pool_c, pool_kp_t)


def mla_layer_prompt(h, p, batch, seq, tm, tq):
    assert tm == tq, "the value tiles written by mla_proj are the key blocks of flash_prompt"
    qc, c, kp, kc, vt = mla_proj(h, jnp.arange(seq), p, tm, True)
    return flash_prompt(qc, kc, vt, batch, seq, tq), c, kp


def mla_layer_sample(h, pool_c, pool_kp, page_table, p, batch, seq, past_len, tm, pp):
    m = batch * seq
    qc, c, kp = mla_proj(h, past_len + jnp.arange(seq), p, tm, False)
    qc = qc.reshape(m, MLA_HEADS, SLOT)
    q_heads = jnp.swapaxes(qc[:, :, :MLA_NOPE], 0, 1)
    q_abs = bmm(q_heads, p["w_uk_heads"], BF16)
    q_abs = jnp.swapaxes(q_abs, 0, 1).reshape(batch, seq * MLA_HEADS, MLA_KV_LORA)
    qp = qc[:, :, MLA_NOPE:MLA_NOPE + MLA_ROPE].reshape(batch, seq * MLA_HEADS, MLA_ROPE)
    o = paged_attn(q_abs, qp, c, kp, pool_c, jnp.swapaxes(pool_kp, 1, 2), page_table, p, batch, seq, pp)
    return o.reshape(m, -1), c, kp


DT_PAD = 128
CONV_HALO = 8
MB_GN = MB_GROUPS * MB_STATE
MB_GROUP_INNER = MB_INNER // MB_GROUPS
MB_HEADS_PER_GROUP = MB_HEADS // MB_GROUPS


def _ssd_kernel(ck, xbc_ref, prev_ref, cs_ref, z_ref, dtr_ref, h0_ref, cw_ref, cb_ref, dtb_ref, alog_ref,
                dskip_ref, nw_ref, expand_ref, y_ref, hout_ref, h_ref):
    c = pl.program_id(1)

    @pl.when(c == 0)
    def _():
        h_ref[...] = h0_ref[0]

    halo = jnp.where(c == 0, cs_ref[0], prev_ref[0])
    xext = jnp.concatenate([halo, xbc_ref[0]], axis=0)
    conv = cb_ref[...] + xext[CONV_HALO:, :] * cw_ref[MB_CONV - 1:MB_CONV, :]
    for j in range(MB_CONV - 1):
        shifted = pltpu.roll(xext, MB_CONV - 1 - j, axis=0)[CONV_HALO:, :]
        conv = conv + shifted * cw_ref[j:j + 1, :]
    xbc = _silu(conv)
    xs = xbc[:, :MB_INNER]
    bm = xbc[:, MB_INNER:MB_INNER + MB_GN]
    cm = xbc[:, MB_INNER + MB_GN:]

    dt = _softplus(dtr_ref[0] + dtb_ref[...])
    da = dt * (-jnp.exp(alog_ref[...]))
    ri = lax.broadcasted_iota(jnp.int32, (ck, ck), 0)
    ci = lax.broadcasted_iota(jnp.int32, (ck, ck), 1)
    causal = ci <= ri
    tri = jnp.where(causal, 1.0, 0.0).astype(BF16)
    hi, mid, lo = _split3(da)
    acum = (jnp.dot(tri, hi, preferred_element_type=F32) + jnp.dot(tri, mid, preferred_element_type=F32)
            + jnp.dot(tri, lo, preferred_element_type=F32))
    tn = (((0,), (0,)), ((), ()))
    tri_t = jnp.where(ri <= ci, 1.0, 0.0).astype(BF16)
    acum_t = (lax.dot_general(hi, tri_t, tn, preferred_element_type=F32)
              + lax.dot_general(mid, tri_t, tn, preferred_element_type=F32)
              + lax.dot_general(lo, tri_t, tn, preferred_element_type=F32))
    e_last = jnp.exp(acum[ck - 1:ck, :])

    spread = expand_ref[...]
    dt_hi = dt.astype(BF16)
    dt_lo = (dt - dt_hi.astype(F32)).astype(BF16)
    dt_x = jnp.dot(dt_hi, spread, preferred_element_type=F32) + jnp.dot(dt_lo, spread, preferred_element_type=F32)
    ah, am, al = _split3(acum)
    acum_x = (jnp.dot(ah, spread, preferred_element_type=F32) + jnp.dot(am, spread, preferred_element_type=F32)
              + jnp.dot(al, spread, preferred_element_type=F32))
    xdt = xs * dt_x
    xdt_end = xdt * jnp.exp(acum_x[ck - 1:ck, :] - acum_x)
    e_cum_x = jnp.exp(acum_x)

    pair_w = 2 * MB_HEAD
    first = lax.broadcasted_iota(jnp.int32, (ck, pair_w), 1) < MB_HEAD
    upper = lax.broadcasted_iota(jnp.int32, (pair_w, MB_STATE), 0) < MB_HEAD
    pairs_per_group = MB_HEADS_PER_GROUP // 2
    ys = []
    for g in range(MB_GROUPS):
        b_g = bm[:, g * MB_STATE:(g + 1) * MB_STATE]
        c_g = cm[:, g * MB_STATE:(g + 1) * MB_STATE]
        cb = _dot_nt(c_g, b_g)
        cols = slice(g * MB_GROUP_INNER, (g + 1) * MB_GROUP_INNER)
        y_state = _dot_nt(c_g, h_ref[cols, :]) * e_cum_x[:, cols]
        lmat = []
        for hh in range(MB_HEADS_PER_GROUP):
            hd = g * MB_HEADS_PER_GROUP + hh
            seg = acum[:, hd:hd + 1] - acum_t[hd:hd + 1, :]
            lmat.append(cb * jnp.where(causal, jnp.exp(jnp.where(causal, seg, 0.0)), 0.0))
        for pr in range(pairs_per_group):
            lanes = slice(g * MB_GROUP_INNER + pr * pair_w, g * MB_GROUP_INNER + (pr + 1) * pair_w)
            x_p = xdt[:, lanes]
            y_p = jnp.where(first, _dot(lmat[2 * pr], x_p), _dot(lmat[2 * pr + 1], x_p))
            ys.append(y_p + y_state[:, pr * pair_w:(pr + 1) * pair_w])
            hd = g * MB_HEADS_PER_GROUP + 2 * pr
            decay = jnp.where(upper, e_last[:, hd:hd + 1], e_last[:, hd + 1:hd + 2])
            h_ref[lanes, :] = h_ref[lanes, :] * decay + _dot_tn(xdt_end[:, lanes], b_g)

    y = jnp.concatenate(ys, axis=1) + dskip_ref[...] * xs
    yz = y * _silu(z_ref[0])
    outs = []
    for g in range(MB_GROUPS):
        yg = yz[:, g * MB_GROUP_INNER:(g + 1) * MB_GROUP_INNER]
        outs.append(yg * lax.rsqrt(jnp.mean(yg * yg, axis=-1, keepdims=True) + NORM_EPS))
    y_ref[0] = (jnp.concatenate(outs, axis=1) * nw_ref[...]).astype(y_ref.dtype)

    @pl.when(c == pl.num_programs(1) - 1)
    def _():
        hout_ref[0] = h_ref[...]


def ssd_scan(xbc, z, dt_raw, conv_state, h0, p, batch, seq, ck):
    assert CONV_HALO % 8 == 0 and ck % CONV_HALO == 0
    nc = seq // ck
    halo_blocks = ck // CONV_HALO
    cs8 = jnp.concatenate([jnp.zeros((batch, CONV_HALO - (MB_CONV - 1), MB_CONV_DIM), F32), conv_state], axis=1)
    chunk = lambda width: pl.BlockSpec((1, ck, width), lambda b, c: (b, c, 0))
    full = lambda arr: pl.BlockSpec(arr.shape, lambda b, c: (0,) * arr.ndim)
    expand = (jnp.arange(DT_PAD)[:, None] == jnp.arange(MB_INNER)[None, :] // MB_HEAD).astype(BF16)
    weights = [p["conv_w"], p["conv_b"], p["dt_bias"], p["a_log"], p["d_skip"], p["norm_w"], expand]
    state = pl.BlockSpec((1, MB_INNER, MB_STATE), lambda b, c: (b, 0, 0))
    return pl.pallas_call(
        functools.partial(_ssd_kernel, ck),
        grid=(batch, nc),
        in_specs=[chunk(MB_CONV_DIM),
                  pl.BlockSpec((1, CONV_HALO, MB_CONV_DIM),
                               lambda b, c: (b, jnp.maximum(c * halo_blocks - 1, 0), 0)),
                  pl.BlockSpec((1, CONV_HALO, MB_CONV_DIM), lambda b, c: (b, 0, 0)),
                  chunk(MB_INNER), chunk(DT_PAD), state] + [full(w) for w in weights],
        out_specs=[chunk(MB_INNER), state],
        out_shape=[jax.ShapeDtypeStruct((batch, seq, MB_INNER), BF16),
                   jax.ShapeDtypeStruct((batch, MB_INNER, MB_STATE), F32)],
        scratch_shapes=[pltpu.VMEM((MB_INNER, MB_STATE), F32)],
        compiler_params=_params("parallel", "arbitrary"),
        name="ssd_scan",
    )(xbc, xbc, cs8, z, dt_raw, h0, *weights)


def prep_mamba(w, j, i):
    row = lambda z: z.reshape(1, -1).astype(F32)
    w_in = w["mamba_w_in"][j]
    padv = lambda z: jnp.concatenate([z.astype(F32), jnp.zeros((DT_PAD - MB_HEADS,), F32)])
    return {
        "norm_mix": w["norm_mix"][i],
        "w_z": w_in[:, :MB_INNER].astype(BF16),
        "w_xbc": w_in[:, MB_INNER:MB_INNER + MB_CONV_DIM].astype(BF16),
        "w_dt": jnp.concatenate([w_in[:, MB_INNER + MB_CONV_DIM:], jnp.zeros((D_MODEL, DT_PAD - MB_HEADS), F32)],
                                axis=1).astype(BF16),
        "conv_w": w["mamba_conv_w"][j], "conv_b": row(w["mamba_conv_b"][j]),
        "dt_bias": row(padv(w["mamba_dt_bias"][j])), "a_log": row(padv(w["mamba_a_log"][j])),
        "d_skip": row(jnp.repeat(w["mamba_d"][j].astype(F32), MB_HEAD)),
        "norm_w": row(w["mamba_norm"][j]),
        "w_o": w["mamba_w_o"][j].astype(BF16),
    }


def mamba_layer(h, conv_state, h0, p, batch, seq, ck):
    m = batch * seq
    assert seq >= MB_CONV - 1
    tm_in = min(TM_FFN, m)
    z = norm_linear(h, p["norm_mix"], p["w_z"], tm_in, 1024)
    xbc = norm_linear(h, p["norm_mix"], p["w_xbc"], tm_in, 1024)
    dt_raw = norm_linear(h, p["norm_mix"], p["w_dt"], tm_in, DT_PAD)
    xbc3 = xbc.reshape(batch, seq, MB_CONV_DIM)
    y, h_new = ssd_scan(xbc3, z.reshape(batch, seq, MB_INNER), dt_raw.reshape(batch, seq, DT_PAD),
                        conv_state, h0.reshape(batch, MB_INNER, MB_STATE), p, batch, seq, ck)
    return (y.reshape(m, MB_INNER), xbc3[:, seq - (MB_CONV - 1):],
            h_new.reshape(batch, MB_HEADS, MB_HEAD, MB_STATE))


N_MIXERS = 3
TM_PROJ = 256
TM_ROWS = 512
TM_FFN = 1024
TH_FFN = 1024
WKV_CHUNK = 64
WKV_PAIRS_PER_STEP = 8
ATTN_TQ = 256
PAGES_PER_STEP = 32


def kernel(x_prompt, x_sample, cache_mla_ckv, cache_mla_kpe, state_rwkv_wkv, state_rwkv_shift, state_ssm, state_conv, page_table, norm_mix, norm_ffn, ffn_w1, ffn_w2, rwkv_mu, rwkv_w_rkv, rwkv_w0, rwkv_w1, rwkv_w2, rwkv_a0, rwkv_a1, rwkv_a2, rwkv_v0, rwkv_v1, rwkv_v2, rwkv_g1, rwkv_g2, rwkv_k_k, rwkv_k_a, rwkv_r_k, rwkv_lnx_w, rwkv_lnx_b, rwkv_w_o, mla_w_in, mla_q_norm, mla_kv_norm, mla_w_uq, mla_w_uk, mla_w_uv, mla_qn_norm, mla_qr_norm, mla_kn_norm, mla_kr_norm, mla_w_o, mamba_w_in, mamba_conv_w, mamba_conv_b, mamba_dt_bias, mamba_a_log, mamba_d, mamba_norm, mamba_w_o):
    w = dict(locals())
    bp, tp, c = x_prompt.shape
    bs, ts, _ = x_sample.shape
    depth = norm_mix.shape[0]
    past_len = page_table.shape[1] * PAGE_SIZE
    hp = x_prompt.reshape(bp * tp, c)
    hs = x_sample.reshape(bs * ts, c)
    vf_p = vf_s = None
    out = {k: [] for k in ("ckv_p", "kpe_p", "ckv_s", "kpe_s", "wkv_p", "sh_p", "wkv_s", "sh_s",
                           "ssm_p", "conv_p", "ssm_s", "conv_s")}
    for i in range(depth):
        kind, j = i % N_MIXERS, i // N_MIXERS
        if kind == 0:
            p = prep_rwkv(w, j, i)
            xp, s_p, l_p, vf_p = rwkv_layer(hp, jnp.zeros((bp, c), F32), jnp.zeros((bp, RW_HEADS, RW_HEAD, RW_HEAD), F32),
                                            vf_p, p, bp, tp, TM_ROWS, WKV_PAIRS_PER_STEP, WKV_CHUNK)
            xs, s_s, l_s, vf_s = rwkv_layer(hs, state_rwkv_shift[j], state_rwkv_wkv[j], vf_s, p, bs, ts,
                                            TM_ROWS, WKV_PAIRS_PER_STEP, ts)
            out["wkv_p"].append(s_p); out["sh_p"].append(l_p); out["wkv_s"].append(s_s); out["sh_s"].append(l_s)
        elif kind == 1:
            p = prep_mla(w, j, i)
            xp, c_p, k_p = mla_layer_prompt(hp, p, bp, tp, TM_PROJ, ATTN_TQ)
            xs, c_s, k_s = mla_layer_sample(hs, cache_mla_ckv[j], cache_mla_kpe[j], page_table, p, bs, ts,
                                            past_len, TM_PROJ, PAGES_PER_STEP)
            out["ckv_p"].append(c_p.reshape(bp, tp, -1)); out["kpe_p"].append(k_p.reshape(bp, tp, -1))
            out["ckv_s"].append(c_s.reshape(bs, ts, -1)); out["kpe_s"].append(k_s.reshape(bs, ts, -1))
        else:
            p = prep_mamba(w, j, i)
            xp, cv_p, h_p = mamba_layer(hp, jnp.zeros((bp, MB_CONV - 1, MB_CONV_DIM), F32),
                                        jnp.zeros((bp, MB_HEADS, MB_HEAD, MB_STATE), F32), p, bp, tp, MB_CHUNK)
            xs, cv_s, h_s = mamba_layer(hs, state_conv[j], state_ssm[j], p, bs, ts, math.gcd(ts, MB_CHUNK))
            out["ssm_p"].append(h_p); out["conv_p"].append(cv_p); out["ssm_s"].append(h_s); out["conv_s"].append(cv_s)
        w1, w2 = ffn_w1[i].astype(BF16), ffn_w2[i].astype(BF16)
        hp = mixer_ffn_res(hp, xp, p["w_o"], norm_ffn[i], w1, w2, TM_FFN, TH_FFN)
        hs = mixer_ffn_res(hs, xs, p["w_o"], norm_ffn[i], w1, w2, TM_FFN, TH_FFN)
    stack = lambda k: jnp.stack(out[k])
    return (hp.reshape(bp, tp, c), hs.reshape(bs, ts, c),
            stack("ckv_p"), stack("kpe_p"), stack("ckv_s"), stack("kpe_s"),
            stack("wkv_p"), stack("sh_p"), stack("wkv_s"), stack("sh_s"),
            stack("ssm_p"), stack("conv_p"), stack("ssm_s"), stack("conv_s"))
```

```python
import functools
import math

import jax
import jax.numpy as jnp
from jax import lax
from jax.experimental import pallas as pl
from jax.experimental.pallas import tpu as pltpu

F32 = jnp.float32
BF16 = jnp.bfloat16

D_MODEL = 1024
NORM_EPS = 1e-6

RW_HEAD = 64
RW_HEADS = D_MODEL // RW_HEAD
RW_LNX_EPS = 64e-5

MLA_HEADS = 16
MLA_Q_LORA = 512
MLA_KV_LORA = 256
MLA_NOPE = 64
MLA_ROPE = 32
MLA_V = 64
MLA_SCALE = 1.0 / math.sqrt(MLA_NOPE + MLA_ROPE)
ROPE_THETA = 10000.0
PAGE_SIZE = 128

MB_INNER = 2 * D_MODEL
MB_HEAD = 64
MB_HEADS = MB_INNER // MB_HEAD
MB_GROUPS = 4
MB_STATE = 128
MB_CONV = 4
MB_CONV_DIM = MB_INNER + 2 * MB_GROUPS * MB_STATE
MB_CHUNK = 128

VMEM_LIMIT_BYTES = 56 * 2**20


def _params(*sem):
    return pltpu.CompilerParams(dimension_semantics=sem, vmem_limit_bytes=VMEM_LIMIT_BYTES)


def _dot(a, b):
    return jnp.dot(a.astype(BF16), b.astype(BF16), preferred_element_type=F32)


def _dot_nt(a, b):
    return lax.dot_general(a.astype(BF16), b.astype(BF16), (((1,), (1,)), ((), ())),
                           preferred_element_type=F32)


def _dot_tn(a, b):
    return lax.dot_general(a.astype(BF16), b.astype(BF16), (((0,), (0,)), ((), ())),
                           preferred_element_type=F32)


def _split3(a):
    hi = a.astype(BF16)
    r1 = a - hi.astype(F32)
    mid = r1.astype(BF16)
    lo = (r1 - mid.astype(F32)).astype(BF16)
    return hi, mid, lo


def _rms(x, g):
    return x * lax.rsqrt(jnp.mean(x * x, axis=-1, keepdims=True) + NORM_EPS) * g


def _softplus(z):
    return jnp.maximum(z, 0.0) + jnp.log(1.0 + jnp.exp(-jnp.abs(z)))


def _sigmoid(z):
    return 0.5 * jnp.tanh(0.5 * z) + 0.5


def _silu(z):
    return z * _sigmoid(z)


def _rmsnorm_kernel(x_ref, g_ref, o_ref):
    o_ref[...] = _rms(x_ref[...], g_ref[...])


def rmsnorm_rows(x, g, tm):
    m, c = x.shape
    return pl.pallas_call(
        _rmsnorm_kernel,
        grid=(m // tm,),
        in_specs=[pl.BlockSpec((tm, c), lambda i: (i, 0)), pl.BlockSpec((1, c), lambda i: (0, 0))],
        out_specs=pl.BlockSpec((tm, c), lambda i: (i, 0)),
        out_shape=jax.ShapeDtypeStruct((m, c), F32),
        compiler_params=_params("parallel"),
        name="rmsnorm_rows",
    )(x, g.reshape(1, c))


def _norm_linear_kernel(x_ref, g_ref, w_ref, o_ref, xn_ref):
    @pl.when(pl.program_id(1) == 0)
    def _():
        xn_ref[...] = _rms(x_ref[...], g_ref[...]).astype(BF16)

    o_ref[...] = jnp.dot(xn_ref[...], w_ref[...], preferred_element_type=F32)


def norm_linear(x, g, w, tm, tn):
    m, k = x.shape
    n = w.shape[1]
    return pl.pallas_call(
        _norm_linear_kernel,
        grid=(m // tm, n // tn),
        in_specs=[pl.BlockSpec((tm, k), lambda i, j: (i, 0)),
                  pl.BlockSpec((1, k), lambda i, j: (0, 0)),
                  pl.BlockSpec((k, tn), lambda i, j: (0, j))],
        out_specs=pl.BlockSpec((tm, tn), lambda i, j: (i, j)),
        out_shape=jax.ShapeDtypeStruct((m, n), F32),
        scratch_shapes=[pltpu.VMEM((tm, k), BF16)],
        compiler_params=_params("parallel", "arbitrary"),
        name="norm_linear",
    )(x, g.reshape(1, k), w)


def _mixer_ffn_kernel(h_ref, x_ref, wo_ref, g_ref, w1_ref, w2_ref, o_ref, xn_ref):
    @pl.when(pl.program_id(1) == 0)
    def _():
        mixed = h_ref[...] + _dot(x_ref[...], wo_ref[...])
        o_ref[...] = mixed
        xn_ref[...] = _rms(mixed, g_ref[...]).astype(BF16)

    u = jnp.dot(xn_ref[...], w1_ref[...], preferred_element_type=F32)
    u = jnp.square(jnp.maximum(u, 0.0))
    o_ref[...] += jnp.dot(u.astype(BF16), w2_ref[...], preferred_element_type=F32)


def mixer_ffn_res(h, x, w_o, g, w1, w2, tm, th):
    m, c = h.shape
    k = x.shape[1]
    hid = w1.shape[1]
    return pl.pallas_call(
        _mixer_ffn_kernel,
        grid=(m // tm, hid // th),
        in_specs=[pl.BlockSpec((tm, c), lambda i, j: (i, 0)),
                  pl.BlockSpec((tm, k), lambda i, j: (i, 0)),
                  pl.BlockSpec((k, c), lambda i, j: (0, 0), pipeline_mode=pl.Buffered(1)),
                  pl.BlockSpec((1, c), lambda i, j: (0, 0)),
                  pl.BlockSpec((c, th), lambda i, j: (0, j)),
                  pl.BlockSpec((th, c), lambda i, j: (j, 0))],
        out_specs=pl.BlockSpec((tm, c), lambda i, j: (i, 0)),
        out_shape=jax.ShapeDtypeStruct((m, c), F32),
        scratch_shapes=[pltpu.VMEM((tm, c), BF16)],
        compiler_params=_params("parallel", "arbitrary"),
        name="mixer_ffn_res",
    )(h, x, w_o, g.reshape(1, c), w1, w2)


SHIFT_HALO = 8


def _rwkv_proj_kernel(has_vres, seq, tm, *refs):
    if has_vres:
        (h_ref, halo_ref, sh_ref, gmix_ref, mu_ref, wrkv_ref, w0_ref, w1_ref, w2_ref, a0_ref, a1_ref, a2_ref,
         g1_ref, g2_ref, v0_ref, v1_ref, v2_ref,
         r_ref, lw_ref, k_ref, v_ref, a_ref, g_ref, vg_ref) = refs
    else:
        (h_ref, halo_ref, sh_ref, gmix_ref, mu_ref, wrkv_ref, w0_ref, w1_ref, w2_ref, a0_ref, a1_ref, a2_ref,
         g1_ref, g2_ref,
         r_ref, lw_ref, k_ref, v_ref, a_ref, g_ref) = refs
    x = _rms(h_ref[...], gmix_ref[...])
    rolled = pltpu.roll(x, 1, axis=0)
    row = lax.broadcasted_iota(jnp.int32, x.shape, 0)
    if seq >= tm:
        is_start = (pl.program_id(0) % (seq // tm)) == 0
        tail = _rms(halo_ref[...], gmix_ref[...])[SHIFT_HALO - 1:SHIFT_HALO, :]
        xprev = jnp.where(row == 0, jnp.where(is_start, sh_ref[0], tail), rolled)
    else:
        n_seq = tm // seq
        starts = jnp.broadcast_to(sh_ref[...][:, None, :], (n_seq, seq, x.shape[1])).reshape(x.shape)
        xprev = jnp.where(row % seq == 0, starts, rolled)
    dx = xprev - x
    xm = [(x + dx * mu_ref[p:p + 1, :]).astype(BF16) for p in range(6)]
    r_ref[...] = jnp.dot(xm[0], wrkv_ref[0], preferred_element_type=F32)
    k_ref[...] = jnp.dot(xm[1], wrkv_ref[1], preferred_element_type=F32)
    v_ref[...] = jnp.dot(xm[2], wrkv_ref[2], preferred_element_type=F32)
    wpre = w0_ref[...] + _dot(jnp.tanh(jnp.dot(xm[3], w1_ref[...], preferred_element_type=F32)), w2_ref[...])
    w_log = -_softplus(-wpre) - 0.5
    lw_ref[...] = -jnp.exp(w_log)
    a_ref[...] = _sigmoid(a0_ref[...] + _dot(jnp.dot(xm[4], a1_ref[...], preferred_element_type=F32), a2_ref[...]))
    g_ref[...] = _dot(_sigmoid(jnp.dot(xm[5], g1_ref[...], preferred_element_type=F32)), g2_ref[...])
    if has_vres:
        vg_ref[...] = _sigmoid(v0_ref[...] + _dot(jnp.dot(xm[2], v1_ref[...], preferred_element_type=F32),
                                                  v2_ref[...]))


def rwkv_proj(h, shift, p, seq, tm):
    m, c = h.shape
    assert tm % SHIFT_HALO == 0 and (seq % tm == 0 or tm % seq == 0)
    has_vres = "v1" in p
    row = pl.BlockSpec((tm, c), lambda i: (i, 0))
    halo = pl.BlockSpec((SHIFT_HALO, c), lambda i: (jnp.maximum(i * (tm // SHIFT_HALO) - 1, 0), 0))
    if seq >= tm:
        shift_arg = shift.reshape(-1, 1, c)
        shift_spec = pl.BlockSpec((1, 1, c), lambda i: (i // (seq // tm), 0, 0))
    else:
        shift_arg = shift
        shift_spec = pl.BlockSpec((tm // seq, c), lambda i: (i, 0))
    full = lambda arr: pl.BlockSpec(arr.shape, lambda i: (0,) * arr.ndim, pipeline_mode=pl.Buffered(1))
    names = ["norm_mix", "mu", "w_rkv", "w0", "w1", "w2", "a0", "a1", "a2", "g1", "g2"]
    if has_vres:
        names += ["v0", "v1", "v2"]
    weights = [p[n] for n in names]
    n_out = 7 if has_vres else 6
    return pl.pallas_call(
        functools.partial(_rwkv_proj_kernel, has_vres, seq, tm),
        grid=(m // tm,),
        in_specs=[row, halo, shift_spec] + [full(w) for w in weights],
        out_specs=[row] * n_out,
        out_shape=[jax.ShapeDtypeStruct((m, c), F32)] * n_out,
        compiler_params=_params("parallel"),
        name="rwkv_proj",
    )(h, h, shift_arg, *weights)


PAIR = 2 * RW_HEAD
WKV_SHORT_SEQS_PER_STEP = 4
WKV_CHUNKS_PER_STEP = 4


def _dot_rhs_hilo(a, b):
    bh = b.astype(BF16)
    bl = (b - bh.astype(F32)).astype(BF16)
    ab = a.astype(BF16)
    return jnp.dot(ab, bh, preferred_element_type=F32) + jnp.dot(ab, bl, preferred_element_type=F32)


def _dot_tn_hilo(a, b):
    tn = (((0,), (0,)), ((), ()))
    ah = a.astype(BF16)
    al = (a - ah.astype(F32)).astype(BF16)
    bh = b.astype(BF16)
    bl = (b - bh.astype(F32)).astype(BF16)
    f = lambda x, y: lax.dot_general(x, y, tn, preferred_element_type=F32)
    return f(ah, bh) + f(ah, bl) + f(al, bh)


def _wkv_kernel(has_vres, nseq, nch, npair, ln, *refs):
    if has_vres:
        (r_ref, lw_ref, k_ref, v_ref, a_ref, g_ref, vf_ref, vg_ref, s0_ref,
         kk_ref, ka_ref, rk_ref, lnw_ref, lnb_ref, y_ref, sout_ref, s_ref) = refs
    else:
        (r_ref, lw_ref, k_ref, v_ref, a_ref, g_ref, s0_ref,
         kk_ref, ka_ref, rk_ref, lnw_ref, lnb_ref, y_ref, sout_ref, s_ref) = refs
    c = pl.program_id(2)
    n = RW_HEAD
    rows = 2 * ln

    @pl.when(c == 0)
    def _():
        zero = jnp.zeros((n, n), F32)
        for q in range(nseq * npair):
            bi, p = divmod(q, npair)
            s_ref[q] = jnp.concatenate([jnp.concatenate([s0_ref[bi, 2 * p], zero], axis=1),
                                        jnp.concatenate([zero, s0_ref[bi, 2 * p + 1]], axis=1)], axis=0)

    first = lax.broadcasted_iota(jnp.int32, (ln, PAIR), 1) < n
    ri = lax.broadcasted_iota(jnp.int32, (rows, rows), 0)
    ci = lax.broadcasted_iota(jnp.int32, (rows, rows), 1)
    same = (ri // ln) == (ci // ln)
    incl = same & (ci <= ri)
    strict = same & (ci < ri)
    eye = jnp.where(ri == ci, 1.0, 0.0)
    ti = lax.broadcasted_iota(jnp.int32, (ln, ln), 0)
    tj = lax.broadcasted_iota(jnp.int32, (ln, ln), 1)
    tri = jnp.where(tj <= ti, 1.0, 0.0).astype(BF16)

    def seg_sum(x):
        s1 = jnp.sum(jnp.where(first, x, 0.0), axis=-1, keepdims=True)
        s2 = jnp.sum(jnp.where(first, 0.0, x), axis=-1, keepdims=True)
        return jnp.where(first, s1, s2)

    def by_head(x):
        return jnp.concatenate([jnp.where(first, x, 0.0), jnp.where(first, 0.0, x)], axis=0)

    pairs = range(nseq * npair)
    lanes = [slice((q % npair) * PAIR, (q % npair + 1) * PAIR) for q in pairs]

    def tok_rows(ch, p):
        first_row = (ch * nseq + p // npair) * ln
        return slice(first_row, first_row + ln)

    def prepare(ch, p):
        here = (tok_rows(ch, p), lanes[p])
        r = r_ref[here]
        lw = lw_ref[here]
        k = k_ref[here]
        v = v_ref[here]
        a = a_ref[here]
        if has_vres:
            v = v + (vf_ref[here] - v) * vg_ref[here]
        kk = k * kk_ref[:, lanes[p]]
        kk = kk / jnp.maximum(jnp.sqrt(seg_sum(kk * kk)), 1e-12)
        k2 = k * (1.0 + (a - 1.0) * ka_ref[:, lanes[p]])
        bv = kk * a
        hi, mid, lo = _split3(lw)
        cl3 = jnp.dot(tri, jnp.concatenate([hi, mid, lo], axis=1), preferred_element_type=F32)
        cl = cl3[:, :PAIR] + cl3[:, PAIR:2 * PAIR] + cl3[:, 2 * PAIR:]
        cl_end = cl[ln - 1:ln, :]
        g_inv = jnp.exp(-cl)
        g_end = jnp.exp(cl_end - cl)
        rt = r * jnp.exp(cl)
        at = -kk * jnp.exp(cl - lw)
        lhs = jnp.concatenate([by_head(at), by_head(rt)], axis=0)
        bh = bv * g_inv
        kh = k2 * g_inv
        return dict(r=r, v=v, k2=k2, lhs=lhs.astype(BF16),
                    rhs=jnp.concatenate([bh, bh, kh, kh], axis=0).astype(BF16),
                    decay=jnp.exp(cl_end), v_h=by_head(v),
                    ws=jnp.concatenate([by_head(bv * g_end), by_head(k2 * g_end)], axis=0))

    def same_block(size):
        return (ri // size) == (ci // size)

    below = ci < ri
    levels = list(range(1, int(math.log2(ln))))

    def couple(st):
        aa = [_dot_nt(st[p]["lhs"], st[p]["rhs"]) for p in pairs]
        a_ak = [jnp.where(strict, aa[p][:rows, rows:], 0.0) for p in pairs]
        a_r = [jnp.concatenate([jnp.where(incl, aa[p][rows:, :rows], 0.0),
                                jnp.where(incl, aa[p][rows:, rows:], 0.0)], axis=1).astype(BF16) for p in pairs]
        akv = [_dot_rhs_hilo(a_ak[p], st[p]["v_h"]) for p in pairs]
        tinv = [eye + jnp.where(below & same_block(2), aa[p][:rows, :rows], 0.0) for p in pairs]
        return dict(n=[aa[p][:rows, :rows] for p in pairs], a_r=a_r, akv=akv, tinv=tinv)

    def merge(cp, level):
        size = 2 ** level
        coupling = below & same_block(2 * size) & jnp.logical_not(same_block(size))
        n_off = [jnp.where(coupling, cp["n"][p], 0.0).astype(BF16) for p in pairs]
        tinv_bf = [cp["tinv"][p].astype(BF16) for p in pairs]
        right = [_dot(n_off[p], tinv_bf[p]) for p in pairs]
        cp["tinv"] = [cp["tinv"][p] + _dot(tinv_bf[p], right[p]) for p in pairs]

    def advance(st, cp):
        s_prev = [s_ref[p] for p in pairs]
        xy = [_dot_nt(st[p]["lhs"], s_prev[p]) for p in pairs]
        u = [_dot_rhs_hilo(cp["tinv"][p], xy[p][:rows] + cp["akv"][p]) for p in pairs]
        zs = [jnp.concatenate([u[p], st[p]["v_h"]], axis=0) for p in pairs]
        for p in pairs:
            s_ref[p] = s_prev[p] * st[p]["decay"] + _dot_tn_hilo(zs[p], st[p]["ws"])
        return [xy[p][rows:] + _dot(cp["a_r"][p], zs[p]) for p in pairs]

    def finish(ch, p, st, y_h):
        y = y_h[p][:ln] + y_h[p][ln:]
        mean = seg_sum(y) * (1.0 / n)
        yc = y - mean
        var = seg_sum(yc * yc) * (1.0 / n)
        y = yc * lax.rsqrt(var + RW_LNX_EPS) * lnw_ref[:, lanes[p]] + lnb_ref[:, lanes[p]]
        y = y + seg_sum(st[p]["r"] * st[p]["k2"] * rk_ref[:, lanes[p]]) * st[p]["v"]
        y_ref[tok_rows(ch, p), lanes[p]] = y * g_ref[tok_rows(ch, p), lanes[p]]

    def shares(items, parts):
        per = -(-len(items) // parts)
        return [items[i * per:(i + 1) * per] for i in range(parts)]

    st = [prepare(0, p) for p in pairs]
    cp = couple(st)
    done = None
    for ch in range(nch):
        st_next = []
        for level, group in zip(levels, shares(list(pairs), len(levels))):
            merge(cp, level)
            if ch + 1 < nch:
                st_next += [prepare(ch + 1, p) for p in group]
            if done is not None:
                for p in group:
                    finish(done[0], p, done[1], done[2])
        cp_next = couple(st_next) if ch + 1 < nch else None
        done = (ch, st, advance(st, cp))
        st, cp = st_next, cp_next
    for p in pairs:
        finish(done[0], p, done[1], done[2])

    @pl.when(c == pl.num_programs(2) - 1)
    def _():
        for q in range(nseq * npair):
            bi, p = divmod(q, npair)
            sout_ref[bi, 2 * p] = s_ref[q, :n, :n]
            sout_ref[bi, 2 * p + 1] = s_ref[q, n:, n:]


def wkv_scan(r, lw, k, v, a, g, s0, p, batch, seq, npair, ln, vres=None):
    nc = seq // ln
    nseq = WKV_SHORT_SEQS_PER_STEP if (nc == 1 and batch % WKV_SHORT_SEQS_PER_STEP == 0) else 1
    has_vres = vres is not None
    width = npair * PAIR
    nch = math.gcd(nc, WKV_CHUNKS_PER_STEP) if nseq == 1 else 1
    steps = nc // nch
    tok = pl.BlockSpec((nseq * nch * ln, width), lambda b, q, c: (b * steps + c, q))
    par = pl.BlockSpec((1, width), lambda b, q, c: (0, q))
    st = pl.BlockSpec((nseq, 2 * npair, RW_HEAD, RW_HEAD), lambda b, q, c: (b, q, 0, 0))
    seqs = [r, lw, k, v, a, g] + (list(vres) if has_vres else [])
    pars = [p[nm].reshape(1, D_MODEL) for nm in ("k_k", "k_a", "r_k", "lnx_w", "lnx_b")]
    return pl.pallas_call(
        functools.partial(_wkv_kernel, has_vres, nseq, nch, npair, ln),
        grid=(batch // nseq, D_MODEL // width, steps),
        in_specs=[tok] * len(seqs) + [st] + [par] * 5,
        out_specs=[tok, st],
        out_shape=[jax.ShapeDtypeStruct(r.shape, F32), jax.ShapeDtypeStruct(s0.shape, F32)],
        scratch_shapes=[pltpu.VMEM((nseq * npair, PAIR, PAIR), F32)],
        compiler_params=_params("parallel", "parallel", "arbitrary"),
        name="wkv_scan",
    )(*seqs, s0, *pars)


def prep_rwkv(w, j, i):
    row = lambda z: z.reshape(1, -1).astype(F32)
    p = {
        "norm_mix": row(w["norm_mix"][i]),
        "mu": w["rwkv_mu"][j],
        "w_rkv": w["rwkv_w_rkv"][j].astype(BF16),
        "w0": row(w["rwkv_w0"][j]), "w1": w["rwkv_w1"][j].astype(BF16), "w2": w["rwkv_w2"][j].astype(BF16),
        "a0": row(w["rwkv_a0"][j]), "a1": w["rwkv_a1"][j].astype(BF16), "a2": w["rwkv_a2"][j].astype(BF16),
        "g1": w["rwkv_g1"][j].astype(BF16), "g2": w["rwkv_g2"][j].astype(BF16),
        "k_k": w["rwkv_k_k"][j], "k_a": w["rwkv_k_a"][j], "r_k": w["rwkv_r_k"][j],
        "lnx_w": w["rwkv_lnx_w"][j], "lnx_b": w["rwkv_lnx_b"][j],
        "w_o": w["rwkv_w_o"][j].astype(BF16),
    }
    if j > 0:
        p["v0"] = row(w["rwkv_v0"][j - 1])
        p["v1"] = w["rwkv_v1"][j - 1].astype(BF16)
        p["v2"] = w["rwkv_v2"][j - 1].astype(BF16)
    return p


def rwkv_layer(h, shift, s0, vfirst, p, batch, seq, tm, npair, ln):
    m, c = h.shape
    outs = rwkv_proj(h, shift, p, seq, tm)
    r, lw, k, v, a, g = outs[:6]
    vres = None if vfirst is None else (vfirst, outs[6])
    y, s_new = wkv_scan(r, lw, k, v, a, g, s0, p, batch, seq, npair, ln, vres)
    shift_new = rmsnorm_rows(h.reshape(batch, seq, c)[:, -1], p["norm_mix"], batch)
    return y, s_new, shift_new, (v if vfirst is None else vfirst)


SEG_PAD = 128


def _dot_hilo_rhs(x, ones):
    hi = x.astype(BF16)
    lo = (x - hi.astype(F32)).astype(BF16)
    return jnp.dot(hi, ones, preferred_element_type=F32) + jnp.dot(lo, ones, preferred_element_type=F32)


def _seg_rms_scale(x, seg_ref, segt_ref, invw_ref):
    ssq = _dot(x * x, seg_ref[...])
    return _dot_hilo_rhs(lax.rsqrt(ssq * invw_ref[...] + NORM_EPS), segt_ref[...])


def _swap_halves(x, group):
    half = group // 2
    width = x.shape[-1]
    lane = lax.broadcasted_iota(jnp.int32, x.shape, x.ndim - 1)
    return jnp.where((lane % group) < half,
                     pltpu.roll(x, width - half, axis=x.ndim - 1),
                     pltpu.roll(x, half, axis=x.ndim - 1))


SLOT = 128


def _mla_proj_kernel(with_kv, *refs):
    (h_ref, cos_ref, sin_ref, gmix_ref, win_ref, gq_ref, gkv_ref, gkr_ref, wuq_ref, gqc_ref,
     seg_ref, segt_ref, invw_ref) = refs[:13]
    if with_kv:
        wuk_ref, wuv_ref, gkn_ref, qc_ref, c_ref, kp_ref, kc_ref, v_ref = refs[13:]
    else:
        qc_ref, c_ref, kp_ref = refs[13:]
    x = _rms(h_ref[...], gmix_ref[...])
    hp = _dot(x, win_ref[...])
    q_a = _rms(hp[:, :MLA_Q_LORA], gq_ref[...])
    c = _rms(hp[:, MLA_Q_LORA:MLA_Q_LORA + MLA_KV_LORA], gkv_ref[...])
    c_ref[...] = c

    cos = cos_ref[...]
    sin = sin_ref[...]

    def rope(z, cs, sn):
        return z * cs + _swap_halves(z, MLA_ROPE) * sn

    kp_raw = hp[:, MLA_Q_LORA + MLA_KV_LORA:]
    kp_scale = lax.rsqrt(jnp.sum(kp_raw * kp_raw, axis=-1, keepdims=True) * (1.0 / MLA_ROPE) + NORM_EPS)
    kp = rope(pltpu.roll(kp_raw, MLA_NOPE, axis=1) * gkr_ref[...], cos, sin) * kp_scale
    kp_ref[...] = kp[:, MLA_NOPE:MLA_NOPE + MLA_ROPE]

    q = _dot(q_a, wuq_ref[...])
    cos_h = jnp.tile(cos, (1, MLA_HEADS))
    sin_h = jnp.tile(sin, (1, MLA_HEADS))
    qc = rope(q * gqc_ref[...], cos_h, sin_h) * _seg_rms_scale(q, seg_ref, segt_ref, invw_ref)
    qc_ref[...] = qc.astype(BF16)
    if with_kv:
        kraw = _dot(c, wuk_ref[...])
        kn = kraw * _seg_rms_scale(kraw, seg_ref, segt_ref, invw_ref) * gkn_ref[...]
        kc_ref[...] = (kn + jnp.tile(kp, (1, MLA_HEADS))).astype(BF16)
        v_ref[0] = _dot_nt(wuv_ref[...], c).astype(BF16)


def _slot_seg_matrix():
    lane = jnp.arange(MLA_HEADS * SLOT)
    head, off = lane // SLOT, lane % SLOT
    col = jnp.where(off < MLA_NOPE, head, jnp.where(off < MLA_NOPE + MLA_ROPE, MLA_HEADS + head, SEG_PAD))
    return (col[:, None] == jnp.arange(SEG_PAD)[None, :]).astype(BF16)


def _slot(nope, rope):
    pad = jnp.zeros(nope.shape[:-1] + (SLOT - MLA_NOPE - MLA_ROPE,), nope.dtype)
    return jnp.concatenate([nope, rope, pad], axis=-1)


def _rope_tables(pos):
    half = MLA_ROPE // 2
    inv = ROPE_THETA ** (-jnp.arange(half, dtype=F32) / half)
    ang = pos.astype(F32)[:, None] * inv[None, :]
    cos = jnp.cos(ang)
    sin = jnp.sin(ang)
    ones = jnp.ones((pos.shape[0], MLA_NOPE), F32)
    return (_slot(ones, jnp.concatenate([cos, cos], axis=1)),
            _slot(jnp.zeros_like(ones), jnp.concatenate([-sin, sin], axis=1)))


def mla_proj(h, pos, p, tm, with_kv):
    m, c = h.shape
    seq = pos.shape[0]
    cos, sin = _rope_tables(pos)
    if seq >= tm:
        nrep = seq // tm
    else:
        cos, sin = jnp.tile(cos, (tm // seq, 1)), jnp.tile(sin, (tm // seq, 1))
        nrep = 1
    row = lambda width: pl.BlockSpec((tm, width), lambda i: (i, 0))
    tab = pl.BlockSpec((tm, SLOT), lambda i: (i % nrep, 0))
    full = lambda arr: pl.BlockSpec(arr.shape, lambda i: (0,) * arr.ndim)
    seg = _slot_seg_matrix()
    invw = jnp.concatenate([jnp.full((MLA_HEADS,), 1.0 / MLA_NOPE, F32), jnp.full((MLA_HEADS,), 1.0 / MLA_ROPE, F32),
                            jnp.ones((SEG_PAD - 2 * MLA_HEADS,), F32)]).reshape(1, SEG_PAD)
    weights = [p["norm_mix"], p["w_in"], p["q_norm"], p["kv_norm"], p["kr_norm"], p["w_uq"],
               p["q_gain"] if with_kv else p["q_gain_abs"], seg, seg.T, invw]
    wide = MLA_HEADS * SLOT
    out_shape = [jax.ShapeDtypeStruct((m, wide), BF16),
                 jax.ShapeDtypeStruct((m, MLA_KV_LORA), F32), jax.ShapeDtypeStruct((m, MLA_ROPE), F32)]
    out_specs = [row(wide), row(MLA_KV_LORA), row(MLA_ROPE)]
    if with_kv:
        weights += [p["w_uk"], p["w_uv_t"], p["kn_gain"]]
        out_shape += [jax.ShapeDtypeStruct((m, wide), BF16),
                      jax.ShapeDtypeStruct((m // tm, MLA_HEADS * MLA_V, tm), BF16)]
        out_specs += [row(wide), pl.BlockSpec((1, MLA_HEADS * MLA_V, tm), lambda i: (i, 0, 0))]
    return pl.pallas_call(
        functools.partial(_mla_proj_kernel, with_kv),
        grid=(m // tm,),
        in_specs=[row(c), tab, tab] + [full(w) for w in weights],
        out_specs=out_specs,
        out_shape=out_shape,
        compiler_params=_params("parallel"),
        name="mla_proj",
    )(h, cos, sin, *weights)


def prep_mla(w, j, i):
    row = lambda z: z.reshape(1, -1).astype(F32)
    w_in = w["mla_w_in"][j]
    pad = jnp.zeros((D_MODEL, 128 - MLA_ROPE), F32)
    w_uq = w["mla_w_uq"][j]
    w_uk = w["mla_w_uk"][j]
    zero_n = jnp.zeros((MLA_NOPE,), F32)
    zero_r = jnp.zeros((MLA_ROPE,), F32)
    qn, qr, kn = w["mla_qn_norm"][j], w["mla_qr_norm"][j], w["mla_kn_norm"][j]
    return {
        "norm_mix": row(w["norm_mix"][i]),
        "w_in": jnp.concatenate([w_in, pad], axis=1).astype(BF16),
        "q_norm": row(w["mla_q_norm"][j]), "kv_norm": row(w["mla_kv_norm"][j]),
        "kr_norm": row(_slot(zero_n, w["mla_kr_norm"][j])),
        "w_uq": _slot(w_uq[:, :, :MLA_NOPE], w_uq[:, :, MLA_NOPE:]).reshape(MLA_Q_LORA, -1).astype(BF16),
        "q_gain": row(jnp.tile(_slot(qn, qr) * MLA_SCALE, MLA_HEADS)),
        "q_gain_abs": row(jnp.tile(_slot(qn * kn, qr) * MLA_SCALE, MLA_HEADS)),
        "kn_gain": row(jnp.tile(_slot(kn, zero_r), MLA_HEADS)),
        "w_uk": _slot(w_uk, jnp.zeros(w_uk.shape[:2] + (MLA_ROPE,), F32)).reshape(MLA_KV_LORA, -1).astype(BF16),
        "w_uv": w["mla_w_uv"][j].reshape(MLA_KV_LORA, -1).astype(BF16),
        "w_uv_t": w["mla_w_uv"][j].reshape(MLA_KV_LORA, -1).T.astype(BF16),
        "w_uk_t": w["mla_w_uk"][j].reshape(MLA_KV_LORA, -1).T.astype(BF16),
        "w_uk_heads": jnp.transpose(w["mla_w_uk"][j], (1, 2, 0)).astype(BF16),
        "w_o": w["mla_w_o"][j].astype(BF16),
    }


ATTN_HEADS_PER_STEP = 16
NEG_BIG = -1e30


def _flash_kernel(tq, qc_ref, kc_ref, vt_ref, o_ref, m_ref, l_ref, acc_ref):
    qi = pl.program_id(2)
    g = ATTN_HEADS_PER_STEP
    heads = range(g)
    key_i = lax.broadcasted_iota(jnp.int32, (tq, tq), 0)
    qry_i = lax.broadcasted_iota(jnp.int32, (tq, tq), 1)
    m_ref[...] = jnp.full_like(m_ref, NEG_BIG)
    l_ref[...] = jnp.zeros_like(l_ref)
    acc_ref[...] = jnp.zeros_like(acc_ref)

    def block(j, masked):
        keys = pl.ds(pl.multiple_of(j * tq, tq), tq)
        s = [_dot_nt(kc_ref[0, keys, hh * SLOT:(hh + 1) * SLOT], qc_ref[0, :, hh * SLOT:(hh + 1) * SLOT])
             for hh in heads]
        if masked:
            s = [jnp.where(key_i <= qry_i, s[hh], NEG_BIG) for hh in heads]
        m_prev = [m_ref[hh] for hh in heads]
        m_new = [jnp.maximum(m_prev[hh], jnp.max(s[hh], axis=0, keepdims=True)) for hh in heads]
        alpha = [jnp.exp(m_prev[hh] - m_new[hh]) for hh in heads]
        pr = [jnp.exp(s[hh] - m_new[hh]) for hh in heads]
        pv = [jnp.dot(vt_ref[j, hh * MLA_V:(hh + 1) * MLA_V, :], pr[hh].astype(BF16), preferred_element_type=F32)
              for hh in heads]
        for hh in heads:
            l_ref[hh] = alpha[hh] * l_ref[hh] + jnp.sum(pr[hh], axis=0, keepdims=True)
            acc_ref[hh] = alpha[hh] * acc_ref[hh] + pv[hh]
            m_ref[hh] = m_new[hh]

    def body(j, carry):
        block(j, False)
        return carry

    lax.fori_loop(0, qi, body, 0)
    block(qi, True)
    o_t = jnp.concatenate([acc_ref[hh] / l_ref[hh] for hh in heads], axis=0)
    o_ref[0] = o_t.T


def flash_prompt(qc, kc, vt, batch, seq, tq):
    g = ATTN_HEADS_PER_STEP
    nq = seq // tq
    r3 = lambda z: z.reshape(batch, seq, -1)
    return pl.pallas_call(
        functools.partial(_flash_kernel, tq),
        grid=(batch, MLA_HEADS // g, nq),
        in_specs=[pl.BlockSpec((1, tq, g * SLOT), lambda b, hq, i: (b, i, hq)),
                  pl.BlockSpec((1, seq, g * SLOT), lambda b, hq, i: (b, 0, hq)),
                  pl.BlockSpec((nq, g * MLA_V, tq), lambda b, hq, i: (b, hq, 0))],
        out_specs=pl.BlockSpec((1, tq, g * MLA_V), lambda b, hq, i: (b, i, hq)),
        out_shape=jax.ShapeDtypeStruct((batch, seq, MLA_HEADS * MLA_V), F32),
        scratch_shapes=[pltpu.VMEM((g, 1, tq), F32), pltpu.VMEM((g, 1, tq), F32),
                        pltpu.VMEM((g, MLA_V, tq), F32)],
        compiler_params=_params("parallel", "parallel", "parallel"),
        name="flash_prompt",
    )(r3(qc), r3(kc), vt).reshape(batch * seq, -1)


def _bmm_kernel(a_ref, b_ref, o_ref):
    o_ref[0] = _dot(a_ref[0], b_ref[0]).astype(o_ref.dtype)


def bmm(a, b, out_dtype):
    g, m, k = a.shape
    n = b.shape[2]
    return pl.pallas_call(
        _bmm_kernel,
        grid=(g,),
        in_specs=[pl.BlockSpec((1, m, k), lambda i: (i, 0, 0)), pl.BlockSpec((1, k, n), lambda i: (i, 0, 0))],
        out_specs=pl.BlockSpec((1, m, n), lambda i: (i, 0, 0)),
        out_shape=jax.ShapeDtypeStruct((g, m, n), out_dtype),
        compiler_params=_params("parallel"),
        name="bmm",
    )(a, b)


def _paged_attn_kernel(pp, seq, n_pages, pt_ref, qa_ref, qp_ref, cn_ref, kpn_ref, wukt_ref, wuv_ref,
                       pool_c_ref, pool_kp_ref, o_ref, m_ref, l_ref, acc_ref, lhs_ref, c_buf, kp_buf, sems):
    b = pl.program_id(0)
    n_groups = n_pages // pp
    nq = seq * MLA_HEADS
    nup = MLA_HEADS * MLA_NOPE

    def group_copies(bb, g, slot):
        out = []
        for i in range(pp):
            page = pt_ref[bb * n_pages + g * pp + i]
            out.append(pltpu.make_async_copy(pool_c_ref.at[page], c_buf.at[slot, i], sems.at[0, slot]))
            out.append(pltpu.make_async_copy(pool_kp_ref.at[page], kp_buf.at[slot, i], sems.at[1, slot]))
        return out

    @pl.when(b == 0)
    def _():
        for cp in group_copies(b, 0, 0):
            cp.start()

    m_ref[...] = jnp.full_like(m_ref, NEG_BIG)
    l_ref[...] = jnp.zeros_like(l_ref)
    acc_ref[...] = jnp.zeros_like(acc_ref)
    lhs_ref[:nup, :] = wukt_ref[...]
    lhs_ref[nup:, :] = qa_ref[0]

    qp = qp_ref[0]

    def attend(c_blks, kp_t_blks, mask):
        subs = range(len(c_blks))
        nk = c_blks[0].shape[0]
        c_bf = [c_blks[i].astype(BF16) for i in subs]
        both = [_dot_nt(lhs_ref[...], c_bf[i]) for i in subs]
        ssq = [jnp.sum(jnp.square(both[i][:nup]).reshape(MLA_HEADS, MLA_NOPE, nk), axis=1) for i in subs]
        rs = [lax.rsqrt(ssq[i] * (1.0 / MLA_NOPE) + NORM_EPS) for i in subs]
        s = [both[i][nup:] * jnp.concatenate([rs[i]] * seq, axis=0) + _dot(qp, kp_t_blks[i]) for i in subs]
        if mask is not None:
            s = [jnp.where(mask, s[i], NEG_BIG) for i in subs]
        m_prev = m_ref[...]
        m_new = m_prev
        for i in subs:
            m_new = jnp.maximum(m_new, jnp.max(s[i], axis=-1, keepdims=True))
        alpha = jnp.exp(m_prev - m_new)
        pr = [jnp.exp(s[i] - m_new) for i in subs]
        l_new = alpha * l_ref[...]
        acc = alpha * acc_ref[...]
        for i in subs:
            l_new = l_new + jnp.sum(pr[i], axis=-1, keepdims=True)
            acc = acc + _dot(pr[i], c_bf[i])
        l_ref[...] = l_new
        acc_ref[...] = acc
        m_ref[...] = m_new

    pages_per_sub = 2

    def group_step(g, carry):
        slot = (b * n_groups + g) % 2
        last = g == n_groups - 1
        next_b = jnp.where(last, b + 1, b)
        next_g = jnp.where(last, 0, g + 1)

        @pl.when(next_b < pl.num_programs(0))
        def _():
            for cp in group_copies(next_b, next_g, 1 - slot):
                cp.start()

        for cp in group_copies(b, g, slot):
            cp.wait()
        subs = range(0, pp, pages_per_sub)
        attend([c_buf[slot, pl.ds(i, pages_per_sub)].reshape(pages_per_sub * PAGE_SIZE, MLA_KV_LORA) for i in subs],
               [jnp.concatenate([kp_buf[slot, i + k] for k in range(pages_per_sub)], axis=1) for i in subs], None)
        return carry

    lax.fori_loop(0, n_groups, group_step, 0)

    qtok = lax.broadcasted_iota(jnp.int32, (nq, seq), 0) // MLA_HEADS
    ktok = lax.broadcasted_iota(jnp.int32, (nq, seq), 1)
    attend([cn_ref[0]], [kpn_ref[0]], ktok <= qtok)
    o_lat = acc_ref[...] / l_ref[...]
    full = _dot(o_lat, wuv_ref[...])
    rhead = lax.broadcasted_iota(jnp.int32, full.shape, 0) % MLA_HEADS
    lhead = lax.broadcasted_iota(jnp.int32, full.shape, 1) // MLA_V
    full = jnp.where(rhead == lhead, full, 0.0)
    o_ref[0] = jnp.sum(full.reshape(seq, MLA_HEADS, MLA_HEADS * MLA_V), axis=1)


def paged_attn(q_abs, qp, c_new, kp_new, pool_c, pool_kp_t, page_table, p, batch, seq, pp):
    n_pages = page_table.shape[1]
    assert n_pages % pp == 0 and pp % 2 == 0
    nq = seq * MLA_HEADS
    per_b = lambda shp: pl.BlockSpec((1,) + shp, lambda b, pt: (b, 0, 0))
    full = lambda arr: pl.BlockSpec(arr.shape, lambda b, pt: (0,) * arr.ndim)
    in_hbm = pl.BlockSpec(memory_space=pl.ANY)
    grid_spec = pltpu.PrefetchScalarGridSpec(
        num_scalar_prefetch=1,
        grid=(batch,),
        in_specs=[per_b((nq, MLA_KV_LORA)), per_b((nq, MLA_ROPE)), per_b((seq, MLA_KV_LORA)),
                  per_b((MLA_ROPE, seq)), full(p["w_uk_t"]), full(p["w_uv"]), in_hbm, in_hbm],
        out_specs=per_b((seq, MLA_HEADS * MLA_V)),
        scratch_shapes=[pltpu.VMEM((nq, 1), F32), pltpu.VMEM((nq, 1), F32), pltpu.VMEM((nq, MLA_KV_LORA), F32),
                        pltpu.VMEM((MLA_HEADS * MLA_NOPE + nq, MLA_KV_LORA), BF16),
                        pltpu.VMEM((2, pp, PAGE_SIZE, MLA_KV_LORA), F32),
                        pltpu.VMEM((2, pp, MLA_ROPE, PAGE_SIZE), F32),
                        pltpu.SemaphoreType.DMA((2, 2))],
    )
    return pl.pallas_call(
        functools.partial(_paged_attn_kernel, pp, seq, n_pages),
        grid_spec=grid_spec,
        out_shape=jax.ShapeDtypeStruct((batch, seq, MLA_HEADS * MLA_V), F32),
        compiler_params=_params("arbitrary"),
        name="paged_attn",
    )(page_table.reshape(-1), q_abs, qp, c_new.reshape(batch, seq, -1),
      jnp.swapaxes(kp_new.reshape(batch, seq, -1), 1, 2),
      p["w_uk_t"], p["w_uv"], pool_c, pool_kp_t)


def mla_layer_prompt(h, p, batch, seq, tm, tq):
    assert tm == tq, "the value tiles written by mla_proj are the key blocks of flash_prompt"
    qc, c, kp, kc, vt = mla_proj(h, jnp.arange(seq), p, tm, True)
    return flash_prompt(qc, kc, vt, batch, seq, tq), c, kp


def mla_layer_sample(h, pool_c, pool_kp, page_table, p, batch, seq, past_len, tm, pp):
    m = batch * seq
    qc, c, kp = mla_proj(h, past_len + jnp.arange(seq), p, tm, False)
    qc = qc.reshape(m, MLA_HEADS, SLOT)
    q_heads = jnp.swapaxes(qc[:, :, :MLA_NOPE], 0, 1)
    q_abs = bmm(q_heads, p["w_uk_heads"], BF16)
    q_abs = jnp.swapaxes(q_abs, 0, 1).reshape(batch, seq * MLA_HEADS, MLA_KV_LORA)
    qp = qc[:, :, MLA_NOPE:MLA_NOPE + MLA_ROPE].reshape(batch, seq * MLA_HEADS, MLA_ROPE)
    o = paged_attn(q_abs, qp, c, kp, pool_c, jnp.swapaxes(pool_kp, 1, 2), page_table, p, batch, seq, pp)
    return o.reshape(m, -1), c, kp


DT_PAD = 128
CONV_HALO = 8
MB_GN = MB_GROUPS * MB_STATE
MB_GROUP_INNER = MB_INNER // MB_GROUPS
MB_HEADS_PER_GROUP = MB_HEADS // MB_GROUPS


def _ssd_kernel(ck, xbc_ref, prev_ref, cs_ref, z_ref, dtr_ref, h0_ref, cw_ref, cb_ref, dtb_ref, alog_ref,
                dskip_ref, nw_ref, expand_ref, y_ref, hout_ref, h_ref):
    c = pl.program_id(1)

    @pl.when(c == 0)
    def _():
        h_ref[...] = h0_ref[0]

    halo = jnp.where(c == 0, cs_ref[0], prev_ref[0])
    xext = jnp.concatenate([halo, xbc_ref[0]], axis=0)
    conv = cb_ref[...] + xext[CONV_HALO:, :] * cw_ref[MB_CONV - 1:MB_CONV, :]
    for j in range(MB_CONV - 1):
        shifted = pltpu.roll(xext, MB_CONV - 1 - j, axis=0)[CONV_HALO:, :]
        conv = conv + shifted * cw_ref[j:j + 1, :]
    xbc = _silu(conv)
    xs = xbc[:, :MB_INNER]
    bm = xbc[:, MB_INNER:MB_INNER + MB_GN]
    cm = xbc[:, MB_INNER + MB_GN:]

    dt = _softplus(dtr_ref[0] + dtb_ref[...])
    da = dt * (-jnp.exp(alog_ref[...]))
    ri = lax.broadcasted_iota(jnp.int32, (ck, ck), 0)
    ci = lax.broadcasted_iota(jnp.int32, (ck, ck), 1)
    causal = ci <= ri
    tri = jnp.where(causal, 1.0, 0.0).astype(BF16)
    hi, mid, lo = _split3(da)
    acum = (jnp.dot(tri, hi, preferred_element_type=F32) + jnp.dot(tri, mid, preferred_element_type=F32)
            + jnp.dot(tri, lo, preferred_element_type=F32))
    tn = (((0,), (0,)), ((), ()))
    tri_t = jnp.where(ri <= ci, 1.0, 0.0).astype(BF16)
    acum_t = (lax.dot_general(hi, tri_t, tn, preferred_element_type=F32)
              + lax.dot_general(mid, tri_t, tn, preferred_element_type=F32)
              + lax.dot_general(lo, tri_t, tn, preferred_element_type=F32))
    e_last = jnp.exp(acum[ck - 1:ck, :])

    spread = expand_ref[...]
    dt_hi = dt.astype(BF16)
    dt_lo = (dt - dt_hi.astype(F32)).astype(BF16)
    dt_x = jnp.dot(dt_hi, spread, preferred_element_type=F32) + jnp.dot(dt_lo, spread, preferred_element_type=F32)
    ah, am, al = _split3(acum)
    acum_x = (jnp.dot(ah, spread, preferred_element_type=F32) + jnp.dot(am, spread, preferred_element_type=F32)
              + jnp.dot(al, spread, preferred_element_type=F32))
    xdt = xs * dt_x
    xdt_end = xdt * jnp.exp(acum_x[ck - 1:ck, :] - acum_x)
    e_cum_x = jnp.exp(acum_x)

    pair_w = 2 * MB_HEAD
    first = lax.broadcasted_iota(jnp.int32, (ck, pair_w), 1) < MB_HEAD
    upper = lax.broadcasted_iota(jnp.int32, (pair_w, MB_STATE), 0) < MB_HEAD
    pairs_per_group = MB_HEADS_PER_GROUP // 2
    ys = []
    for g in range(MB_GROUPS):
        b_g = bm[:, g * MB_STATE:(g + 1) * MB_STATE]
        c_g = cm[:, g * MB_STATE:(g + 1) * MB_STATE]
        cb = _dot_nt(c_g, b_g)
        cols = slice(g * MB_GROUP_INNER, (g + 1) * MB_GROUP_INNER)
        y_state = _dot_nt(c_g, h_ref[cols, :]) * e_cum_x[:, cols]
        lmat = []
        for hh in range(MB_HEADS_PER_GROUP):
            hd = g * MB_HEADS_PER_GROUP + hh
            seg = acum[:, hd:hd + 1] - acum_t[hd:hd + 1, :]
            lmat.append(cb * jnp.where(causal, jnp.exp(jnp.where(causal, seg, 0.0)), 0.0))
        for pr in range(pairs_per_group):
            lanes = slice(g * MB_GROUP_INNER + pr * pair_w, g * MB_GROUP_INNER + (pr + 1) * pair_w)
            x_p = xdt[:, lanes]
            y_p = jnp.where(first, _dot(lmat[2 * pr], x_p), _dot(lmat[2 * pr + 1], x_p))
            ys.append(y_p + y_state[:, pr * pair_w:(pr + 1) * pair_w])
            hd = g * MB_HEADS_PER_GROUP + 2 * pr
            decay = jnp.where(upper, e_last[:, hd:hd + 1], e_last[:, hd + 1:hd + 2])
            h_ref[lanes, :] = h_ref[lanes, :] * decay + _dot_tn(xdt_end[:, lanes], b_g)

    y = jnp.concatenate(ys, axis=1) + dskip_ref[...] * xs
    yz = y * _silu(z_ref[0])
    outs = []
    for g in range(MB_GROUPS):
        yg = yz[:, g * MB_GROUP_INNER:(g + 1) * MB_GROUP_INNER]
        outs.append(yg * lax.rsqrt(jnp.mean(yg * yg, axis=-1, keepdims=True) + NORM_EPS))
    y_ref[0] = (jnp.concatenate(outs, axis=1) * nw_ref[...]).astype(y_ref.dtype)

    @pl.when(c == pl.num_programs(1) - 1)
    def _():
        hout_ref[0] = h_ref[...]


def ssd_scan(xbc, z, dt_raw, conv_state, h0, p, batch, seq, ck):
    assert CONV_HALO % 8 == 0 and ck % CONV_HALO == 0
    nc = seq // ck
    halo_blocks = ck // CONV_HALO
    cs8 = jnp.concatenate([jnp.zeros((batch, CONV_HALO - (MB_CONV - 1), MB_CONV_DIM), F32), conv_state], axis=1)
    chunk = lambda width: pl.BlockSpec((1, ck, width), lambda b, c: (b, c, 0))
    full = lambda arr: pl.BlockSpec(arr.shape, lambda b, c: (0,) * arr.ndim)
    expand = (jnp.arange(DT_PAD)[:, None] == jnp.arange(MB_INNER)[None, :] // MB_HEAD).astype(BF16)
    weights = [p["conv_w"], p["conv_b"], p["dt_bias"], p["a_log"], p["d_skip"], p["norm_w"], expand]
    state = pl.BlockSpec((1, MB_INNER, MB_STATE), lambda b, c: (b, 0, 0))
    return pl.pallas_call(
        functools.partial(_ssd_kernel, ck),
        grid=(batch, nc),
        in_specs=[chunk(MB_CONV_DIM),
                  pl.BlockSpec((1, CONV_HALO, MB_CONV_DIM),
                               lambda b, c: (b, jnp.maximum(c * halo_blocks - 1, 0), 0)),
                  pl.BlockSpec((1, CONV_HALO, MB_CONV_DIM), lambda b, c: (b, 0, 0)),
                  chunk(MB_INNER), chunk(DT_PAD), state] + [full(w) for w in weights],
        out_specs=[chunk(MB_INNER), state],
        out_shape=[jax.ShapeDtypeStruct((batch, seq, MB_INNER), BF16),
                   jax.ShapeDtypeStruct((batch, MB_INNER, MB_STATE), F32)],
        scratch_shapes=[pltpu.VMEM((MB_INNER, MB_STATE), F32)],
        compiler_params=_params("parallel", "arbitrary"),
        name="ssd_scan",
    )(xbc, xbc, cs8, z, dt_raw, h0, *weights)


def prep_mamba(w, j, i):
    row = lambda z: z.reshape(1, -1).astype(F32)
    w_in = w["mamba_w_in"][j]
    padv = lambda z: jnp.concatenate([z.astype(F32), jnp.zeros((DT_PAD - MB_HEADS,), F32)])
    return {
        "norm_mix": w["norm_mix"][i],
        "w_z": w_in[:, :MB_INNER].astype(BF16),
        "w_xbc": w_in[:, MB_INNER:MB_INNER + MB_CONV_DIM].astype(BF16),
        "w_dt": jnp.concatenate([w_in[:, MB_INNER + MB_CONV_DIM:], jnp.zeros((D_MODEL, DT_PAD - MB_HEADS), F32)],
                                axis=1).astype(BF16),
        "conv_w": w["mamba_conv_w"][j], "conv_b": row(w["mamba_conv_b"][j]),
        "dt_bias": row(padv(w["mamba_dt_bias"][j])), "a_log": row(padv(w["mamba_a_log"][j])),
        "d_skip": row(jnp.repeat(w["mamba_d"][j].astype(F32), MB_HEAD)),
        "norm_w": row(w["mamba_norm"][j]),
        "w_o": w["mamba_w_o"][j].astype(BF16),
    }


def mamba_layer(h, conv_state, h0, p, batch, seq, ck):
    m = batch * seq
    assert seq >= MB_CONV - 1
    tm_in = min(TM_FFN, m)
    z = norm_linear(h, p["norm_mix"], p["w_z"], tm_in, TN_IN)
    xbc = norm_linear(h, p["norm_mix"], p["w_xbc"], tm_in, TN_IN)
    dt_raw = norm_linear(h, p["norm_mix"], p["w_dt"], tm_in, DT_PAD)
    xbc3 = xbc.reshape(batch, seq, MB_CONV_DIM)
    y, h_new = ssd_scan(xbc3, z.reshape(batch, seq, MB_INNER), dt_raw.reshape(batch, seq, DT_PAD),
                        conv_state, h0.reshape(batch, MB_INNER, MB_STATE), p, batch, seq, ck)
    return (y.reshape(m, MB_INNER), xbc3[:, seq - (MB_CONV - 1):],
            h_new.reshape(batch, MB_HEADS, MB_HEAD, MB_STATE))


N_MIXERS = 3
TM_PROJ = 256
TM_ROWS = 512
TM_FFN = 1024
TH_FFN = 1024
TN_IN = 1024
WKV_CHUNK = 64
WKV_PAIRS_PER_STEP = 8
ATTN_TQ = 256
PAGES_PER_STEP = 32


def kernel(x_prompt, x_sample, cache_mla_ckv, cache_mla_kpe, state_rwkv_wkv, state_rwkv_shift, state_ssm, state_conv, page_table, norm_mix, norm_ffn, ffn_w1, ffn_w2, rwkv_mu, rwkv_w_rkv, rwkv_w0, rwkv_w1, rwkv_w2, rwkv_a0, rwkv_a1, rwkv_a2, rwkv_v0, rwkv_v1, rwkv_v2, rwkv_g1, rwkv_g2, rwkv_k_k, rwkv_k_a, rwkv_r_k, rwkv_lnx_w, rwkv_lnx_b, rwkv_w_o, mla_w_in, mla_q_norm, mla_kv_norm, mla_w_uq, mla_w_uk, mla_w_uv, mla_qn_norm, mla_qr_norm, mla_kn_norm, mla_kr_norm, mla_w_o, mamba_w_in, mamba_conv_w, mamba_conv_b, mamba_dt_bias, mamba_a_log, mamba_d, mamba_norm, mamba_w_o):
    w = dict(locals())
    bp, tp, c = x_prompt.shape
    bs, ts, _ = x_sample.shape
    depth = norm_mix.shape[0]
    past_len = page_table.shape[1] * PAGE_SIZE
    hp = x_prompt.reshape(bp * tp, c)
    hs = x_sample.reshape(bs * ts, c)
    vf_p = vf_s = None
    out = {k: [] for k in ("ckv_p", "kpe_p", "ckv_s", "kpe_s", "wkv_p", "sh_p", "wkv_s", "sh_s",
                           "ssm_p", "conv_p", "ssm_s", "conv_s")}
    for i in range(depth):
        kind, j = i % N_MIXERS, i // N_MIXERS
        if kind == 0:
            p = prep_rwkv(w, j, i)
            xp, s_p, l_p, vf_p = rwkv_layer(hp, jnp.zeros((bp, c), F32), jnp.zeros((bp, RW_HEADS, RW_HEAD, RW_HEAD), F32),
                                            vf_p, p, bp, tp, TM_ROWS, WKV_PAIRS_PER_STEP, WKV_CHUNK)
            xs, s_s, l_s, vf_s = rwkv_layer(hs, state_rwkv_shift[j], state_rwkv_wkv[j], vf_s, p, bs, ts,
                                            TM_ROWS, WKV_PAIRS_PER_STEP, ts)
            out["wkv_p"].append(s_p); out["sh_p"].append(l_p); out["wkv_s"].append(s_s); out["sh_s"].append(l_s)
        elif kind == 1:
            p = prep_mla(w, j, i)
            xp, c_p, k_p = mla_layer_prompt(hp, p, bp, tp, TM_PROJ, ATTN_TQ)
            xs, c_s, k_s = mla_layer_sample(hs, cache_mla_ckv[j], cache_mla_kpe[j], page_table, p, bs, ts,
                                            past_len, TM_PROJ, PAGES_PER_STEP)
            out["ckv_p"].append(c_p.reshape(bp, tp, -1)); out["kpe_p"].append(k_p.reshape(bp, tp, -1))
            out["ckv_s"].append(c_s.reshape(bs, ts, -1)); out["kpe_s"].append(k_s.reshape(bs, ts, -1))
        else:
            p = prep_mamba(w, j, i)
            xp, cv_p, h_p = mamba_layer(hp, jnp.zeros((bp, MB_CONV - 1, MB_CONV_DIM), F32),
                                        jnp.zeros((bp, MB_HEADS, MB_HEAD, MB_STATE), F32), p, bp, tp, MB_CHUNK)
            xs, cv_s, h_s = mamba_layer(hs, state_conv[j], state_ssm[j], p, bs, ts, math.gcd(ts, MB_CHUNK))
            out["ssm_p"].append(h_p); out["conv_p"].append(cv_p); out["ssm_s"].append(h_s); out["conv_s"].append(cv_s)
        w1, w2 = ffn_w1[i].astype(BF16), ffn_w2[i].astype(BF16)
        hp = mixer_ffn_res(hp, xp, p["w_o"], norm_ffn[i], w1, w2, TM_FFN, TH_FFN)
        hs = mixer_ffn_res(hs, xs, p["w_o"], norm_ffn[i], w1, w2, TM_FFN, TH_FFN)
    stack = lambda k: jnp.stack(out[k])
    return (hp.reshape(bp, tp, c), hs.reshape(bs, ts, c),
            stack("ckv_p"), stack("kpe_p"), stack("ckv_s"), stack("kpe_s"),
            stack("wkv_p"), stack("sh_p"), stack("wkv_s"), stack("sh_s"),
            stack("ssm_p"), stack("conv_p"), stack("ssm_s"), stack("conv_s"))
```

```python
import functools
import math

import jax
import jax.numpy as jnp
from jax import lax
from jax.experimental import pallas as pl
from jax.experimental.pallas import tpu as pltpu

F32 = jnp.float32
BF16 = jnp.bfloat16

D_MODEL = 1024
NORM_EPS = 1e-6

RW_HEAD = 64
RW_HEADS = D_MODEL // RW_HEAD
RW_LNX_EPS = 64e-5

MLA_HEADS = 16
MLA_Q_LORA = 512
MLA_KV_LORA = 256
MLA_NOPE = 64
MLA_ROPE = 32
MLA_V = 64
MLA_SCALE = 1.0 / math.sqrt(MLA_NOPE + MLA_ROPE)
ROPE_THETA = 10000.0
PAGE_SIZE = 128

MB_INNER = 2 * D_MODEL
MB_HEAD = 64
MB_HEADS = MB_INNER // MB_HEAD
MB_GROUPS = 4
MB_STATE = 128
MB_CONV = 4
MB_CONV_DIM = MB_INNER + 2 * MB_GROUPS * MB_STATE
MB_CHUNK = 128

VMEM_LIMIT_BYTES = 56 * 2**20


def _params(*sem):
    return pltpu.CompilerParams(dimension_semantics=sem, vmem_limit_bytes=VMEM_LIMIT_BYTES)


def _dot(a, b):
    return jnp.dot(a.astype(BF16), b.astype(BF16), preferred_element_type=F32)


def _dot_nt(a, b):
    return lax.dot_general(a.astype(BF16), b.astype(BF16), (((1,), (1,)), ((), ())),
                           preferred_element_type=F32)


def _dot_tn(a, b):
    return lax.dot_general(a.astype(BF16), b.astype(BF16), (((0,), (0,)), ((), ())),
                           preferred_element_type=F32)


def _split3(a):
    hi = a.astype(BF16)
    r1 = a - hi.astype(F32)
    mid = r1.astype(BF16)
    lo = (r1 - mid.astype(F32)).astype(BF16)
    return hi, mid, lo


def _rms(x, g):
    return x * lax.rsqrt(jnp.mean(x * x, axis=-1, keepdims=True) + NORM_EPS) * g


def _softplus(z):
    return jnp.maximum(z, 0.0) + jnp.log(1.0 + jnp.exp(-jnp.abs(z)))


def _sigmoid(z):
    return 0.5 * jnp.tanh(0.5 * z) + 0.5


def _silu(z):
    return z * _sigmoid(z)


def _rmsnorm_kernel(x_ref, g_ref, o_ref):
    o_ref[...] = _rms(x_ref[...], g_ref[...])


def rmsnorm_rows(x, g, tm):
    m, c = x.shape
    return pl.pallas_call(
        _rmsnorm_kernel,
        grid=(m // tm,),
        in_specs=[pl.BlockSpec((tm, c), lambda i: (i, 0)), pl.BlockSpec((1, c), lambda i: (0, 0))],
        out_specs=pl.BlockSpec((tm, c), lambda i: (i, 0)),
        out_shape=jax.ShapeDtypeStruct((m, c), F32),
        compiler_params=_params("parallel"),
        name="rmsnorm_rows",
    )(x, g.reshape(1, c))


def _norm_linear_kernel(x_ref, g_ref, w_ref, o_ref, xn_ref):
    @pl.when(pl.program_id(1) == 0)
    def _():
        xn_ref[...] = _rms(x_ref[...], g_ref[...]).astype(BF16)

    o_ref[...] = jnp.dot(xn_ref[...], w_ref[...], preferred_element_type=F32)


def norm_linear(x, g, w, tm, tn):
    m, k = x.shape
    n = w.shape[1]
    return pl.pallas_call(
        _norm_linear_kernel,
        grid=(m // tm, n // tn),
        in_specs=[pl.BlockSpec((tm, k), lambda i, j: (i, 0)),
                  pl.BlockSpec((1, k), lambda i, j: (0, 0)),
                  pl.BlockSpec((k, tn), lambda i, j: (0, j))],
        out_specs=pl.BlockSpec((tm, tn), lambda i, j: (i, j)),
        out_shape=jax.ShapeDtypeStruct((m, n), F32),
        scratch_shapes=[pltpu.VMEM((tm, k), BF16)],
        compiler_params=_params("parallel", "arbitrary"),
        name="norm_linear",
    )(x, g.reshape(1, k), w)


def _mixer_ffn_kernel(h_ref, x_ref, wo_ref, g_ref, w1_ref, w2_ref, o_ref, xn_ref):
    @pl.when(pl.program_id(1) == 0)
    def _():
        mixed = h_ref[...] + _dot(x_ref[...], wo_ref[...])
        o_ref[...] = mixed
        xn_ref[...] = _rms(mixed, g_ref[...]).astype(BF16)

    u = jnp.dot(xn_ref[...], w1_ref[...], preferred_element_type=F32)
    u = jnp.square(jnp.maximum(u, 0.0))
    o_ref[...] += jnp.dot(u.astype(BF16), w2_ref[...], preferred_element_type=F32)


def mixer_ffn_res(h, x, w_o, g, w1, w2, tm, th):
    m, c = h.shape
    k = x.shape[1]
    hid = w1.shape[1]
    return pl.pallas_call(
        _mixer_ffn_kernel,
        grid=(m // tm, hid // th),
        in_specs=[pl.BlockSpec((tm, c), lambda i, j: (i, 0)),
                  pl.BlockSpec((tm, k), lambda i, j: (i, 0)),
                  pl.BlockSpec((k, c), lambda i, j: (0, 0), pipeline_mode=pl.Buffered(1)),
                  pl.BlockSpec((1, c), lambda i, j: (0, 0)),
                  pl.BlockSpec((c, th), lambda i, j: (0, j)),
                  pl.BlockSpec((th, c), lambda i, j: (j, 0))],
        out_specs=pl.BlockSpec((tm, c), lambda i, j: (i, 0)),
        out_shape=jax.ShapeDtypeStruct((m, c), F32),
        scratch_shapes=[pltpu.VMEM((tm, c), BF16)],
        compiler_params=_params("parallel", "arbitrary"),
        name="mixer_ffn_res",
    )(h, x, w_o, g.reshape(1, c), w1, w2)


SHIFT_HALO = 8


def _rwkv_proj_kernel(has_vres, seq, tm, *refs):
    if has_vres:
        (h_ref, halo_ref, sh_ref, gmix_ref, mu_ref, wrkv_ref, w0_ref, w1_ref, w2_ref, a0_ref, a1_ref, a2_ref,
         g1_ref, g2_ref, v0_ref, v1_ref, v2_ref,
         r_ref, lw_ref, k_ref, v_ref, a_ref, g_ref, vg_ref) = refs
    else:
        (h_ref, halo_ref, sh_ref, gmix_ref, mu_ref, wrkv_ref, w0_ref, w1_ref, w2_ref, a0_ref, a1_ref, a2_ref,
         g1_ref, g2_ref,
         r_ref, lw_ref, k_ref, v_ref, a_ref, g_ref) = refs
    x = _rms(h_ref[...], gmix_ref[...])
    rolled = pltpu.roll(x, 1, axis=0)
    row = lax.broadcasted_iota(jnp.int32, x.shape, 0)
    if seq >= tm:
        is_start = (pl.program_id(0) % (seq // tm)) == 0
        tail = _rms(halo_ref[...], gmix_ref[...])[SHIFT_HALO - 1:SHIFT_HALO, :]
        xprev = jnp.where(row == 0, jnp.where(is_start, sh_ref[0], tail), rolled)
    else:
        n_seq = tm // seq
        starts = jnp.broadcast_to(sh_ref[...][:, None, :], (n_seq, seq, x.shape[1])).reshape(x.shape)
        xprev = jnp.where(row % seq == 0, starts, rolled)
    dx = xprev - x
    xm = [(x + dx * mu_ref[p:p + 1, :]).astype(BF16) for p in range(6)]
    r_ref[...] = jnp.dot(xm[0], wrkv_ref[0], preferred_element_type=F32)
    k_ref[...] = jnp.dot(xm[1], wrkv_ref[1], preferred_element_type=F32)
    v_ref[...] = jnp.dot(xm[2], wrkv_ref[2], preferred_element_type=F32)
    wpre = w0_ref[...] + _dot(jnp.tanh(jnp.dot(xm[3], w1_ref[...], preferred_element_type=F32)), w2_ref[...])
    w_log = -_softplus(-wpre) - 0.5
    lw_ref[...] = -jnp.exp(w_log)
    a_ref[...] = _sigmoid(a0_ref[...] + _dot(jnp.dot(xm[4], a1_ref[...], preferred_element_type=F32), a2_ref[...]))
    g_ref[...] = _dot(_sigmoid(jnp.dot(xm[5], g1_ref[...], preferred_element_type=F32)), g2_ref[...])
    if has_vres:
        vg_ref[...] = _sigmoid(v0_ref[...] + _dot(jnp.dot(xm[2], v1_ref[...], preferred_element_type=F32),
                                                  v2_ref[...]))


def rwkv_proj(h, shift, p, seq, tm):
    m, c = h.shape
    assert tm % SHIFT_HALO == 0 and (seq % tm == 0 or tm % seq == 0)
    has_vres = "v1" in p
    row = pl.BlockSpec((tm, c), lambda i: (i, 0))
    halo = pl.BlockSpec((SHIFT_HALO, c), lambda i: (jnp.maximum(i * (tm // SHIFT_HALO) - 1, 0), 0))
    if seq >= tm:
        shift_arg = shift.reshape(-1, 1, c)
        shift_spec = pl.BlockSpec((1, 1, c), lambda i: (i // (seq // tm), 0, 0))
    else:
        shift_arg = shift
        shift_spec = pl.BlockSpec((tm // seq, c), lambda i: (i, 0))
    full = lambda arr: pl.BlockSpec(arr.shape, lambda i: (0,) * arr.ndim, pipeline_mode=pl.Buffered(1))
    names = ["norm_mix", "mu", "w_rkv", "w0", "w1", "w2", "a0", "a1", "a2", "g1", "g2"]
    if has_vres:
        names += ["v0", "v1", "v2"]
    weights = [p[n] for n in names]
    n_out = 7 if has_vres else 6
    return pl.pallas_call(
        functools.partial(_rwkv_proj_kernel, has_vres, seq, tm),
        grid=(m // tm,),
        in_specs=[row, halo, shift_spec] + [full(w) for w in weights],
        out_specs=[row] * n_out,
        out_shape=[jax.ShapeDtypeStruct((m, c), F32)] * n_out,
        compiler_params=_params("parallel"),
        name="rwkv_proj",
    )(h, h, shift_arg, *weights)


PAIR = 2 * RW_HEAD
WKV_SHORT_SEQS_PER_STEP = 4
WKV_CHUNKS_PER_STEP = 4


def _dot_rhs_hilo(a, b):
    bh = b.astype(BF16)
    bl = (b - bh.astype(F32)).astype(BF16)
    ab = a.astype(BF16)
    return jnp.dot(ab, bh, preferred_element_type=F32) + jnp.dot(ab, bl, preferred_element_type=F32)


def _dot_tn_hilo(a, b):
    tn = (((0,), (0,)), ((), ()))
    ah = a.astype(BF16)
    al = (a - ah.astype(F32)).astype(BF16)
    bh = b.astype(BF16)
    bl = (b - bh.astype(F32)).astype(BF16)
    f = lambda x, y: lax.dot_general(x, y, tn, preferred_element_type=F32)
    return f(ah, bh) + f(ah, bl) + f(al, bh)


def _wkv_kernel(has_vres, nseq, nch, npair, ln, *refs):
    if has_vres:
        (r_ref, lw_ref, k_ref, v_ref, a_ref, g_ref, vf_ref, vg_ref, s0_ref,
         kk_ref, ka_ref, rk_ref, lnw_ref, lnb_ref, y_ref, sout_ref, s_ref) = refs
    else:
        (r_ref, lw_ref, k_ref, v_ref, a_ref, g_ref, s0_ref,
         kk_ref, ka_ref, rk_ref, lnw_ref, lnb_ref, y_ref, sout_ref, s_ref) = refs
    c = pl.program_id(2)
    n = RW_HEAD
    rows = 2 * ln

    @pl.when(c == 0)
    def _():
        zero = jnp.zeros((n, n), F32)
        for q in range(nseq * npair):
            bi, p = divmod(q, npair)
            s_ref[q] = jnp.concatenate([jnp.concatenate([s0_ref[bi, 2 * p], zero], axis=1),
                                        jnp.concatenate([zero, s0_ref[bi, 2 * p + 1]], axis=1)], axis=0)

    first = lax.broadcasted_iota(jnp.int32, (ln, PAIR), 1) < n
    ri = lax.broadcasted_iota(jnp.int32, (rows, rows), 0)
    ci = lax.broadcasted_iota(jnp.int32, (rows, rows), 1)
    same = (ri // ln) == (ci // ln)
    incl = same & (ci <= ri)
    strict = same & (ci < ri)
    eye = jnp.where(ri == ci, 1.0, 0.0)
    ti = lax.broadcasted_iota(jnp.int32, (ln, ln), 0)
    tj = lax.broadcasted_iota(jnp.int32, (ln, ln), 1)
    tri = jnp.where(tj <= ti, 1.0, 0.0).astype(BF16)

    def seg_sum(x):
        s1 = jnp.sum(jnp.where(first, x, 0.0), axis=-1, keepdims=True)
        s2 = jnp.sum(jnp.where(first, 0.0, x), axis=-1, keepdims=True)
        return jnp.where(first, s1, s2)

    def by_head(x):
        return jnp.concatenate([jnp.where(first, x, 0.0), jnp.where(first, 0.0, x)], axis=0)

    pairs = range(nseq * npair)
    lanes = [slice((q % npair) * PAIR, (q % npair + 1) * PAIR) for q in pairs]

    def tok_rows(ch, p):
        first_row = (ch * nseq + p // npair) * ln
        return slice(first_row, first_row + ln)

    def prepare(ch, p):
        here = (tok_rows(ch, p), lanes[p])
        r = r_ref[here]
        lw = lw_ref[here]
        k = k_ref[here]
        v = v_ref[here]
        a = a_ref[here]
        if has_vres:
            v = v + (vf_ref[here] - v) * vg_ref[here]
        kk = k * kk_ref[:, lanes[p]]
        kk = kk / jnp.maximum(jnp.sqrt(seg_sum(kk * kk)), 1e-12)
        k2 = k * (1.0 + (a - 1.0) * ka_ref[:, lanes[p]])
        bv = kk * a
        hi, mid, lo = _split3(lw)
        cl3 = jnp.dot(tri, jnp.concatenate([hi, mid, lo], axis=1), preferred_element_type=F32)
        cl = cl3[:, :PAIR] + cl3[:, PAIR:2 * PAIR] + cl3[:, 2 * PAIR:]
        cl_end = cl[ln - 1:ln, :]
        g_inv = jnp.exp(-cl)
        g_end = jnp.exp(cl_end - cl)
        rt = r * jnp.exp(cl)
        at = -kk * jnp.exp(cl - lw)
        lhs = jnp.concatenate([by_head(at), by_head(rt)], axis=0)
        bh = bv * g_inv
        kh = k2 * g_inv
        return dict(r=r, v=v, k2=k2, lhs=lhs.astype(BF16),
                    rhs=jnp.concatenate([bh, bh, kh, kh], axis=0).astype(BF16),
                    decay=jnp.exp(cl_end), v_h=by_head(v),
                    ws=jnp.concatenate([by_head(bv * g_end), by_head(k2 * g_end)], axis=0))

    def same_block(size):
        return (ri // size) == (ci // size)

    below = ci < ri
    levels = list(range(1, int(math.log2(ln))))

    def couple(st):
        aa = [_dot_nt(st[p]["lhs"], st[p]["rhs"]) for p in pairs]
        a_ak = [jnp.where(strict, aa[p][:rows, rows:], 0.0) for p in pairs]
        a_r = [jnp.concatenate([jnp.where(incl, aa[p][rows:, :rows], 0.0),
                                jnp.where(incl, aa[p][rows:, rows:], 0.0)], axis=1).astype(BF16) for p in pairs]
        akv = [_dot_rhs_hilo(a_ak[p], st[p]["v_h"]) for p in pairs]
        tinv = [eye + jnp.where(below & same_block(2), aa[p][:rows, :rows], 0.0) for p in pairs]
        return dict(n=[aa[p][:rows, :rows] for p in pairs], a_r=a_r, akv=akv, tinv=tinv)

    def merge(cp, level):
        size = 2 ** level
        coupling = below & same_block(2 * size) & jnp.logical_not(same_block(size))
        n_off = [jnp.where(coupling, cp["n"][p], 0.0).astype(BF16) for p in pairs]
        tinv_bf = [cp["tinv"][p].astype(BF16) for p in pairs]
        right = [_dot(n_off[p], tinv_bf[p]) for p in pairs]
        cp["tinv"] = [cp["tinv"][p] + _dot(tinv_bf[p], right[p]) for p in pairs]

    def advance(st, cp):
        s_prev = [s_ref[p] for p in pairs]
        xy = [_dot_nt(st[p]["lhs"], s_prev[p]) for p in pairs]
        u = [_dot_rhs_hilo(cp["tinv"][p], xy[p][:rows] + cp["akv"][p]) for p in pairs]
        zs = [jnp.concatenate([u[p], st[p]["v_h"]], axis=0) for p in pairs]
        for p in pairs:
            s_ref[p] = s_prev[p] * st[p]["decay"] + _dot_tn_hilo(zs[p], st[p]["ws"])
        return [xy[p][rows:] + _dot(cp["a_r"][p], zs[p]) for p in pairs]

    def finish(ch, p, st, y_h):
        y = y_h[p][:ln] + y_h[p][ln:]
        mean = seg_sum(y) * (1.0 / n)
        yc = y - mean
        var = seg_sum(yc * yc) * (1.0 / n)
        y = yc * lax.rsqrt(var + RW_LNX_EPS) * lnw_ref[:, lanes[p]] + lnb_ref[:, lanes[p]]
        y = y + seg_sum(st[p]["r"] * st[p]["k2"] * rk_ref[:, lanes[p]]) * st[p]["v"]
        y_ref[tok_rows(ch, p), lanes[p]] = y * g_ref[tok_rows(ch, p), lanes[p]]

    def shares(items, parts):
        per = -(-len(items) // parts)
        return [items[i * per:(i + 1) * per] for i in range(parts)]

    st = [prepare(0, p) for p in pairs]
    cp = couple(st)
    done = None
    for ch in range(nch):
        st_next = []
        for level, group in zip(levels, shares(list(pairs), len(levels))):
            merge(cp, level)
            if ch + 1 < nch:
                st_next += [prepare(ch + 1, p) for p in group]
            if done is not None:
                for p in group:
                    finish(done[0], p, done[1], done[2])
        cp_next = couple(st_next) if ch + 1 < nch else None
        done = (ch, st, advance(st, cp))
        st, cp = st_next, cp_next
    for p in pairs:
        finish(done[0], p, done[1], done[2])

    @pl.when(c == pl.num_programs(2) - 1)
    def _():
        for q in range(nseq * npair):
            bi, p = divmod(q, npair)
            sout_ref[bi, 2 * p] = s_ref[q, :n, :n]
            sout_ref[bi, 2 * p + 1] = s_ref[q, n:, n:]


def wkv_scan(r, lw, k, v, a, g, s0, p, batch, seq, npair, ln, vres=None):
    nc = seq // ln
    nseq = WKV_SHORT_SEQS_PER_STEP if (nc == 1 and batch % WKV_SHORT_SEQS_PER_STEP == 0) else 1
    has_vres = vres is not None
    width = npair * PAIR
    nch = math.gcd(nc, WKV_CHUNKS_PER_STEP) if nseq == 1 else 1
    steps = nc // nch
    tok = pl.BlockSpec((nseq * nch * ln, width), lambda b, q, c: (b * steps + c, q))
    par = pl.BlockSpec((1, width), lambda b, q, c: (0, q))
    st = pl.BlockSpec((nseq, 2 * npair, RW_HEAD, RW_HEAD), lambda b, q, c: (b, q, 0, 0))
    seqs = [r, lw, k, v, a, g] + (list(vres) if has_vres else [])
    pars = [p[nm].reshape(1, D_MODEL) for nm in ("k_k", "k_a", "r_k", "lnx_w", "lnx_b")]
    return pl.pallas_call(
        functools.partial(_wkv_kernel, has_vres, nseq, nch, npair, ln),
        grid=(batch // nseq, D_MODEL // width, steps),
        in_specs=[tok] * len(seqs) + [st] + [par] * 5,
        out_specs=[tok, st],
        out_shape=[jax.ShapeDtypeStruct(r.shape, F32), jax.ShapeDtypeStruct(s0.shape, F32)],
        scratch_shapes=[pltpu.VMEM((nseq * npair, PAIR, PAIR), F32)],
        compiler_params=_params("parallel", "parallel", "arbitrary"),
        name="wkv_scan",
    )(*seqs, s0, *pars)


def prep_rwkv(w, j, i):
    row = lambda z: z.reshape(1, -1).astype(F32)
    p = {
        "norm_mix": row(w["norm_mix"][i]),
        "mu": w["rwkv_mu"][j],
        "w_rkv": w["rwkv_w_rkv"][j].astype(BF16),
        "w0": row(w["rwkv_w0"][j]), "w1": w["rwkv_w1"][j].astype(BF16), "w2": w["rwkv_w2"][j].astype(BF16),
        "a0": row(w["rwkv_a0"][j]), "a1": w["rwkv_a1"][j].astype(BF16), "a2": w["rwkv_a2"][j].astype(BF16),
        "g1": w["rwkv_g1"][j].astype(BF16), "g2": w["rwkv_g2"][j].astype(BF16),
        "k_k": w["rwkv_k_k"][j], "k_a": w["rwkv_k_a"][j], "r_k": w["rwkv_r_k"][j],
        "lnx_w": w["rwkv_lnx_w"][j], "lnx_b": w["rwkv_lnx_b"][j],
        "w_o": w["rwkv_w_o"][j].astype(BF16),
    }
    if j > 0:
        p["v0"] = row(w["rwkv_v0"][j - 1])
        p["v1"] = w["rwkv_v1"][j - 1].astype(BF16)
        p["v2"] = w["rwkv_v2"][j - 1].astype(BF16)
    return p


def rwkv_layer(h, shift, s0, vfirst, p, batch, seq, tm, npair, ln):
    m, c = h.shape
    outs = rwkv_proj(h, shift, p, seq, tm)
    r, lw, k, v, a, g = outs[:6]
    vres = None if vfirst is None else (vfirst, outs[6])
    y, s_new = wkv_scan(r, lw, k, v, a, g, s0, p, batch, seq, npair, ln, vres)
    shift_new = rmsnorm_rows(h.reshape(batch, seq, c)[:, -1], p["norm_mix"], batch)
    return y, s_new, shift_new, (v if vfirst is None else vfirst)


SEG_PAD = 128


def _dot_hilo_rhs(x, ones):
    hi = x.astype(BF16)
    lo = (x - hi.astype(F32)).astype(BF16)
    return jnp.dot(hi, ones, preferred_element_type=F32) + jnp.dot(lo, ones, preferred_element_type=F32)


def _seg_rms_scale(x, seg_ref, segt_ref, invw_ref):
    ssq = _dot(x * x, seg_ref[...])
    return _dot_hilo_rhs(lax.rsqrt(ssq * invw_ref[...] + NORM_EPS), segt_ref[...])


def _swap_halves(x, group):
    half = group // 2
    width = x.shape[-1]
    lane = lax.broadcasted_iota(jnp.int32, x.shape, x.ndim - 1)
    return jnp.where((lane % group) < half,
                     pltpu.roll(x, width - half, axis=x.ndim - 1),
                     pltpu.roll(x, half, axis=x.ndim - 1))


SLOT = 128


def _mla_proj_kernel(with_kv, *refs):
    (h_ref, cos_ref, sin_ref, gmix_ref, win_ref, gq_ref, gkv_ref, gkr_ref, wuq_ref, gqc_ref,
     seg_ref, segt_ref, invw_ref) = refs[:13]
    if with_kv:
        wuk_ref, wuv_ref, gkn_ref, qc_ref, c_ref, kp_ref, kc_ref, v_ref = refs[13:]
    else:
        qc_ref, c_ref, kp_ref = refs[13:]
    x = _rms(h_ref[...], gmix_ref[...])
    hp = _dot(x, win_ref[...])
    q_a = _rms(hp[:, :MLA_Q_LORA], gq_ref[...])
    c = _rms(hp[:, MLA_Q_LORA:MLA_Q_LORA + MLA_KV_LORA], gkv_ref[...])
    c_ref[...] = c

    cos = cos_ref[...]
    sin = sin_ref[...]

    def rope(z, cs, sn):
        return z * cs + _swap_halves(z, MLA_ROPE) * sn

    kp_raw = hp[:, MLA_Q_LORA + MLA_KV_LORA:]
    kp_scale = lax.rsqrt(jnp.sum(kp_raw * kp_raw, axis=-1, keepdims=True) * (1.0 / MLA_ROPE) + NORM_EPS)
    kp = rope(pltpu.roll(kp_raw, MLA_NOPE, axis=1) * gkr_ref[...], cos, sin) * kp_scale
    kp_ref[...] = kp[:, MLA_NOPE:MLA_NOPE + MLA_ROPE]

    q = _dot(q_a, wuq_ref[...])
    cos_h = jnp.tile(cos, (1, MLA_HEADS))
    sin_h = jnp.tile(sin, (1, MLA_HEADS))
    qc = rope(q * gqc_ref[...], cos_h, sin_h) * _seg_rms_scale(q, seg_ref, segt_ref, invw_ref)
    qc_ref[...] = qc.astype(BF16)
    if with_kv:
        kraw = _dot(c, wuk_ref[...])
        kn = kraw * _seg_rms_scale(kraw, seg_ref, segt_ref, invw_ref) * gkn_ref[...]
        kc_ref[...] = (kn + jnp.tile(kp, (1, MLA_HEADS))).astype(BF16)
        v_ref[0] = _dot_nt(wuv_ref[...], c).astype(BF16)


def _slot_seg_matrix():
    lane = jnp.arange(MLA_HEADS * SLOT)
    head, off = lane // SLOT, lane % SLOT
    col = jnp.where(off < MLA_NOPE, head, jnp.where(off < MLA_NOPE + MLA_ROPE, MLA_HEADS + head, SEG_PAD))
    return (col[:, None] == jnp.arange(SEG_PAD)[None, :]).astype(BF16)


def _slot(nope, rope):
    pad = jnp.zeros(nope.shape[:-1] + (SLOT - MLA_NOPE - MLA_ROPE,), nope.dtype)
    return jnp.concatenate([nope, rope, pad], axis=-1)


def _rope_tables(pos):
    half = MLA_ROPE // 2
    inv = ROPE_THETA ** (-jnp.arange(half, dtype=F32) / half)
    ang = pos.astype(F32)[:, None] * inv[None, :]
    cos = jnp.cos(ang)
    sin = jnp.sin(ang)
    ones = jnp.ones((pos.shape[0], MLA_NOPE), F32)
    return (_slot(ones, jnp.concatenate([cos, cos], axis=1)),
            _slot(jnp.zeros_like(ones), jnp.concatenate([-sin, sin], axis=1)))


def mla_proj(h, pos, p, tm, with_kv):
    m, c = h.shape
    seq = pos.shape[0]
    cos, sin = _rope_tables(pos)
    if seq >= tm:
        nrep = seq // tm
    else:
        cos, sin = jnp.tile(cos, (tm // seq, 1)), jnp.tile(sin, (tm // seq, 1))
        nrep = 1
    row = lambda width: pl.BlockSpec((tm, width), lambda i: (i, 0))
    tab = pl.BlockSpec((tm, SLOT), lambda i: (i % nrep, 0))
    full = lambda arr: pl.BlockSpec(arr.shape, lambda i: (0,) * arr.ndim)
    seg = _slot_seg_matrix()
    invw = jnp.concatenate([jnp.full((MLA_HEADS,), 1.0 / MLA_NOPE, F32), jnp.full((MLA_HEADS,), 1.0 / MLA_ROPE, F32),
                            jnp.ones((SEG_PAD - 2 * MLA_HEADS,), F32)]).reshape(1, SEG_PAD)
    weights = [p["norm_mix"], p["w_in"], p["q_norm"], p["kv_norm"], p["kr_norm"], p["w_uq"],
               p["q_gain"] if with_kv else p["q_gain_abs"], seg, seg.T, invw]
    wide = MLA_HEADS * SLOT
    out_shape = [jax.ShapeDtypeStruct((m, wide), BF16),
                 jax.ShapeDtypeStruct((m, MLA_KV_LORA), F32), jax.ShapeDtypeStruct((m, MLA_ROPE), F32)]
    out_specs = [row(wide), row(MLA_KV_LORA), row(MLA_ROPE)]
    if with_kv:
        weights += [p["w_uk"], p["w_uv_t"], p["kn_gain"]]
        out_shape += [jax.ShapeDtypeStruct((m, wide), BF16),
                      jax.ShapeDtypeStruct((m // tm, MLA_HEADS * MLA_V, tm), BF16)]
        out_specs += [row(wide), pl.BlockSpec((1, MLA_HEADS * MLA_V, tm), lambda i: (i, 0, 0))]
    return pl.pallas_call(
        functools.partial(_mla_proj_kernel, with_kv),
        grid=(m // tm,),
        in_specs=[row(c), tab, tab] + [full(w) for w in weights],
        out_specs=out_specs,
        out_shape=out_shape,
        compiler_params=_params("parallel"),
        name="mla_proj",
    )(h, cos, sin, *weights)


def prep_mla(w, j, i):
    row = lambda z: z.reshape(1, -1).astype(F32)
    w_in = w["mla_w_in"][j]
    pad = jnp.zeros((D_MODEL, 128 - MLA_ROPE), F32)
    w_uq = w["mla_w_uq"][j]
    w_uk = w["mla_w_uk"][j]
    zero_n = jnp.zeros((MLA_NOPE,), F32)
    zero_r = jnp.zeros((MLA_ROPE,), F32)
    qn, qr, kn = w["mla_qn_norm"][j], w["mla_qr_norm"][j], w["mla_kn_norm"][j]
    return {
        "norm_mix": row(w["norm_mix"][i]),
        "w_in": jnp.concatenate([w_in, pad], axis=1).astype(BF16),
        "q_norm": row(w["mla_q_norm"][j]), "kv_norm": row(w["mla_kv_norm"][j]),
        "kr_norm": row(_slot(zero_n, w["mla_kr_norm"][j])),
        "w_uq": _slot(w_uq[:, :, :MLA_NOPE], w_uq[:, :, MLA_NOPE:]).reshape(MLA_Q_LORA, -1).astype(BF16),
        "q_gain": row(jnp.tile(_slot(qn, qr) * MLA_SCALE, MLA_HEADS)),
        "q_gain_abs": row(jnp.tile(_slot(qn * kn, qr) * MLA_SCALE, MLA_HEADS)),
        "kn_gain": row(jnp.tile(_slot(kn, zero_r), MLA_HEADS)),
        "w_uk": _slot(w_uk, jnp.zeros(w_uk.shape[:2] + (MLA_ROPE,), F32)).reshape(MLA_KV_LORA, -1).astype(BF16),
        "w_uv": w["mla_w_uv"][j].reshape(MLA_KV_LORA, -1).astype(BF16),
        "w_uv_t": w["mla_w_uv"][j].reshape(MLA_KV_LORA, -1).T.astype(BF16),
        "w_uk_t": w["mla_w_uk"][j].reshape(MLA_KV_LORA, -1).T.astype(BF16),
        "w_uk_heads": jnp.transpose(w["mla_w_uk"][j], (1, 2, 0)).astype(BF16),
        "w_o": w["mla_w_o"][j].astype(BF16),
    }


ATTN_HEADS_PER_STEP = 16
NEG_BIG = -1e30


def _flash_kernel(tq, qc_ref, kc_ref, vt_ref, o_ref, m_ref, l_ref, acc_ref):
    qi = pl.program_id(2)
    g = ATTN_HEADS_PER_STEP
    heads = range(g)
    key_i = lax.broadcasted_iota(jnp.int32, (tq, tq), 0)
    qry_i = lax.broadcasted_iota(jnp.int32, (tq, tq), 1)
    m_ref[...] = jnp.full_like(m_ref, NEG_BIG)
    l_ref[...] = jnp.zeros_like(l_ref)
    acc_ref[...] = jnp.zeros_like(acc_ref)

    def block(j, masked):
        keys = pl.ds(pl.multiple_of(j * tq, tq), tq)
        s = [_dot_nt(kc_ref[0, keys, hh * SLOT:(hh + 1) * SLOT], qc_ref[0, :, hh * SLOT:(hh + 1) * SLOT])
             for hh in heads]
        if masked:
            s = [jnp.where(key_i <= qry_i, s[hh], NEG_BIG) for hh in heads]
        m_prev = [m_ref[hh] for hh in heads]
        m_new = [jnp.maximum(m_prev[hh], jnp.max(s[hh], axis=0, keepdims=True)) for hh in heads]
        alpha = [jnp.exp(m_prev[hh] - m_new[hh]) for hh in heads]
        pr = [jnp.exp(s[hh] - m_new[hh]) for hh in heads]
        pv = [jnp.dot(vt_ref[j, hh * MLA_V:(hh + 1) * MLA_V, :], pr[hh].astype(BF16), preferred_element_type=F32)
              for hh in heads]
        for hh in heads:
            l_ref[hh] = alpha[hh] * l_ref[hh] + jnp.sum(pr[hh], axis=0, keepdims=True)
            acc_ref[hh] = alpha[hh] * acc_ref[hh] + pv[hh]
            m_ref[hh] = m_new[hh]

    def body(j, carry):
        block(j, False)
        return carry

    lax.fori_loop(0, qi, body, 0)
    block(qi, True)
    o_t = jnp.concatenate([acc_ref[hh] / l_ref[hh] for hh in heads], axis=0)
    o_ref[0] = o_t.T


def flash_prompt(qc, kc, vt, batch, seq, tq):
    g = ATTN_HEADS_PER_STEP
    nq = seq // tq
    r3 = lambda z: z.reshape(batch, seq, -1)
    return pl.pallas_call(
        functools.partial(_flash_kernel, tq),
        grid=(batch, MLA_HEADS // g, nq),
        in_specs=[pl.BlockSpec((1, tq, g * SLOT), lambda b, hq, i: (b, i, hq)),
                  pl.BlockSpec((1, seq, g * SLOT), lambda b, hq, i: (b, 0, hq)),
                  pl.BlockSpec((nq, g * MLA_V, tq), lambda b, hq, i: (b, hq, 0))],
        out_specs=pl.BlockSpec((1, tq, g * MLA_V), lambda b, hq, i: (b, i, hq)),
        out_shape=jax.ShapeDtypeStruct((batch, seq, MLA_HEADS * MLA_V), F32),
        scratch_shapes=[pltpu.VMEM((g, 1, tq), F32), pltpu.VMEM((g, 1, tq), F32),
                        pltpu.VMEM((g, MLA_V, tq), F32)],
        compiler_params=_params("parallel", "parallel", "parallel"),
        name="flash_prompt",
    )(r3(qc), r3(kc), vt).reshape(batch * seq, -1)


def _bmm_kernel(a_ref, b_ref, o_ref):
    o_ref[0] = _dot(a_ref[0], b_ref[0]).astype(o_ref.dtype)


def bmm(a, b, out_dtype):
    g, m, k = a.shape
    n = b.shape[2]
    return pl.pallas_call(
        _bmm_kernel,
        grid=(g,),
        in_specs=[pl.BlockSpec((1, m, k), lambda i: (i, 0, 0)), pl.BlockSpec((1, k, n), lambda i: (i, 0, 0))],
        out_specs=pl.BlockSpec((1, m, n), lambda i: (i, 0, 0)),
        out_shape=jax.ShapeDtypeStruct((g, m, n), out_dtype),
        compiler_params=_params("parallel"),
        name="bmm",
    )(a, b)


def _paged_attn_kernel(pp, seq, n_pages, pt_ref, qa_ref, qp_ref, cn_ref, kpn_ref, wukt_ref, wuv_ref,
                       pool_c_ref, pool_kp_ref, o_ref, m_ref, l_ref, acc_ref, lhs_ref, c_buf, kp_buf, sems):
    b = pl.program_id(0)
    n_groups = n_pages // pp
    nq = seq * MLA_HEADS
    nup = MLA_HEADS * MLA_NOPE

    def group_copies(bb, g, slot):
        out = []
        for i in range(pp):
            page = pt_ref[bb * n_pages + g * pp + i]
            out.append(pltpu.make_async_copy(pool_c_ref.at[page], c_buf.at[slot, i], sems.at[0, slot]))
            out.append(pltpu.make_async_copy(pool_kp_ref.at[page], kp_buf.at[slot, i], sems.at[1, slot]))
        return out

    def start_group(bb, g, slot):
        for i, cp in enumerate(group_copies(bb, g, slot)):
            cp.start(priority=i % 2)

    @pl.when(b == 0)
    def _():
        start_group(b, 0, 0)

    m_ref[...] = jnp.full_like(m_ref, NEG_BIG)
    l_ref[...] = jnp.zeros_like(l_ref)
    acc_ref[...] = jnp.zeros_like(acc_ref)
    lhs_ref[:nup, :] = wukt_ref[...]
    lhs_ref[nup:, :] = qa_ref[0]

    qp = qp_ref[0]

    def attend(c_blks, kp_t_blks, mask):
        subs = range(len(c_blks))
        nk = c_blks[0].shape[0]
        c_bf = [c_blks[i].astype(BF16) for i in subs]
        both = [_dot_nt(lhs_ref[...], c_bf[i]) for i in subs]
        ssq = [jnp.sum(jnp.square(both[i][:nup]).reshape(MLA_HEADS, MLA_NOPE, nk), axis=1) for i in subs]
        rs = [lax.rsqrt(ssq[i] * (1.0 / MLA_NOPE) + NORM_EPS) for i in subs]
        s = [both[i][nup:] * jnp.concatenate([rs[i]] * seq, axis=0) + _dot(qp, kp_t_blks[i]) for i in subs]
        if mask is not None:
            s = [jnp.where(mask, s[i], NEG_BIG) for i in subs]
        m_prev = m_ref[...]
        m_new = m_prev
        for i in subs:
            m_new = jnp.maximum(m_new, jnp.max(s[i], axis=-1, keepdims=True))
        alpha = jnp.exp(m_prev - m_new)
        pr = [jnp.exp(s[i] - m_new) for i in subs]
        l_new = alpha * l_ref[...]
        acc = alpha * acc_ref[...]
        for i in subs:
            l_new = l_new + jnp.sum(pr[i], axis=-1, keepdims=True)
            acc = acc + _dot(pr[i], c_bf[i])
        l_ref[...] = l_new
        acc_ref[...] = acc
        m_ref[...] = m_new

    pages_per_sub = 2

    def group_step(g, carry):
        slot = (b * n_groups + g) % 2
        last = g == n_groups - 1
        next_b = jnp.where(last, b + 1, b)
        next_g = jnp.where(last, 0, g + 1)

        @pl.when(next_b < pl.num_programs(0))
        def _():
            start_group(next_b, next_g, 1 - slot)

        for cp in group_copies(b, g, slot):
            cp.wait()
        subs = range(0, pp, pages_per_sub)
        attend([c_buf[slot, pl.ds(i, pages_per_sub)].reshape(pages_per_sub * PAGE_SIZE, MLA_KV_LORA) for i in subs],
               [jnp.concatenate([kp_buf[slot, i + k] for k in range(pages_per_sub)], axis=1) for i in subs], None)
        return carry

    lax.fori_loop(0, n_groups, group_step, 0)

    qtok = lax.broadcasted_iota(jnp.int32, (nq, seq), 0) // MLA_HEADS
    ktok = lax.broadcasted_iota(jnp.int32, (nq, seq), 1)
    attend([cn_ref[0]], [kpn_ref[0]], ktok <= qtok)
    o_lat = acc_ref[...] / l_ref[...]
    full = _dot(o_lat, wuv_ref[...])
    rhead = lax.broadcasted_iota(jnp.int32, full.shape, 0) % MLA_HEADS
    lhead = lax.broadcasted_iota(jnp.int32, full.shape, 1) // MLA_V
    full = jnp.where(rhead == lhead, full, 0.0)
    o_ref[0] = jnp.sum(full.reshape(seq, MLA_HEADS, MLA_HEADS * MLA_V), axis=1)


def paged_attn(q_abs, qp, c_new, kp_new, pool_c, pool_kp_t, page_table, p, batch, seq, pp):
    n_pages = page_table.shape[1]
    assert n_pages % pp == 0 and pp % 2 == 0
    nq = seq * MLA_HEADS
    per_b = lambda shp: pl.BlockSpec((1,) + shp, lambda b, pt: (b, 0, 0))
    full = lambda arr: pl.BlockSpec(arr.shape, lambda b, pt: (0,) * arr.ndim)
    in_hbm = pl.BlockSpec(memory_space=pl.ANY)
    grid_spec = pltpu.PrefetchScalarGridSpec(
        num_scalar_prefetch=1,
        grid=(batch,),
        in_specs=[per_b((nq, MLA_KV_LORA)), per_b((nq, MLA_ROPE)), per_b((seq, MLA_KV_LORA)),
                  per_b((MLA_ROPE, seq)), full(p["w_uk_t"]), full(p["w_uv"]), in_hbm, in_hbm],
        out_specs=per_b((seq, MLA_HEADS * MLA_V)),
        scratch_shapes=[pltpu.VMEM((nq, 1), F32), pltpu.VMEM((nq, 1), F32), pltpu.VMEM((nq, MLA_KV_LORA), F32),
                        pltpu.VMEM((MLA_HEADS * MLA_NOPE + nq, MLA_KV_LORA), BF16),
                        pltpu.VMEM((2, pp, PAGE_SIZE, MLA_KV_LORA), F32),
                        pltpu.VMEM((2, pp, MLA_ROPE, PAGE_SIZE), F32),
                        pltpu.SemaphoreType.DMA((2, 2))],
    )
    return pl.pallas_call(
        functools.partial(_paged_attn_kernel, pp, seq, n_pages),
        grid_spec=grid_spec,
        out_shape=jax.ShapeDtypeStruct((batch, seq, MLA_HEADS * MLA_V), F32),
        compiler_params=_params("arbitrary"),
        name="paged_attn",
    )(page_table.reshape(-1), q_abs, qp, c_new.reshape(batch, seq, -1),
      jnp.swapaxes(kp_new.reshape(batch, seq, -1), 1, 2),
      p["w_uk_t"], p["w_uv"], pool_c, pool_kp_t)


def mla_layer_prompt(h, p, batch, seq, tm, tq):
    assert tm == tq, "the value tiles written by mla_proj are the key blocks of flash_prompt"
    qc, c, kp, kc, vt = mla_proj(h, jnp.arange(seq), p, tm, True)
    return flash_prompt(qc, kc, vt, batch, seq, tq), c, kp


def mla_layer_sample(h, pool_c, pool_kp, page_table, p, batch, seq, past_len, tm, pp):
    m = batch * seq
    qc, c, kp = mla_proj(h, past_len + jnp.arange(seq), p, tm, False)
    qc = qc.reshape(m, MLA_HEADS, SLOT)
    q_heads = jnp.swapaxes(qc[:, :, :MLA_NOPE], 0, 1)
    q_abs = bmm(q_heads, p["w_uk_heads"], BF16)
    q_abs = jnp.swapaxes(q_abs, 0, 1).reshape(batch, seq * MLA_HEADS, MLA_KV_LORA)
    qp = qc[:, :, MLA_NOPE:MLA_NOPE + MLA_ROPE].reshape(batch, seq * MLA_HEADS, MLA_ROPE)
    o = paged_attn(q_abs, qp, c, kp, pool_c, jnp.swapaxes(pool_kp, 1, 2), page_table, p, batch, seq, pp)
    return o.reshape(m, -1), c, kp


DT_PAD = 128
CONV_HALO = 8
MB_GN = MB_GROUPS * MB_STATE
MB_GROUP_INNER = MB_INNER // MB_GROUPS
MB_HEADS_PER_GROUP = MB_HEADS // MB_GROUPS


def _ssd_kernel(ck, xbc_ref, prev_ref, cs_ref, z_ref, dtr_ref, h0_ref, cw_ref, cb_ref, dtb_ref, alog_ref,
                dskip_ref, nw_ref, expand_ref, y_ref, hout_ref, h_ref):
    c = pl.program_id(1)

    @pl.when(c == 0)
    def _():
        h_ref[...] = h0_ref[0]

    halo = jnp.where(c == 0, cs_ref[0], prev_ref[0])
    xext = jnp.concatenate([halo, xbc_ref[0]], axis=0)
    conv = cb_ref[...] + xext[CONV_HALO:, :] * cw_ref[MB_CONV - 1:MB_CONV, :]
    for j in range(MB_CONV - 1):
        shifted = pltpu.roll(xext, MB_CONV - 1 - j, axis=0)[CONV_HALO:, :]
        conv = conv + shifted * cw_ref[j:j + 1, :]
    xbc = _silu(conv)
    xs = xbc[:, :MB_INNER]
    bm = xbc[:, MB_INNER:MB_INNER + MB_GN]
    cm = xbc[:, MB_INNER + MB_GN:]

    dt = _softplus(dtr_ref[0] + dtb_ref[...])
    da = dt * (-jnp.exp(alog_ref[...]))
    ri = lax.broadcasted_iota(jnp.int32, (ck, ck), 0)
    ci = lax.broadcasted_iota(jnp.int32, (ck, ck), 1)
    causal = ci <= ri
    tri = jnp.where(causal, 1.0, 0.0).astype(BF16)
    hi, mid, lo = _split3(da)
    acum = (jnp.dot(tri, hi, preferred_element_type=F32) + jnp.dot(tri, mid, preferred_element_type=F32)
            + jnp.dot(tri, lo, preferred_element_type=F32))
    tn = (((0,), (0,)), ((), ()))
    tri_t = jnp.where(ri <= ci, 1.0, 0.0).astype(BF16)
    acum_t = (lax.dot_general(hi, tri_t, tn, preferred_element_type=F32)
              + lax.dot_general(mid, tri_t, tn, preferred_element_type=F32)
              + lax.dot_general(lo, tri_t, tn, preferred_element_type=F32))
    e_last = jnp.exp(acum[ck - 1:ck, :])

    spread = expand_ref[...]
    dt_hi = dt.astype(BF16)
    dt_lo = (dt - dt_hi.astype(F32)).astype(BF16)
    dt_x = jnp.dot(dt_hi, spread, preferred_element_type=F32) + jnp.dot(dt_lo, spread, preferred_element_type=F32)
    ah, am, al = _split3(acum)
    acum_x = (jnp.dot(ah, spread, preferred_element_type=F32) + jnp.dot(am, spread, preferred_element_type=F32)
              + jnp.dot(al, spread, preferred_element_type=F32))
    xdt = xs * dt_x
    xdt_end = xdt * jnp.exp(acum_x[ck - 1:ck, :] - acum_x)
    e_cum_x = jnp.exp(acum_x)

    pair_w = 2 * MB_HEAD
    first = lax.broadcasted_iota(jnp.int32, (ck, pair_w), 1) < MB_HEAD
    upper = lax.broadcasted_iota(jnp.int32, (pair_w, MB_STATE), 0) < MB_HEAD
    pairs_per_group = MB_HEADS_PER_GROUP // 2
    ys = []
    for g in range(MB_GROUPS):
        b_g = bm[:, g * MB_STATE:(g + 1) * MB_STATE]
        c_g = cm[:, g * MB_STATE:(g + 1) * MB_STATE]
        cb = _dot_nt(c_g, b_g)
        cols = slice(g * MB_GROUP_INNER, (g + 1) * MB_GROUP_INNER)
        y_state = _dot_nt(c_g, h_ref[cols, :]) * e_cum_x[:, cols]
        lmat = []
        for hh in range(MB_HEADS_PER_GROUP):
            hd = g * MB_HEADS_PER_GROUP + hh
            seg = acum[:, hd:hd + 1] - acum_t[hd:hd + 1, :]
            lmat.append(cb * jnp.where(causal, jnp.exp(jnp.where(causal, seg, 0.0)), 0.0))
        for pr in range(pairs_per_group):
            lanes = slice(g * MB_GROUP_INNER + pr * pair_w, g * MB_GROUP_INNER + (pr + 1) * pair_w)
            x_p = xdt[:, lanes]
            y_p = jnp.where(first, _dot(lmat[2 * pr], x_p), _dot(lmat[2 * pr + 1], x_p))
            ys.append(y_p + y_state[:, pr * pair_w:(pr + 1) * pair_w])
            hd = g * MB_HEADS_PER_GROUP + 2 * pr
            decay = jnp.where(upper, e_last[:, hd:hd + 1], e_last[:, hd + 1:hd + 2])
            h_ref[lanes, :] = h_ref[lanes, :] * decay + _dot_tn(xdt_end[:, lanes], b_g)

    y = jnp.concatenate(ys, axis=1) + dskip_ref[...] * xs
    yz = y * _silu(z_ref[0])
    outs = []
    for g in range(MB_GROUPS):
        yg = yz[:, g * MB_GROUP_INNER:(g + 1) * MB_GROUP_INNER]
        outs.append(yg * lax.rsqrt(jnp.mean(yg * yg, axis=-1, keepdims=True) + NORM_EPS))
    y_ref[0] = (jnp.concatenate(outs, axis=1) * nw_ref[...]).astype(y_ref.dtype)

    @pl.when(c == pl.num_programs(1) - 1)
    def _():
        hout_ref[0] = h_ref[...]


def ssd_scan(xbc, z, dt_raw, conv_state, h0, p, batch, seq, ck):
    assert CONV_HALO % 8 == 0 and ck % CONV_HALO == 0
    nc = seq // ck
    halo_blocks = ck // CONV_HALO
    cs8 = jnp.concatenate([jnp.zeros((batch, CONV_HALO - (MB_CONV - 1), MB_CONV_DIM), F32), conv_state], axis=1)
    chunk = lambda width: pl.BlockSpec((1, ck, width), lambda b, c: (b, c, 0))
    full = lambda arr: pl.BlockSpec(arr.shape, lambda b, c: (0,) * arr.ndim)
    expand = (jnp.arange(DT_PAD)[:, None] == jnp.arange(MB_INNER)[None, :] // MB_HEAD).astype(BF16)
    weights = [p["conv_w"], p["conv_b"], p["dt_bias"], p["a_log"], p["d_skip"], p["norm_w"], expand]
    state = pl.BlockSpec((1, MB_INNER, MB_STATE), lambda b, c: (b, 0, 0))
    return pl.pallas_call(
        functools.partial(_ssd_kernel, ck),
        grid=(batch, nc),
        in_specs=[chunk(MB_CONV_DIM),
                  pl.BlockSpec((1, CONV_HALO, MB_CONV_DIM),
                               lambda b, c: (b, jnp.maximum(c * halo_blocks - 1, 0), 0)),
                  pl.BlockSpec((1, CONV_HALO, MB_CONV_DIM), lambda b, c: (b, 0, 0)),
                  chunk(MB_INNER), chunk(DT_PAD), state] + [full(w) for w in weights],
        out_specs=[chunk(MB_INNER), state],
        out_shape=[jax.ShapeDtypeStruct((batch, seq, MB_INNER), BF16),
                   jax.ShapeDtypeStruct((batch, MB_INNER, MB_STATE), F32)],
        scratch_shapes=[pltpu.VMEM((MB_INNER, MB_STATE), F32)],
        compiler_params=_params("parallel", "arbitrary"),
        name="ssd_scan",
    )(xbc, xbc, cs8, z, dt_raw, h0, *weights)


def prep_mamba(w, j, i):
    row = lambda z: z.reshape(1, -1).astype(F32)
    w_in = w["mamba_w_in"][j]
    padv = lambda z: jnp.concatenate([z.astype(F32), jnp.zeros((DT_PAD - MB_HEADS,), F32)])
    return {
        "norm_mix": w["norm_mix"][i],
        "w_z": w_in[:, :MB_INNER].astype(BF16),
        "w_xbc": w_in[:, MB_INNER:MB_INNER + MB_CONV_DIM].astype(BF16),
        "w_dt": jnp.concatenate([w_in[:, MB_INNER + MB_CONV_DIM:], jnp.zeros((D_MODEL, DT_PAD - MB_HEADS), F32)],
                                axis=1).astype(BF16),
        "conv_w": w["mamba_conv_w"][j], "conv_b": row(w["mamba_conv_b"][j]),
        "dt_bias": row(padv(w["mamba_dt_bias"][j])), "a_log": row(padv(w["mamba_a_log"][j])),
        "d_skip": row(jnp.repeat(w["mamba_d"][j].astype(F32), MB_HEAD)),
        "norm_w": row(w["mamba_norm"][j]),
        "w_o": w["mamba_w_o"][j].astype(BF16),
    }


def mamba_layer(h, conv_state, h0, p, batch, seq, ck):
    m = batch * seq
    assert seq >= MB_CONV - 1
    tm_in = min(TM_FFN, m)
    z = norm_linear(h, p["norm_mix"], p["w_z"], tm_in, TN_IN)
    xbc = norm_linear(h, p["norm_mix"], p["w_xbc"], tm_in, TN_IN)
    dt_raw = norm_linear(h, p["norm_mix"], p["w_dt"], tm_in, DT_PAD)
    xbc3 = xbc.reshape(batch, seq, MB_CONV_DIM)
    y, h_new = ssd_scan(xbc3, z.reshape(batch, seq, MB_INNER), dt_raw.reshape(batch, seq, DT_PAD),
                        conv_state, h0.reshape(batch, MB_INNER, MB_STATE), p, batch, seq, ck)
    return (y.reshape(m, MB_INNER), xbc3[:, seq - (MB_CONV - 1):],
            h_new.reshape(batch, MB_HEADS, MB_HEAD, MB_STATE))


N_MIXERS = 3
TM_PROJ = 256
TM_ROWS = 512
TM_FFN = 1024
TH_FFN = 1024
TN_IN = 1024
WKV_CHUNK = 64
WKV_PAIRS_PER_STEP = 8
ATTN_TQ = 256
PAGES_PER_STEP = 32


def kernel(x_prompt, x_sample, cache_mla_ckv, cache_mla_kpe, state_rwkv_wkv, state_rwkv_shift, state_ssm, state_conv, page_table, norm_mix, norm_ffn, ffn_w1, ffn_w2, rwkv_mu, rwkv_w_rkv, rwkv_w0, rwkv_w1, rwkv_w2, rwkv_a0, rwkv_a1, rwkv_a2, rwkv_v0, rwkv_v1, rwkv_v2, rwkv_g1, rwkv_g2, rwkv_k_k, rwkv_k_a, rwkv_r_k, rwkv_lnx_w, rwkv_lnx_b, rwkv_w_o, mla_w_in, mla_q_norm, mla_kv_norm, mla_w_uq, mla_w_uk, mla_w_uv, mla_qn_norm, mla_qr_norm, mla_kn_norm, mla_kr_norm, mla_w_o, mamba_w_in, mamba_conv_w, mamba_conv_b, mamba_dt_bias, mamba_a_log, mamba_d, mamba_norm, mamba_w_o):
    w = dict(locals())
    bp, tp, c = x_prompt.shape
    bs, ts, _ = x_sample.shape
    depth = norm_mix.shape[0]
    past_len = page_table.shape[1] * PAGE_SIZE
    hp = x_prompt.reshape(bp * tp, c)
    hs = x_sample.reshape(bs * ts, c)
    vf_p = vf_s = None
    out = {k: [] for k in ("ckv_p", "kpe_p", "ckv_s", "kpe_s", "wkv_p", "sh_p", "wkv_s", "sh_s",
                           "ssm_p", "conv_p", "ssm_s", "conv_s")}
    for i in range(depth):
        kind, j = i % N_MIXERS, i // N_MIXERS
        if kind == 0:
            p = prep_rwkv(w, j, i)
            xp, s_p, l_p, vf_p = rwkv_layer(hp, jnp.zeros((bp, c), F32), jnp.zeros((bp, RW_HEADS, RW_HEAD, RW_HEAD), F32),
                                            vf_p, p, bp, tp, TM_ROWS, WKV_PAIRS_PER_STEP, WKV_CHUNK)
            xs, s_s, l_s, vf_s = rwkv_layer(hs, state_rwkv_shift[j], state_rwkv_wkv[j], vf_s, p, bs, ts,
                                            TM_ROWS, WKV_PAIRS_PER_STEP, ts)
            out["wkv_p"].append(s_p); out["sh_p"].append(l_p); out["wkv_s"].append(s_s); out["sh_s"].append(l_s)
        elif kind == 1:
            p = prep_mla(w, j, i)
            xp, c_p, k_p = mla_layer_prompt(hp, p, bp, tp, TM_PROJ, ATTN_TQ)
            xs, c_s, k_s = mla_layer_sample(hs, cache_mla_ckv[j], cache_mla_kpe[j], page_table, p, bs, ts,
                                            past_len, TM_PROJ, PAGES_PER_STEP)
            out["ckv_p"].append(c_p.reshape(bp, tp, -1)); out["kpe_p"].append(k_p.reshape(bp, tp, -1))
            out["ckv_s"].append(c_s.reshape(bs, ts, -1)); out["kpe_s"].append(k_s.reshape(bs, ts, -1))
        else:
            p = prep_mamba(w, j, i)
            xp, cv_p, h_p = mamba_layer(hp, jnp.zeros((bp, MB_CONV - 1, MB_CONV_DIM), F32),
                                        jnp.zeros((bp, MB_HEADS, MB_HEAD, MB_STATE), F32), p, bp, tp, MB_CHUNK)
            xs, cv_s, h_s = mamba_layer(hs, state_conv[j], state_ssm[j], p, bs, ts, math.gcd(ts, MB_CHUNK))
            out["ssm_p"].append(h_p); out["conv_p"].append(cv_p); out["ssm_s"].append(h_s); out["conv_s"].append(cv_s)
        w1, w2 = ffn_w1[i].astype(BF16), ffn_w2[i].astype(BF16)
        hp = mixer_ffn_res(hp, xp, p["w_o"], norm_ffn[i], w1, w2, TM_FFN, TH_FFN)
        hs = mixer_ffn_res(hs, xs, p["w_o"], norm_ffn[i], w1, w2, TM_FFN, TH_FFN)
    stack = lambda k: jnp.stack(out[k])
    return (hp.reshape(bp, tp, c), hs.reshape(bs, ts, c),
            stack("ckv_p"), stack("kpe_p"), stack("ckv_s"), stack("kpe_s"),
            stack("wkv_p"), stack("sh_p"), stack("wkv_s"), stack("sh_s"),
            stack("ssm_p"), stack("conv_p"), stack("ssm_s"), stack("conv_s"))
```
